```python
import jax, jax.numpy as jnp
from jax import lax
import numpy as np

D_MODEL = 2048
BATCH = 16
SEQ = 2048
DEPTH = 1
DEC_BATCH = 32
DEC_SEQ = 8
PAST_LEN = 16384
PAGE_SIZE = 128

D_PLE = 256
D_FF = 5632
POOL_WIDTH = 1024
POOL_WINDOWS = (2, 4, 8, 16)
POOL_GROUPS = len(POOL_WINDOWS)
POOL_GROUP_WIDTH = POOL_WIDTH // POOL_GROUPS
POOL_STATE = max(POOL_WINDOWS) - 1
HEAD_DIM = 128
HEADS_PER_GROUP = 8
ATTN_GROUPS = ((128, 1), (512, 4), (2048, 16))
N_ATTN_GROUPS = len(ATTN_GROUPS)
ATTN_HEADS = N_ATTN_GROUPS * HEADS_PER_GROUP
QKV_WIDTH = ATTN_HEADS * HEAD_DIM
ATTN_OUT_WIDTH = HEADS_PER_GROUP * HEAD_DIM
BAND_BLOCK = 128
IN_WIDTH = POOL_WIDTH + 3 * QKV_WIDTH + 2 * D_MODEL
RMS_EPS = 1e-6
NEG_INF = -1e30

kernel_name = "hybrid_pool_dilated_attn_macaron_decoder_step"


def rmsnorm(x, g):
    x32 = x.astype(jnp.float32)
    y = x32 * lax.rsqrt(jnp.mean(x32 * x32, axis=-1, keepdims=True) + RMS_EPS) * g.astype(jnp.float32)
    return y.astype(x.dtype)


def swiglu(x, w_in, w_out):
    a, b = jnp.split(x @ w_in, 2, axis=-1)
    return (jax.nn.silu(a) * b) @ w_out


def alibi_slopes():
    j = jnp.arange(1, HEADS_PER_GROUP + 1, dtype=jnp.float32)
    return jnp.exp2(-8.0 * j / HEADS_PER_GROUP)


def causal_multiscale_pool(z, prev, pos0):
    B, T, P = z.shape
    ext = jnp.concatenate([prev, z], axis=1)
    ext32 = ext.astype(jnp.float32)
    cs = jnp.concatenate([jnp.zeros((B, 1, P), jnp.float32), jnp.cumsum(ext32, axis=1)], axis=1)
    S = POOL_STATE
    pos = pos0 + jnp.arange(T)
    outs = []
    for gi, w in enumerate(POOL_WINDOWS):
        sl = slice(gi * POOL_GROUP_WIDTH, (gi + 1) * POOL_GROUP_WIDTH)
        wsum = cs[:, S + 1:S + 1 + T, sl] - cs[:, S + 1 - w:S + 1 - w + T, sl]
        cnt = jnp.minimum(w, pos + 1).astype(jnp.float32)
        outs.append(wsum / cnt[None, :, None] - ext32[:, S:, sl])
    return jnp.stack(outs, axis=2), ext


def dilated_band_attn(q, k, v, dil, n_steps, slopes):
    B, T, H, hd = q.shape
    M = T // dil
    C = BAND_BLOCK
    nb = -(-M // C)
    Mp = nb * C

    def to_res(a):
        return a.reshape(B, M, dil, H, hd).transpose(0, 2, 1, 3, 4)

    qr, kr, vr = to_res(q), to_res(k), to_res(v)
    qb = jnp.pad(qr, ((0, 0), (0, 0), (0, Mp - M), (0, 0), (0, 0))).reshape(B, dil, nb, C, H, hd)
    kp = jnp.pad(kr, ((0, 0), (0, 0), (C, Mp - M), (0, 0), (0, 0))).reshape(B, dil, nb + 1, C, H, hd)
    vp = jnp.pad(vr, ((0, 0), (0, 0), (C, Mp - M), (0, 0), (0, 0))).reshape(B, dil, nb + 1, C, H, hd)
    kwin = jnp.concatenate([kp[:, :, :-1], kp[:, :, 1:]], axis=3)
    vwin = jnp.concatenate([vp[:, :, :-1], vp[:, :, 1:]], axis=3)
    s = jnp.einsum('bgnqhd,bgnkhd->bgnhqk', qb.astype(jnp.float32), kwin.astype(jnp.float32)) * (HEAD_DIM ** -0.5)
    iq = jnp.arange(C)[:, None]
    jk = jnp.arange(2 * C)[None, :]
    steps = C + iq - jk
    m_k = jnp.arange(nb)[:, None, None] * C - C + jk[None]
    valid = (steps >= 0) & (steps <= n_steps) & (m_k >= 0)
    bias = -slopes[:, None, None] * (steps * dil).astype(jnp.float32)[None]
    logits = jnp.where(valid[None, None, :, None], s + bias[None, None, None], NEG_INF)
    mx = jnp.max(logits, axis=-1, keepdims=True)
    p = jnp.exp(logits - mx)
    den = jnp.sum(p, axis=-1, keepdims=True)
    o = jnp.einsum('bgnhqk,bgnkhd->bgnqhd', p, vwin.astype(jnp.float32))
    o = o / jnp.swapaxes(den[..., 0], 3, 4)[..., None]
    lse = jnp.swapaxes((mx + jnp.log(den))[..., 0], 3, 4)
    o = o.reshape(B, dil, Mp, H, hd)[:, :, :M].transpose(0, 2, 1, 3, 4).reshape(B, T, H, hd)
    lse = lse.reshape(B, dil, Mp, H)[:, :, :M].transpose(0, 2, 1, 3).reshape(B, T, H)
    return o, lse


def dilated_gather_attn(q, k_ext, v_ext, dil, n_steps, slopes):
    B, T, H, hd = q.shape
    L = k_ext.shape[1] - T
    j = jnp.arange(n_steps + 1)
    idx = L + jnp.arange(T)[:, None] - j[None, :] * dil
    valid = idx >= 0
    idx_c = jnp.maximum(idx, 0)
    kg = jnp.take(k_ext, idx_c, axis=1).astype(jnp.float32)
    vg = jnp.take(v_ext, idx_c, axis=1).astype(jnp.float32)
    s = jnp.einsum('bthd,btjhd->bthj', q.astype(jnp.float32), kg) * (HEAD_DIM ** -0.5)
    bias = -slopes[:, None] * (j * dil).astype(jnp.float32)[None, :]
    logits = jnp.where(valid[None, :, None, :], s + bias[None, None], NEG_INF)
    mx = jnp.max(logits, axis=-1, keepdims=True)
    p = jnp.exp(logits - mx)
    den = jnp.sum(p, axis=-1, keepdims=True)
    o = jnp.einsum('bthj,btjhd->bthd', p, vg) / den
    lse = (mx + jnp.log(den))[..., 0]
    return o, lse


def layer(x, p, pool_prev, kv_prev, pos0, ffn1_norm, ffn1_w_in, ffn1_w_out, mix_norm, w_in, q_norm, k_norm,
          pool_map, pool_scale, w_pool_out, w_attn_out, w_out, ffn2_norm, ffn2_w_in, ffn2_w_out,
          ple_norm, ple_gate, ple_proj):
    B, T, _ = x.shape
    h = x + 0.5 * swiglu(rmsnorm(x, ffn1_norm), ffn1_w_in, ffn1_w_out)
    u = rmsnorm(h, mix_norm)
    i1 = POOL_WIDTH
    i2 = i1 + QKV_WIDTH
    i3 = i2 + QKV_WIDTH
    i4 = i3 + QKV_WIDTH
    z, q, k, v, gates = jnp.split(u @ w_in, [i1, i2, i3, i4], axis=-1)
    pooled, pool_ext = causal_multiscale_pool(z, pool_prev, pos0)
    y_pool = jnp.einsum('btgc,gcd->btgd', pooled.astype(x.dtype), pool_map).reshape(B, T, POOL_WIDTH) * pool_scale
    a_pool = y_pool @ w_pool_out
    q = rmsnorm(q.reshape(B, T, N_ATTN_GROUPS, HEADS_PER_GROUP, HEAD_DIM), q_norm)
    k = rmsnorm(k.reshape(B, T, N_ATTN_GROUPS, HEADS_PER_GROUP, HEAD_DIM), k_norm)
    v = v.reshape(B, T, N_ATTN_GROUPS, HEADS_PER_GROUP, HEAD_DIM)
    slopes = alibi_slopes()
    outs, lses, new_kv = [], [], []
    for g, (window, dil) in enumerate(ATTN_GROUPS):
        n_steps = window // dil
        qg, kg, vg = q[:, :, g], k[:, :, g], v[:, :, g]
        if kv_prev is None:
            o, lse = dilated_band_attn(qg, kg, vg, dil, n_steps, slopes)
            keep = min(window, T)
            new_kv += [kg[:, T - keep:], vg[:, T - keep:]]
        else:
            k_ext = jnp.concatenate([kv_prev[2 * g], kg], axis=1)
            v_ext = jnp.concatenate([kv_prev[2 * g + 1], vg], axis=1)
            o, lse = dilated_gather_attn(qg, k_ext, v_ext, dil, n_steps, slopes)
            keep = min(window, k_ext.shape[1])
            new_kv += [k_ext[:, k_ext.shape[1] - keep:], v_ext[:, v_ext.shape[1] - keep:]]
        outs.append(o)
        lses.append(lse)
    wts = jax.nn.softmax(jnp.stack(lses), axis=0)
    o_attn = jnp.einsum('gbth,gbthd->bthd', wts, jnp.stack(outs)).reshape(B, T, ATTN_OUT_WIDTH).astype(x.dtype)
    a_attn = o_attn @ w_attn_out
    g_pool, g_attn = jnp.split(jax.nn.sigmoid(gates), 2, axis=-1)
    h = h + (g_pool * a_pool + g_attn * a_attn) @ w_out
    h = h + 0.5 * swiglu(rmsnorm(h, ffn2_norm), ffn2_w_in, ffn2_w_out)
    h = h + jax.nn.sigmoid(rmsnorm(h, ple_norm) @ ple_gate) * (p @ ple_proj)
    new_pool = pool_ext[:, pool_ext.shape[1] - POOL_STATE:]
    return h, new_pool, new_kv


def setup_inputs(seed: int = 0) -> dict:
    key = jax.random.key(seed)
    ks = iter(jax.random.split(key, 48))

    def nrm(shape, scale=1.0):
        return scale * jax.random.normal(next(ks), shape, jnp.float32)

    def gain(shape):
        return 1.0 + 0.1 * jax.random.normal(next(ks), shape, jnp.float32)

    d = {}
    d["x_prompt"] = nrm((BATCH, SEQ, D_MODEL))
    d["x_sample"] = nrm((DEC_BATCH, DEC_SEQ, D_MODEL))
    d["p_prompt"] = nrm((DEPTH, BATCH, SEQ, D_PLE))
    d["p_sample"] = nrm((DEPTH, DEC_BATCH, DEC_SEQ, D_PLE))
    d["state_pool"] = nrm((DEPTH, DEC_BATCH, POOL_STATE, POOL_WIDTH))
    for g, (window, _) in enumerate(ATTN_GROUPS):
        L = min(window, PAST_LEN)
        d["cache_k%d" % g] = nrm((DEPTH, DEC_BATCH, L, HEADS_PER_GROUP, HEAD_DIM))
        d["cache_v%d" % g] = nrm((DEPTH, DEC_BATCH, L, HEADS_PER_GROUP, HEAD_DIM))
    d["ffn1_norm"] = gain((DEPTH, D_MODEL))
    d["ffn1_w_in"] = nrm((DEPTH, D_MODEL, 2 * D_FF), D_MODEL ** -0.5)
    d["ffn1_w_out"] = nrm((DEPTH, D_FF, D_MODEL), D_FF ** -0.5)
    d["mix_norm"] = gain((DEPTH, D_MODEL))
    d["w_in"] = nrm((DEPTH, D_MODEL, IN_WIDTH), D_MODEL ** -0.5)
    d["q_norm"] = gain((DEPTH, HEAD_DIM))
    d["k_norm"] = gain((DEPTH, HEAD_DIM))
    d["pool_map"] = nrm((DEPTH, POOL_GROUPS, POOL_GROUP_WIDTH, POOL_GROUP_WIDTH), POOL_GROUP_WIDTH ** -0.5)
    d["pool_scale"] = gain((DEPTH, POOL_WIDTH))
    d["w_pool_out"] = nrm((DEPTH, POOL_WIDTH, D_MODEL), POOL_WIDTH ** -0.5)
    d["w_attn_out"] = nrm((DEPTH, ATTN_OUT_WIDTH, D_MODEL), ATTN_OUT_WIDTH ** -0.5)
    d["w_out"] = nrm((DEPTH, D_MODEL, D_MODEL), D_MODEL ** -0.5)
    d["ffn2_norm"] = gain((DEPTH, D_MODEL))
    d["ffn2_w_in"] = nrm((DEPTH, D_MODEL, 2 * D_FF), D_MODEL ** -0.5)
    d["ffn2_w_out"] = nrm((DEPTH, D_FF, D_MODEL), D_FF ** -0.5)
    d["ple_norm"] = gain((DEPTH, D_MODEL))
    d["ple_gate"] = nrm((DEPTH, D_MODEL, D_MODEL), D_MODEL ** -0.5)
    d["ple_proj"] = nrm((DEPTH, D_PLE, D_MODEL), D_PLE ** -0.5)
    return d


def reference(x_prompt, x_sample, p_prompt, p_sample, state_pool, cache_k0, cache_v0, cache_k1, cache_v1,
              cache_k2, cache_v2, ffn1_norm, ffn1_w_in, ffn1_w_out, mix_norm, w_in, q_norm, k_norm, pool_map,
              pool_scale, w_pool_out, w_attn_out, w_out, ffn2_norm, ffn2_w_in, ffn2_w_out, ple_norm, ple_gate,
              ple_proj):
    caches = (cache_k0, cache_v0, cache_k1, cache_v1, cache_k2, cache_v2)
    hp, hs = x_prompt, x_sample
    pool_p_rows, pool_s_rows, kv_p_rows, kv_s_rows = [], [], [], []
    for i in range(DEPTH):
        w = (ffn1_norm[i], ffn1_w_in[i], ffn1_w_out[i], mix_norm[i], w_in[i], q_norm[i], k_norm[i], pool_map[i],
             pool_scale[i], w_pool_out[i], w_attn_out[i], w_out[i], ffn2_norm[i], ffn2_w_in[i], ffn2_w_out[i],
             ple_norm[i], ple_gate[i], ple_proj[i])
        zeros_prev = jnp.zeros((hp.shape[0], POOL_STATE, POOL_WIDTH), hp.dtype)
        hp, pp, kvp = layer(hp, p_prompt[i], zeros_prev, None, 0, *w)
        hs, ps, kvs = layer(hs, p_sample[i], state_pool[i], tuple(c[i] for c in caches), PAST_LEN, *w)
        pool_p_rows.append(pp)
        pool_s_rows.append(ps)
        kv_p_rows.append(kvp)
        kv_s_rows.append(kvs)
    pool_p = jnp.stack(pool_p_rows)
    pool_s = jnp.stack(pool_s_rows)
    kp = [jnp.stack([r[j] for r in kv_p_rows]) for j in range(2 * N_ATTN_GROUPS)]
    kvs_ = [jnp.stack([r[j] for r in kv_s_rows]) for j in range(2 * N_ATTN_GROUPS)]
    return (hp, hs, pool_p, kp[0], kp[1], kp[2], kp[3], kp[4], kp[5],
            pool_s, kvs_[0], kvs_[1], kvs_[2], kvs_[3], kvs_[4], kvs_[5])
```

```python
import functools

import jax
import jax.numpy as jnp
from jax import lax
from jax.experimental import pallas as pl
from jax.experimental.pallas import tpu as pltpu

F32 = jnp.float32
BF16 = jnp.bfloat16

POOL_WINDOWS = (2, 4, 8, 16)
POOL_GROUP_WIDTH = 256
POOL_WIDTH = POOL_GROUP_WIDTH * len(POOL_WINDOWS)
POOL_STATE = max(POOL_WINDOWS) - 1
POOL_HALO = 16
HEAD_DIM = 128
HEADS = 8
GROUP_WIDTH = HEADS * HEAD_DIM
ATTN_GROUPS = ((128, 1), (512, 4), (2048, 16))
N_GROUPS = len(ATTN_GROUPS)
BAND = 128
LSE_LANES = 128
ATTN_OUT_COLS = GROUP_WIDTH + LSE_LANES
RMS_EPS = 1e-6
NEG_INF = -1e30
PAST_LEN = 16384

VMEM_LIMIT = 56 * 1024 * 1024


def _params(*semantics):
    return pltpu.CompilerParams(dimension_semantics=semantics, vmem_limit_bytes=VMEM_LIMIT)


def _resident(shape, index_map):
    return pl.BlockSpec(shape, index_map, pipeline_mode=pl.Buffered(1))


def _rms(x, g):
    return x * lax.rsqrt(jnp.mean(x * x, axis=-1, keepdims=True) + RMS_EPS) * g


def _row_tile(m, want):
    t = min(m, want)
    assert m % t == 0, (m, t)
    return t


def _ffn_kernel(x_ref, g_ref, wa_ref, wb_ref, wo_ref, o_ref, xn_ref):
    @pl.when(pl.program_id(1) == 0)
    def _():
        x = x_ref[...]
        xn_ref[...] = _rms(x, g_ref[...]).astype(BF16)
        o_ref[...] = x

    xn = xn_ref[...]
    a = jnp.dot(xn, wa_ref[...], preferred_element_type=F32)
    b = jnp.dot(xn, wb_ref[...], preferred_element_type=F32)
    hid = (0.5 * (a * jax.nn.sigmoid(a)) * b).astype(BF16)
    o_ref[...] += jnp.dot(hid, wo_ref[...], preferred_element_type=F32)


def _ffn(x, g, w_in, w_out, *, tm=512, tf=512):
    m, d = x.shape
    d_ff = w_out.shape[0]
    tm = _row_tile(m, tm)
    tf = _row_tile(d_ff, tf)
    nk = d_ff // tf
    return pl.pallas_call(
        _ffn_kernel,
        grid=(m // tm, nk),
        in_specs=[
            pl.BlockSpec((tm, d), lambda i, k: (i, 0)),
            _resident((1, d), lambda i, k: (0, 0)),
            pl.BlockSpec((d, tf), lambda i, k: (0, k)),
            pl.BlockSpec((d, tf), lambda i, k: (0, k + nk)),
            pl.BlockSpec((tf, d), lambda i, k: (k, 0)),
        ],
        out_specs=pl.BlockSpec((tm, d), lambda i, k: (i, 0)),
        out_shape=jax.ShapeDtypeStruct((m, d), F32),
        scratch_shapes=[pltpu.VMEM((tm, d), BF16)],
        compiler_params=_params("parallel", "arbitrary"),
        name="ffn",
    )(x, g, w_in, w_in, w_out)


_Q0, _K0, _V0, _G0 = 1, 1 + N_GROUPS, 1 + 2 * N_GROUPS, 1 + 3 * N_GROUPS


def _head_rms(x, g):
    heads = []
    for hh in range(HEADS):
        xh = x[:, hh * HEAD_DIM:(hh + 1) * HEAD_DIM]
        heads.append(_rms(xh, g))
    return jnp.concatenate(heads, axis=-1)


def _inproj_kernel(h_ref, g_ref, w_ref, qn_ref, kn_ref,
                   z_ref, q_ref, kb_ref, vb_ref, kf_ref, vf_ref, gate_ref, xn_ref, acc_ref):
    j = pl.program_id(1)

    @pl.when(j == 0)
    def _():
        xn_ref[...] = _rms(h_ref[...], g_ref[...]).astype(BF16)

    acc_ref[...] = jnp.dot(xn_ref[...], w_ref[...], preferred_element_type=F32)

    @pl.when(j == 0)
    def _():
        z_ref[...] = acc_ref[...]

    @pl.when((j >= _Q0) & (j < _K0))
    def _():
        q = _head_rms(acc_ref[...], qn_ref[...]) * (HEAD_DIM ** -0.5)
        q_ref[...] = q.astype(BF16)

    @pl.when((j >= _K0) & (j < _V0))
    def _():
        k = _head_rms(acc_ref[...], kn_ref[...])
        kf_ref[...] = k
        kb_ref[...] = k.astype(BF16)

    @pl.when((j >= _V0) & (j < _G0))
    def _():
        v = acc_ref[...]
        vf_ref[...] = v
        vb_ref[...] = v.astype(BF16)

    @pl.when(j >= _G0)
    def _():
        gate_ref[...] = jax.nn.sigmoid(acc_ref[...]).astype(BF16)


def _inproj(h, g, w, qn, kn, *, tm=512):
    m, d = h.shape
    tn = GROUP_WIDTH
    n_tiles = w.shape[1] // tn
    n_gate = n_tiles - _G0
    tm = _row_tile(m, tm)

    def part(first, count):
        return lambda i, j: (i, jnp.clip(j - first, 0, count - 1))

    def part3(first, count):
        return lambda i, j: (jnp.clip(j - first, 0, count - 1), i, 0)

    return pl.pallas_call(
        _inproj_kernel,
        grid=(m // tm, n_tiles),
        in_specs=[
            pl.BlockSpec((tm, d), lambda i, j: (i, 0)),
            _resident((1, d), lambda i, j: (0, 0)),
            pl.BlockSpec((d, tn), lambda i, j: (0, j)),
            _resident((1, HEAD_DIM), lambda i, j: (0, 0)),
            _resident((1, HEAD_DIM), lambda i, j: (0, 0)),
        ],
        out_specs=[
            pl.BlockSpec((tm, tn), lambda i, j: (i, 0)),
            pl.BlockSpec((tm, tn), part(_Q0, N_GROUPS)),
            pl.BlockSpec((tm, tn), part(_K0, N_GROUPS)),
            pl.BlockSpec((tm, tn), part(_V0, N_GROUPS)),
            pl.BlockSpec((None, tm, tn), part3(_K0, N_GROUPS)),
            pl.BlockSpec((None, tm, tn), part3(_V0, N_GROUPS)),
            pl.BlockSpec((tm, tn), part(_G0, n_gate)),
        ],
        out_shape=[
            jax.ShapeDtypeStruct((m, tn), F32),
            jax.ShapeDtypeStruct((m, N_GROUPS * tn), BF16),
            jax.ShapeDtypeStruct((m, N_GROUPS * tn), BF16),
            jax.ShapeDtypeStruct((m, N_GROUPS * tn), BF16),
            jax.ShapeDtypeStruct((N_GROUPS, m, tn), F32),
            jax.ShapeDtypeStruct((N_GROUPS, m, tn), F32),
            jax.ShapeDtypeStruct((m, n_gate * tn), BF16),
        ],
        scratch_shapes=[pltpu.VMEM((tm, d), BF16), pltpu.VMEM((tm, tn), F32)],
        compiler_params=_params("parallel", "arbitrary"),
        name="inproj",
    )(h, g, w, qn, kn)


def _pool_kernel(z_ref, halo_ref, pm_ref, ps_ref, y_ref, ext_ref, *, tp, pos0, zero_first_halo):
    i = pl.program_id(1)
    halo = halo_ref[...]
    if zero_first_halo:
        halo = jnp.where(i == 0, 0.0, halo)
    ext_ref[0:POOL_HALO, :] = halo
    ext_ref[POOL_HALO:, :] = z_ref[...]
    pos = pos0 + i * tp + lax.broadcasted_iota(jnp.int32, (tp, 1), 0)
    for gi, w in enumerate(POOL_WINDOWS):
        cols = slice(gi * POOL_GROUP_WIDTH, (gi + 1) * POOL_GROUP_WIDTH)
        zc = ext_ref[POOL_HALO:POOL_HALO + tp, cols]
        wsum = zc
        for back in range(1, w):
            wsum = wsum + ext_ref[POOL_HALO - back:POOL_HALO - back + tp, cols]
        cnt = jnp.minimum(w, pos + 1).astype(F32)
        pooled = (wsum / cnt - zc).astype(BF16)
        y = jnp.dot(pooled, pm_ref[gi], preferred_element_type=F32) * ps_ref[:, cols]
        y_ref[:, cols] = y.astype(BF16)


def _pool(z, halo, pool_map, pool_scale, *, tp, pos0, zero_first_halo):
    b, t, p = z.shape
    tp = _row_tile(t, tp)
    hb = tp // POOL_HALO
    if halo is z:
        halo_map = lambda bi, i: (bi, jnp.maximum(i * hb - 1, 0), 0)
    else:
        assert t == tp and halo.shape == (b, POOL_HALO, p)
        halo_map = lambda bi, i: (bi, 0, 0)
    return pl.pallas_call(
        functools.partial(_pool_kernel, tp=tp, pos0=pos0, zero_first_halo=zero_first_halo),
        grid=(b, t // tp),
        in_specs=[
            pl.BlockSpec((None, tp, p), lambda bi, i: (bi, i, 0)),
            pl.BlockSpec((None, POOL_HALO, p), halo_map),
            _resident(pool_map.shape, lambda bi, i: (0, 0, 0)),
            _resident((1, p), lambda bi, i: (0, 0)),
        ],
        out_specs=pl.BlockSpec((None, tp, p), lambda bi, i: (bi, i, 0)),
        out_shape=jax.ShapeDtypeStruct((b, t, p), BF16),
        scratch_shapes=[pltpu.VMEM((POOL_HALO + tp, p), F32)],
        compiler_params=_params("parallel", "arbitrary"),
        name="pool",
    )(z, halo, pool_map, pool_scale)


def _band_attn_kernel(q_ref, k_ref, v_ref, o_ref, *, tq, kw, dil):
    nblk = tq // BAND
    i = pl.program_id(2)

    def body(nl, carry):
        n = i * nblk + nl
        if kw == BAND:
            kstart, off = 0, 0
        else:
            kstart = pl.multiple_of(jnp.maximum(n - 1, 0) * BAND, BAND)
            off = n * BAND - kstart
        qstart = pl.multiple_of(nl * BAND, BAND)
        steps = (off + lax.broadcasted_iota(jnp.int32, (BAND, kw), 0)
                 - lax.broadcasted_iota(jnp.int32, (BAND, kw), 1))
        valid = jnp.abs(steps - BAND // 2) <= BAND // 2
        dist = (steps * dil).astype(F32)
        lane = lax.broadcasted_iota(jnp.int32, (BAND, LSE_LANES), 1)
        lse_tile = jnp.zeros((BAND, LSE_LANES), F32)
        for hh in range(HEADS):
            cols = slice(hh * HEAD_DIM, (hh + 1) * HEAD_DIM)
            qh = q_ref[pl.ds(qstart, BAND), cols]
            kh = k_ref[pl.ds(kstart, kw), cols]
            vh = v_ref[pl.ds(kstart, kw), cols]
            s = lax.dot_general(qh, kh, (((1,), (1,)), ((), ())), preferred_element_type=F32)
            logits = jnp.where(valid, s - (2.0 ** -(hh + 1)) * dist, NEG_INF)
            mx = jnp.max(logits, axis=-1, keepdims=True)
            p = jnp.exp(logits - mx)
            den = jnp.sum(p, axis=-1, keepdims=True)
            o = jnp.dot(p.astype(BF16), vh, preferred_element_type=F32) / den
            o_ref[pl.ds(qstart, BAND), cols] = o
            lse_tile = jnp.where(lane == hh, mx + jnp.log(den), lse_tile)
        o_ref[pl.ds(qstart, BAND), GROUP_WIDTH:] = lse_tile
        return carry

    lax.fori_loop(0, nblk, body, 0)


def _band_attn(q, k, v, g, *, batch, seq):
    _, dil = ATTN_GROUPS[g]
    mr = seq // dil
    assert mr % BAND == 0
    tq = min(mr, 512)
    kw = min(mr, 2 * BAND)
    cols = N_GROUPS * GROUP_WIDTH
    view = lambda a: a.reshape(batch, mr, dil * cols)
    out = pl.pallas_call(
        functools.partial(_band_attn_kernel, tq=tq, kw=kw, dil=dil),
        grid=(batch, dil, mr // tq),
        in_specs=[
            pl.BlockSpec((None, tq, GROUP_WIDTH), lambda b, r, i: (b, i, r * N_GROUPS + g)),
            pl.BlockSpec((None, mr, GROUP_WIDTH), lambda b, r, i: (b, 0, r * N_GROUPS + g)),
            pl.BlockSpec((None, mr, GROUP_WIDTH), lambda b, r, i: (b, 0, r * N_GROUPS + g)),
        ],
        out_specs=pl.BlockSpec((None, tq, ATTN_OUT_COLS), lambda b, r, i: (b, i, r)),
        out_shape=jax.ShapeDtypeStruct((batch, mr, dil * ATTN_OUT_COLS), F32),
        compiler_params=_params("parallel", "parallel", "arbitrary"),
        name="band_attn_g%d" % g,
    )(view(q), view(k), view(v))
    return out.reshape(batch * seq, ATTN_OUT_COLS)


def _gather_attn_kernel(q_ref, kc_ref, vc_ref, kn_ref, vn_ref, o_ref, kx_ref, vx_ref, *, win, dil, t_new):
    length = kc_ref.shape[0]
    pad = kx_ref.shape[0] - length
    tail = jnp.zeros((pad - t_new, GROUP_WIDTH), BF16)
    kx_ref[0:length, :] = kc_ref[...].astype(BF16)
    kx_ref[length:, :] = jnp.concatenate([kn_ref[...].astype(BF16), tail], axis=0)
    vx_ref[0:length, :] = vc_ref[...].astype(BF16)
    vx_ref[length:, :] = jnp.concatenate([vn_ref[...].astype(BF16), tail], axis=0)

    rows = HEADS * t_new
    t_shift = t_new.bit_length() - 1
    q = q_ref[...].astype(F32)
    qrep = jnp.concatenate([q] * HEADS, axis=0)
    rr = lax.broadcasted_iota(jnp.int32, (rows, GROUP_WIDTH), 0)
    cc = lax.broadcasted_iota(jnp.int32, (rows, GROUP_WIDTH), 1)
    qbd = jnp.where((rr >> t_shift) == (cc >> 7), qrep, 0.0).astype(BF16)

    s = lax.dot_general(qbd, kx_ref[...], (((1,), (1,)), ((), ())), preferred_element_type=F32)
    width = length + pad
    r = lax.broadcasted_iota(jnp.int32, (rows, width), 0)
    col = lax.broadcasted_iota(jnp.int32, (rows, width), 1)
    dist = length + (r & (t_new - 1)) - col
    on_stride = jnp.where((dist & (dil - 1)) == 0, dist, -1)
    valid = jnp.abs(on_stride - win // 2) <= win // 2
    r1 = lax.broadcasted_iota(jnp.int32, (rows, 1), 0)
    slope = lax.bitcast_convert_type((126 - (r1 >> t_shift)) << 23, F32)
    logits = jnp.where(valid, s - slope * dist.astype(F32), NEG_INF)
    mx = jnp.max(logits, axis=-1, keepdims=True)
    p = jnp.exp(logits - mx)
    den = jnp.sum(p, axis=-1, keepdims=True)
    o = jnp.dot(p.astype(BF16), vx_ref[...], preferred_element_type=F32) / den
    lse = mx + jnp.log(den)

    lane = lax.broadcasted_iota(jnp.int32, (t_new, LSE_LANES), 1)
    lse_tile = jnp.zeros((t_new, LSE_LANES), F32)
    for hh in range(HEADS):
        cols = slice(hh * HEAD_DIM, (hh + 1) * HEAD_DIM)
        o_ref[:, cols] = o[hh * t_new:(hh + 1) * t_new, cols]
        lse_tile = jnp.where(lane == hh, lse[hh * t_new:(hh + 1) * t_new], lse_tile)
    o_ref[:, GROUP_WIDTH:] = lse_tile


def _gather_attn(q, kc, vc, kn, vn, g):
    win, dil = ATTN_GROUPS[g]
    b, length, _ = kc.shape
    t_new = q.shape[1]
    pad = 128
    return pl.pallas_call(
        functools.partial(_gather_attn_kernel, win=win, dil=dil, t_new=t_new),
        grid=(b,),
        in_specs=[
            pl.BlockSpec((None, t_new, GROUP_WIDTH), lambda bi: (bi, 0, g)),
            pl.BlockSpec((None, length, GROUP_WIDTH), lambda bi: (bi, 0, 0)),
            pl.BlockSpec((None, length, GROUP_WIDTH), lambda bi: (bi, 0, 0)),
            pl.BlockSpec((None, None, t_new, GROUP_WIDTH), lambda bi: (g, bi, 0, 0)),
            pl.BlockSpec((None, None, t_new, GROUP_WIDTH), lambda bi: (g, bi, 0, 0)),
        ],
        out_specs=pl.BlockSpec((None, t_new, ATTN_OUT_COLS), lambda bi: (bi, 0, 0)),
        out_shape=jax.ShapeDtypeStruct((b, t_new, ATTN_OUT_COLS), F32),
        scratch_shapes=[pltpu.VMEM((length + pad, GROUP_WIDTH), BF16),
                        pltpu.VMEM((length + pad, GROUP_WIDTH), BF16)],
        compiler_params=_params("parallel"),
        name="gather_attn_g%d" % g,
    )(q, kc, vc, kn, vn)


def _roll_kernel(kc_ref, kx_ref, kn_ref, vc_ref, vx_ref, vn_ref, ko_ref, vo_ref, *, t_new):
    last = pl.program_id(1) == pl.num_programs(1) - 1
    lc = kc_ref.shape[0]
    for c_ref, x_ref, n_ref, o_ref in ((kc_ref, kx_ref, kn_ref, ko_ref), (vc_ref, vx_ref, vn_ref, vo_ref)):
        o_ref[0:lc - t_new, :] = c_ref[t_new:, :]
        o_ref[lc - t_new:, :] = jnp.where(last, n_ref[...], x_ref[...])


def _roll_cache(kc, vc, kn, vn, g, *, lc=512):
    b, length, _ = kc.shape
    t_new = kn.shape[2]
    lc = min(lc, length)
    nchunk = length // lc
    per = lc // t_new
    main = pl.BlockSpec((None, lc, GROUP_WIDTH), lambda bi, j: (bi, j, 0))
    nxt = pl.BlockSpec((None, t_new, GROUP_WIDTH),
                       lambda bi, j: (bi, jnp.minimum((j + 1) * per, length // t_new - 1), 0))
    new = pl.BlockSpec((None, None, t_new, GROUP_WIDTH), lambda bi, j: (g, bi, 0, 0))
    return pl.pallas_call(
        functools.partial(_roll_kernel, t_new=t_new),
        grid=(b, nchunk),
        in_specs=[main, nxt, new, main, nxt, new],
        out_specs=[main, main],
        out_shape=[jax.ShapeDtypeStruct(kc.shape, F32), jax.ShapeDtypeStruct(vc.shape, F32)],
        compiler_params=_params("parallel", "arbitrary"),
        name="roll_cache_g%d" % g,
    )(kc, kc, kn, vc, vc, vn)


def _merge_kernel(h_ref, y_ref, o0_ref, o1_ref, o2_ref, gate_ref, wp_ref, wa_ref, wo_ref, out_ref, oa_ref):
    o_refs = (o0_ref, o1_ref, o2_ref)
    lses = [r[:, GROUP_WIDTH:] for r in o_refs]
    mx = jnp.maximum(jnp.maximum(lses[0], lses[1]), lses[2])
    es = [jnp.exp(l - mx) for l in lses]
    inv = 1.0 / (es[0] + es[1] + es[2])
    wts = [e * inv for e in es]
    for hh in range(HEADS):
        cols = slice(hh * HEAD_DIM, (hh + 1) * HEAD_DIM)
        acc = wts[0][:, hh:hh + 1] * o_refs[0][:, cols]
        for gi in range(1, N_GROUPS):
            acc = acc + wts[gi][:, hh:hh + 1] * o_refs[gi][:, cols]
        oa_ref[:, cols] = acc.astype(BF16)
    a_attn = jnp.dot(oa_ref[...], wa_ref[...], preferred_element_type=F32)
    a_pool = jnp.dot(y_ref[...], wp_ref[...], preferred_element_type=F32)
    d = out_ref.shape[1]
    mix = gate_ref[:, :d].astype(F32) * a_pool + gate_ref[:, d:].astype(F32) * a_attn
    out_ref[...] = h_ref[...] + jnp.dot(mix.astype(BF16), wo_ref[...], preferred_element_type=F32)


def _merge(h, y_pool, o_groups, gates, w_pool_out, w_attn_out, w_out, *, tm=256):
    m, d = h.shape
    tm = _row_tile(m, tm)
    row = lambda i: (i, 0)
    fixed = lambda i: (0, 0)
    return pl.pallas_call(
        _merge_kernel,
        grid=(m // tm,),
        in_specs=[
            pl.BlockSpec((tm, d), row),
            pl.BlockSpec((tm, POOL_WIDTH), row),
            pl.BlockSpec((tm, ATTN_OUT_COLS), row),
            pl.BlockSpec((tm, ATTN_OUT_COLS), row),
            pl.BlockSpec((tm, ATTN_OUT_COLS), row),
            pl.BlockSpec((tm, 2 * d), row),
            _resident(w_pool_out.shape, fixed),
            _resident(w_attn_out.shape, fixed),
            _resident(w_out.shape, fixed),
        ],
        out_specs=pl.BlockSpec((tm, d), row),
        out_shape=jax.ShapeDtypeStruct((m, d), F32),
        scratch_shapes=[pltpu.VMEM((tm, GROUP_WIDTH), BF16)],
        compiler_params=_params("parallel"),
        name="merge",
    )(h, y_pool, *o_groups, gates, w_pool_out, w_attn_out, w_out)


def _ple_kernel(h_ref, p_ref, g_ref, wg_ref, wp_ref, out_ref):
    h = h_ref[...]
    hn = _rms(h, g_ref[...]).astype(BF16)
    gate = jax.nn.sigmoid(jnp.dot(hn, wg_ref[...], preferred_element_type=F32))
    emb = jnp.dot(p_ref[...].astype(BF16), wp_ref[...], preferred_element_type=F32)
    out_ref[...] = h + gate * emb


def _ple(h, p, g, w_gate, w_proj, *, tm=512):
    m, d = h.shape
    tm = _row_tile(m, tm)
    row = lambda i: (i, 0)
    fixed = lambda i: (0, 0)
    return pl.pallas_call(
        _ple_kernel,
        grid=(m // tm,),
        in_specs=[
            pl.BlockSpec((tm, d), row),
            pl.BlockSpec((tm, p.shape[1]), row),
            _resident((1, d), fixed),
            _resident(w_gate.shape, fixed),
            _resident(w_proj.shape, fixed),
        ],
        out_specs=pl.BlockSpec((tm, d), row),
        out_shape=jax.ShapeDtypeStruct((m, d), F32),
        compiler_params=_params("parallel"),
        name="ple",
    )(h, p, g, w_gate, w_proj)


def _layer(x, p, w, *, prompt, pool_prev=None, caches=None):
    b, t, d = x.shape
    m = b * t
    h1 = _ffn(x.reshape(m, d), w["ffn1_norm"], w["ffn1_w_in"], w["ffn1_w_out"])
    z, q, kb, vb, kf, vf, gates = _inproj(h1, w["mix_norm"], w["w_in"], w["q_norm"], w["k_norm"])
    z3 = z.reshape(b, t, POOL_WIDTH)
    if prompt:
        y_pool = _pool(z3, z3, w["pool_map"], w["pool_scale"], tp=512, pos0=0, zero_first_halo=True)
        new_pool = z3[:, t - POOL_STATE:]
        o_groups = [_band_attn(q, kb, vb, g, batch=b, seq=t) for g in range(N_GROUPS)]
        new_kv = []
        for g, (window, _) in enumerate(ATTN_GROUPS):
            keep = min(window, t)
            for full in (kf, vf):
                new_kv.append(full[g].reshape(b, t, HEADS, HEAD_DIM)[:, t - keep:])
    else:
        halo = jnp.pad(pool_prev, ((0, 0), (POOL_HALO - POOL_STATE, 0), (0, 0)))
        y_pool = _pool(z3, halo, w["pool_map"], w["pool_scale"], tp=t, pos0=PAST_LEN, zero_first_halo=False)
        new_pool = jnp.concatenate([pool_prev, z3], axis=1)[:, t:]
        q3 = q.reshape(b, t, N_GROUPS * GROUP_WIDTH)
        kn = kf.reshape(N_GROUPS, b, t, GROUP_WIDTH)
        vn = vf.reshape(N_GROUPS, b, t, GROUP_WIDTH)
        o_groups, new_kv = [], []
        for g in range(N_GROUPS):
            kc = caches[2 * g].reshape(b, -1, GROUP_WIDTH)
            vc = caches[2 * g + 1].reshape(b, -1, GROUP_WIDTH)
            o_groups.append(_gather_attn(q3, kc, vc, kn, vn, g).reshape(m, ATTN_OUT_COLS))
            for new in _roll_cache(kc, vc, kn, vn, g):
                new_kv.append(new.reshape(b, -1, HEADS, HEAD_DIM))
    h2 = _merge(h1, y_pool.reshape(m, POOL_WIDTH), o_groups, gates,
                w["w_pool_out"], w["w_attn_out"], w["w_out"])
    h3 = _ffn(h2, w["ffn2_norm"], w["ffn2_w_in"], w["ffn2_w_out"])
    h4 = _ple(h3, p.reshape(m, -1), w["ple_norm"], w["ple_gate"], w["ple_proj"])
    return h4.reshape(b, t, d), new_pool, new_kv


def kernel(x_prompt, x_sample, p_prompt, p_sample, state_pool, cache_k0, cache_v0, cache_k1, cache_v1, cache_k2, cache_v2, ffn1_norm, ffn1_w_in, ffn1_w_out, mix_norm, w_in, q_norm, k_norm, pool_map, pool_scale, w_pool_out, w_attn_out, w_out, ffn2_norm, ffn2_w_in, ffn2_w_out, ple_norm, ple_gate, ple_proj):
    caches = (cache_k0, cache_v0, cache_k1, cache_v1, cache_k2, cache_v2)
    depth = ffn1_norm.shape[0]
    hp, hs = x_prompt, x_sample
    pool_p, pool_s, kv_p, kv_s = [], [], [], []
    for i in range(depth):
        row = lambda a: a[i][None, :]
        w = dict(
            ffn1_norm=row(ffn1_norm), ffn1_w_in=ffn1_w_in[i].astype(BF16), ffn1_w_out=ffn1_w_out[i].astype(BF16),
            mix_norm=row(mix_norm), w_in=w_in[i].astype(BF16), q_norm=row(q_norm), k_norm=row(k_norm),
            pool_map=pool_map[i].astype(BF16), pool_scale=row(pool_scale),
            w_pool_out=w_pool_out[i].astype(BF16), w_attn_out=w_attn_out[i].astype(BF16),
            w_out=w_out[i].astype(BF16),
            ffn2_norm=row(ffn2_norm), ffn2_w_in=ffn2_w_in[i].astype(BF16), ffn2_w_out=ffn2_w_out[i].astype(BF16),
            ple_norm=row(ple_norm), ple_gate=ple_gate[i].astype(BF16), ple_proj=ple_proj[i].astype(BF16),
        )
        hp, pp, kvp = _layer(hp, p_prompt[i], w, prompt=True)
        hs, ps, kvs = _layer(hs, p_sample[i], w, prompt=False, pool_prev=state_pool[i],
                             caches=tuple(c[i] for c in caches))
        pool_p.append(pp)
        pool_s.append(ps)
        kv_p.append(kvp)
        kv_s.append(kvs)
    stack = lambda rows: jnp.stack(rows)
    kp = [stack([r[j] for r in kv_p]) for j in range(2 * N_GROUPS)]
    ks = [stack([r[j] for r in kv_s]) for j in range(2 * N_GROUPS)]
    return (hp, hs, stack(pool_p), *kp, stack(pool_s), *ks)
```

```python
import functools

import jax
import jax.numpy as jnp
from jax import lax
from jax.experimental import pallas as pl
from jax.experimental.pallas import tpu as pltpu

F32 = jnp.float32
BF16 = jnp.bfloat16

POOL_WINDOWS = (2, 4, 8, 16)
POOL_GROUP_WIDTH = 256
POOL_WIDTH = POOL_GROUP_WIDTH * len(POOL_WINDOWS)
POOL_STATE = max(POOL_WINDOWS) - 1
POOL_HALO = 16
HEAD_DIM = 128
HEADS = 8
GROUP_WIDTH = HEADS * HEAD_DIM
ATTN_GROUPS = ((128, 1), (512, 4), (2048, 16))
DILS = tuple(d for _, d in ATTN_GROUPS)
N_GROUPS = len(ATTN_GROUPS)
SLAB_OF = tuple(sum(1 for d in DILS[:g] if d > 1) for g in range(N_GROUPS))
N_SLABS = sum(1 for d in DILS if d > 1)
BAND = 128
LSE_LANES = 128
ATTN_OUT_COLS = GROUP_WIDTH + LSE_LANES
ATTN_OUT_TILES = ATTN_OUT_COLS // 128
RMS_EPS = 1e-6
NEG_INF = -1e30
PAST_LEN = 16384

VMEM_LIMIT = 56 * 1024 * 1024


def _params(*semantics):
    return pltpu.CompilerParams(dimension_semantics=semantics, vmem_limit_bytes=VMEM_LIMIT)


def _resident(shape, index_map):
    return pl.BlockSpec(shape, index_map, pipeline_mode=pl.Buffered(1))


def _rms(x, g):
    return x * lax.rsqrt(jnp.mean(x * x, axis=-1, keepdims=True) + RMS_EPS) * g


def _row_tile(m, want):
    t = min(m, want)
    assert m % t == 0, (m, t)
    return t


def _head_cols(hh):
    return slice(hh * HEAD_DIM, (hh + 1) * HEAD_DIM)


def _ffn_kernel(*refs, emit_norm):
    if emit_norm:
        x_ref, g_ref, wa_ref, wb_ref, wo_ref, gn_ref, o_ref, u_ref, xn_ref = refs
    else:
        x_ref, g_ref, wa_ref, wb_ref, wo_ref, o_ref, xn_ref = refs
    k = pl.program_id(1)

    @pl.when(k == 0)
    def _():
        x = x_ref[...]
        xn_ref[...] = _rms(x, g_ref[...]).astype(BF16)
        o_ref[...] = x

    xn = xn_ref[...]
    a = jnp.dot(xn, wa_ref[...], preferred_element_type=F32)
    b = jnp.dot(xn, wb_ref[...], preferred_element_type=F32)
    hid = (0.5 * (a * jax.nn.sigmoid(a)) * b).astype(BF16)
    o_ref[...] += jnp.dot(hid, wo_ref[...], preferred_element_type=F32)

    if emit_norm:
        @pl.when(k == pl.num_programs(1) - 1)
        def _():
            u_ref[...] = _rms(o_ref[...], gn_ref[...]).astype(BF16)


def _ffn(x, g, w_in, w_out, g_next=None, *, tm=512, tf=512):
    m, d = x.shape
    d_ff = w_out.shape[0]
    tm = _row_tile(m, tm)
    tf = _row_tile(d_ff, tf)
    nk = d_ff // tf
    emit_norm = g_next is not None
    row = pl.BlockSpec((tm, d), lambda i, k: (i, 0))
    vec = _resident((1, d), lambda i, k: (0, 0))
    in_specs = [row, vec,
                pl.BlockSpec((d, tf), lambda i, k: (0, k)),
                pl.BlockSpec((d, tf), lambda i, k: (0, k + nk)),
                pl.BlockSpec((tf, d), lambda i, k: (k, 0))]
    args = [x, g, w_in, w_in, w_out]
    out_specs, out_shape = row, jax.ShapeDtypeStruct((m, d), F32)
    if emit_norm:
        in_specs.append(vec)
        args.append(g_next)
        out_specs = [row, row]
        out_shape = [out_shape, jax.ShapeDtypeStruct((m, d), BF16)]
    return pl.pallas_call(
        functools.partial(_ffn_kernel, emit_norm=emit_norm),
        grid=(m // tm, nk),
        in_specs=in_specs,
        out_specs=out_specs,
        out_shape=out_shape,
        scratch_shapes=[pltpu.VMEM((tm, d), BF16)],
        compiler_params=_params("parallel", "arbitrary"),
        name="ffn",
    )(*args)


def _head_rms(x, g):
    return jnp.concatenate([_rms(x[:, _head_cols(hh)], g) for hh in range(HEADS)], axis=-1)


def _store_head_rows(ref, val):
    tm = val.shape[0]
    for hh in range(HEADS):
        ref[pl.ds(hh, tm, stride=HEADS), :] = val[:, _head_cols(hh)]


def _store_residue_major(ref, slab_ref, val, dil):
    if dil == 1:
        ref[0] = val.astype(BF16)
        return
    n = val.shape[0] // dil
    for hh in range(HEADS):
        slab_ref[hh] = val[:, _head_cols(hh)]
    for r in range(dil):
        for hh in range(HEADS):
            ref[r, :, _head_cols(hh)] = slab_ref[hh, pl.ds(r, n, stride=dil), :].astype(BF16)


def _tile(u, w_ref, t):
    return jnp.dot(u, w_ref[:, t * GROUP_WIDTH:(t + 1) * GROUP_WIDTH], preferred_element_type=F32)


def _proj_a_kernel(*refs, prompt):
    if prompt:
        u_ref, wzq_ref, wg_ref, qn_ref, z_ref, q0_ref, q1_ref, q2_ref, gate_ref, slab_ref = refs
        q_refs = (q0_ref, q1_ref, q2_ref)
    else:
        u_ref, wzq_ref, wg_ref, qn_ref, z_ref, q_ref, gate_ref = refs
    u = u_ref[...]
    z_ref[...] = _tile(u, wzq_ref, 0)
    for g in range(N_GROUPS):
        q = _head_rms(_tile(u, wzq_ref, 1 + g), qn_ref[...]) * (HEAD_DIM ** -0.5)
        if prompt:
            _store_residue_major(q_refs[g], slab_ref.at[SLAB_OF[g]], q, DILS[g])
        else:
            _store_head_rows(q_ref.at[g], q)
    for t in range(wg_ref.shape[1] // GROUP_WIDTH):
        gate = jax.nn.sigmoid(_tile(u, wg_ref, t))
        gate_ref[:, t * GROUP_WIDTH:(t + 1) * GROUP_WIDTH] = gate.astype(BF16)


def _proj_b_kernel(*refs, prompt):
    if prompt:
        (u_ref, w_ref, kn_ref, kb0_ref, kb1_ref, kb2_ref, vb0_ref, vb1_ref, vb2_ref,
         kf01_ref, kf2_ref, vf01_ref, vf2_ref, slab_ref) = refs
        kb_refs, vb_refs = (kb0_ref, kb1_ref, kb2_ref), (vb0_ref, vb1_ref, vb2_ref)
        kf_refs = (kf01_ref.at[0], kf01_ref.at[1], kf2_ref)
        vf_refs = (vf01_ref.at[0], vf01_ref.at[1], vf2_ref)
    else:
        u_ref, w_ref, kn_ref, kf_ref, vf_ref = refs
        kf_refs = [kf_ref.at[g] for g in range(N_GROUPS)]
        vf_refs = [vf_ref.at[g] for g in range(N_GROUPS)]
    u = u_ref[...]
    for g in range(N_GROUPS):
        k = _head_rms(_tile(u, w_ref, g), kn_ref[...])
        v = _tile(u, w_ref, N_GROUPS + g)
        _store_head_rows(kf_refs[g], k)
        _store_head_rows(vf_refs[g], v)
        if prompt:
            _store_residue_major(kb_refs[g], slab_ref.at[0, SLAB_OF[g]], k, DILS[g])
            _store_residue_major(vb_refs[g], slab_ref.at[1, SLAB_OF[g]], v, DILS[g])


def _residue_specs(batch, seq, tm, dtype):
    per_seq = seq // tm
    specs, shapes = [], []
    for dil in DILS:
        specs.append(pl.BlockSpec((None, dil, tm // dil, GROUP_WIDTH),
                                  lambda i: (i // per_seq, 0, i % per_seq, 0)))
        shapes.append(jax.ShapeDtypeStruct((batch, dil, seq // dil, GROUP_WIDTH), dtype))
    return specs, shapes


def _head_rows_spec(n_arrays, m, tm):
    if n_arrays is None:
        return (pl.BlockSpec((tm * HEADS, HEAD_DIM), lambda i: (i, 0)),
                jax.ShapeDtypeStruct((m * HEADS, HEAD_DIM), F32))
    return (pl.BlockSpec((n_arrays, tm * HEADS, HEAD_DIM), lambda i: (0, i, 0)),
            jax.ShapeDtypeStruct((n_arrays, m * HEADS, HEAD_DIM), F32))


def _proj_a(u, w_zq, w_gate, qn, *, batch, seq, prompt, tm=256):
    m, d = u.shape
    tm = _row_tile(seq if prompt else m, tm)
    fixed = lambda i: (0, 0)
    rows = lambda width: pl.BlockSpec((tm, width), lambda i: (i, 0))
    scratch = []
    if prompt:
        q_specs, q_shapes = _residue_specs(batch, seq, tm, BF16)
        scratch = [pltpu.VMEM((N_SLABS, HEADS, tm, HEAD_DIM), F32)]
    else:
        spec, shape = _head_rows_spec(N_GROUPS, m, tm)
        q_specs, q_shapes = [spec], [shape]
    return pl.pallas_call(
        functools.partial(_proj_a_kernel, prompt=prompt),
        grid=(m // tm,),
        in_specs=[rows(d), _resident(w_zq.shape, fixed), _resident(w_gate.shape, fixed),
                  _resident((1, HEAD_DIM), fixed)],
        out_specs=[rows(GROUP_WIDTH), *q_specs, rows(w_gate.shape[1])],
        out_shape=[jax.ShapeDtypeStruct((m, GROUP_WIDTH), F32), *q_shapes,
                   jax.ShapeDtypeStruct((m, w_gate.shape[1]), BF16)],
        scratch_shapes=scratch,
        compiler_params=_params("parallel"),
        name="proj_a",
    )(u, w_zq, w_gate, qn)


def _proj_b(u, w_kv, kn, *, batch, seq, prompt, tm=256):
    m, d = u.shape
    tm = _row_tile(seq if prompt else m, tm)
    fixed = lambda i: (0, 0)
    scratch = []
    if prompt:
        kb_specs, kb_shapes = _residue_specs(batch, seq, tm, BF16)
        pair_spec, pair_shape = _head_rows_spec(N_GROUPS - 1, m, tm)
        last_spec, last_shape = _head_rows_spec(None, m, tm)
        out_specs = [*kb_specs, *kb_specs, pair_spec, last_spec, pair_spec, last_spec]
        out_shape = [*kb_shapes, *kb_shapes, pair_shape, last_shape, pair_shape, last_shape]
        scratch = [pltpu.VMEM((2, N_SLABS, HEADS, tm, HEAD_DIM), F32)]
    else:
        spec, shape = _head_rows_spec(N_GROUPS, m, tm)
        out_specs, out_shape = [spec, spec], [shape, shape]
    return pl.pallas_call(
        functools.partial(_proj_b_kernel, prompt=prompt),
        grid=(m // tm,),
        in_specs=[pl.BlockSpec((tm, d), lambda i: (i, 0)), _resident(w_kv.shape, fixed),
                  _resident((1, HEAD_DIM), fixed)],
        out_specs=out_specs,
        out_shape=out_shape,
        scratch_shapes=scratch,
        compiler_params=_params("parallel"),
        name="proj_b",
    )(u, w_kv, kn)


def _pool_kernel(z_ref, halo_ref, pm_ref, ps_ref, y_ref, ext_ref, *, tp, pos0, zero_first_halo):
    i = pl.program_id(1)
    halo = halo_ref[...]
    if zero_first_halo:
        halo = jnp.where(i == 0, 0.0, halo)
    ext_ref[0:POOL_HALO, :] = halo
    ext_ref[POOL_HALO:, :] = z_ref[...]
    pos = pos0 + i * tp + lax.broadcasted_iota(jnp.int32, (tp, 1), 0)
    for gi, w in enumerate(POOL_WINDOWS):
        cols = slice(gi * POOL_GROUP_WIDTH, (gi + 1) * POOL_GROUP_WIDTH)
        zc = ext_ref[POOL_HALO:POOL_HALO + tp, cols]
        wsum = zc
        for back in range(1, w):
            wsum = wsum + ext_ref[POOL_HALO - back:POOL_HALO - back + tp, cols]
        cnt = jnp.minimum(w, pos + 1).astype(F32)
        pooled = (wsum / cnt - zc).astype(BF16)
        y = jnp.dot(pooled, pm_ref[gi], preferred_element_type=F32) * ps_ref[:, cols]
        y_ref[:, cols] = y.astype(BF16)


def _pool(z, halo, pool_map, pool_scale, *, tp, pos0, zero_first_halo):
    b, t, p = z.shape
    tp = _row_tile(t, tp)
    hb = tp // POOL_HALO
    if halo is z:
        halo_map = lambda bi, i: (bi, jnp.maximum(i * hb - 1, 0), 0)
    else:
        assert t == tp and halo.shape == (b, POOL_HALO, p)
        halo_map = lambda bi, i: (bi, 0, 0)
    return pl.pallas_call(
        functools.partial(_pool_kernel, tp=tp, pos0=pos0, zero_first_halo=zero_first_halo),
        grid=(b, t // tp),
        in_specs=[
            pl.BlockSpec((None, tp, p), lambda bi, i: (bi, i, 0)),
            pl.BlockSpec((None, POOL_HALO, p), halo_map),
            _resident(pool_map.shape, lambda bi, i: (0, 0, 0)),
            _resident((1, p), lambda bi, i: (0, 0)),
        ],
        out_specs=pl.BlockSpec((None, tp, p), lambda bi, i: (bi, i, 0)),
        out_shape=jax.ShapeDtypeStruct((b, t, p), BF16),
        scratch_shapes=[pltpu.VMEM((POOL_HALO + tp, p), F32)],
        compiler_params=_params("parallel", "arbitrary"),
        name="pool",
    )(z, halo, pool_map, pool_scale)


def _band_attn_kernel(q_ref, k_ref, v_ref, o_ref, *, nblk, kw, dil):
    rb = q_ref.shape[0]
    i = pl.program_id(2)

    def body(it, carry):
        rl, nl = it // nblk, it % nblk
        n = i * nblk + nl
        if kw == BAND:
            kstart, off = 0, 0
        else:
            kstart = pl.multiple_of(jnp.maximum(n - 1, 0) * BAND, BAND)
            off = n * BAND - kstart
        qstart = pl.multiple_of(nl * BAND, BAND)
        steps = (off + lax.broadcasted_iota(jnp.int32, (BAND, kw), 0)
                 - lax.broadcasted_iota(jnp.int32, (BAND, kw), 1))
        valid = jnp.abs(steps - BAND // 2) <= BAND // 2
        dist = (steps * dil).astype(F32)
        lane = lax.broadcasted_iota(jnp.int32, (BAND, LSE_LANES), 1)
        lse_tile = jnp.zeros((BAND, LSE_LANES), F32)
        scores = []
        for hh in range(HEADS):
            qh = q_ref[rl, pl.ds(qstart, BAND), _head_cols(hh)]
            kh = k_ref[rl, pl.ds(kstart, kw), _head_cols(hh)]
            scores.append(lax.dot_general(qh, kh, (((1,), (1,)), ((), ())), preferred_element_type=F32))
        probs, dens = [], []
        for hh in range(HEADS):
            logits = jnp.where(valid, scores[hh] - (2.0 ** -(hh + 1)) * dist, NEG_INF)
            mx = jnp.max(logits, axis=-1, keepdims=True)
            p = jnp.exp(logits - mx)
            den = jnp.sum(p, axis=-1, keepdims=True)
            probs.append(p.astype(BF16))
            dens.append(den)
            lse_tile = jnp.where(lane == hh, mx + jnp.log(den), lse_tile)
        for hh in range(HEADS):
            vh = v_ref[rl, pl.ds(kstart, kw), _head_cols(hh)]
            o = jnp.dot(probs[hh], vh, preferred_element_type=F32) / dens[hh]
            o_ref[rl, pl.ds(qstart, BAND), _head_cols(hh)] = o
        o_ref[rl, pl.ds(qstart, BAND), GROUP_WIDTH:] = lse_tile
        return carry

    lax.fori_loop(0, rb * nblk, body, 0)


def _band_attn(q, k, v, g, *, blocks_per_step=4):
    batch, dil, mr, _ = q.shape
    assert dil == DILS[g] and mr % BAND == 0
    tq = min(mr, blocks_per_step * BAND)
    nblk = tq // BAND
    rb = min(dil, blocks_per_step // nblk)
    kw = min(mr, 2 * BAND)
    return pl.pallas_call(
        functools.partial(_band_attn_kernel, nblk=nblk, kw=kw, dil=dil),
        grid=(batch, dil // rb, mr // tq),
        in_specs=[
            pl.BlockSpec((None, rb, tq, GROUP_WIDTH), lambda b, r, i: (b, r, i, 0)),
            pl.BlockSpec((None, rb, mr, GROUP_WIDTH), lambda b, r, i: (b, r, 0, 0)),
            pl.BlockSpec((None, rb, mr, GROUP_WIDTH), lambda b, r, i: (b, r, 0, 0)),
        ],
        out_specs=pl.BlockSpec((None, rb, tq, ATTN_OUT_COLS), lambda b, r, i: (b, r, i, 0)),
        out_shape=jax.ShapeDtypeStruct((batch, dil, mr, ATTN_OUT_COLS), F32),
        compiler_params=_params("parallel", "parallel", "arbitrary"),
        name="band_attn_g%d" % g,
    )(q, k, v)


def _gather_attn_kernel(q_ref, kc_ref, vc_ref, kn_ref, vn_ref, o_ref, oh_ref, lh_ref, *, dil, t_new):
    length = kc_ref.shape[0]
    head = lax.broadcasted_iota(jnp.int32, (HEADS, 1), 0)
    slope = lax.bitcast_convert_type((126 - head) << 23, F32)
    for i in range(t_new):
        qi = q_ref[i]
        n_new = i // dil + 1
        n_old = BAND + 1 - n_new
        first = length + i - BAND * dil
        assert first >= 0 and first + (n_old - 1) * dil < length
        kc = kc_ref[pl.ds(first, n_old, stride=dil)]
        vc = vc_ref[pl.ds(first, n_old, stride=dil)]
        steps = BAND - lax.broadcasted_iota(jnp.int32, (n_old, 1, 1), 0)
        sc = jnp.sum(kc * qi[None], axis=-1, keepdims=True) - slope[None] * (steps * dil).astype(F32)
        sn = [jnp.sum(kn_ref[i - jj * dil] * qi, axis=-1, keepdims=True) - slope * float(jj * dil)
              for jj in range(n_new)]
        mx = jnp.max(sc, axis=0)
        for s in sn:
            mx = jnp.maximum(mx, s)
        pc = jnp.exp(sc - mx[None])
        den = jnp.sum(pc, axis=0)
        acc = jnp.sum(pc * vc, axis=0)
        for jj, s in enumerate(sn):
            pn = jnp.exp(s - mx)
            den = den + pn
            acc = acc + pn * vn_ref[i - jj * dil]
        oh_ref[i] = acc / den
        lh_ref[i] = jnp.broadcast_to(mx + jnp.log(den), (HEADS, LSE_LANES))
    lane = lax.broadcasted_iota(jnp.int32, (t_new, LSE_LANES), 1)
    lse_tile = jnp.zeros((t_new, LSE_LANES), F32)
    for hh in range(HEADS):
        o_ref[:, _head_cols(hh)] = oh_ref[:, hh, :]
        lse_tile = jnp.where(lane == hh, lh_ref[:, hh, :], lse_tile)
    o_ref[:, GROUP_WIDTH:] = lse_tile


def _gather_attn(q, kc, vc, kn, vn, g):
    win, dil = ATTN_GROUPS[g]
    b, length = kc.shape[:2]
    t_new = q.shape[2]
    assert length == win and t_new <= dil * BAND
    new = pl.BlockSpec((None, None, t_new, HEADS, HEAD_DIM), lambda bi: (g, bi, 0, 0, 0))
    cache = pl.BlockSpec((None, length, HEADS, HEAD_DIM), lambda bi: (bi, 0, 0, 0))
    return pl.pallas_call(
        functools.partial(_gather_attn_kernel, dil=dil, t_new=t_new),
        grid=(b,),
        in_specs=[new, cache, cache, new, new],
        out_specs=pl.BlockSpec((None, t_new, ATTN_OUT_COLS), lambda bi: (bi, 0, 0)),
        out_shape=jax.ShapeDtypeStruct((b, t_new, ATTN_OUT_COLS), F32),
        scratch_shapes=[pltpu.VMEM((t_new, HEADS, HEAD_DIM), F32),
                        pltpu.VMEM((t_new, HEADS, LSE_LANES), F32)],
        compiler_params=_params("parallel"),
        name="gather_attn_g%d" % g,
    )(q, kc, vc, kn, vn)


def _roll_kernel(kc_ref, kx_ref, kn_ref, vc_ref, vx_ref, vn_ref, ko_ref, vo_ref, *, t_new):
    last = pl.program_id(1) == pl.num_programs(1) - 1
    lc = kc_ref.shape[0]
    for c_ref, x_ref, n_ref, o_ref in ((kc_ref, kx_ref, kn_ref, ko_ref), (vc_ref, vx_ref, vn_ref, vo_ref)):
        o_ref[0:lc - t_new] = c_ref[t_new:]
        o_ref[lc - t_new:] = jnp.where(last, n_ref[...], x_ref[...])


def _roll_cache(kc, vc, kn, vn, g, *, lc=512):
    b, length = kc.shape[:2]
    t_new = kn.shape[2]
    lc = min(lc, length)
    per = lc // t_new
    main = pl.BlockSpec((None, lc, HEADS, HEAD_DIM), lambda bi, j: (bi, j, 0, 0))
    nxt = pl.BlockSpec((None, t_new, HEADS, HEAD_DIM),
                       lambda bi, j: (bi, jnp.minimum((j + 1) * per, length // t_new - 1), 0, 0))
    new = pl.BlockSpec((None, None, t_new, HEADS, HEAD_DIM), lambda bi, j: (g, bi, 0, 0, 0))
    return pl.pallas_call(
        functools.partial(_roll_kernel, t_new=t_new),
        grid=(b, length // lc),
        in_specs=[main, nxt, new, main, nxt, new],
        out_specs=[main, main],
        out_shape=[jax.ShapeDtypeStruct(kc.shape, F32), jax.ShapeDtypeStruct(vc.shape, F32)],
        compiler_params=_params("parallel", "arbitrary"),
        name="roll_cache_g%d" % g,
    )(kc, kc, kn, vc, vc, vn)


def _merge_kernel(h_ref, y_ref, o0_ref, o1_ref, o2_ref, gate_ref, wp_ref, wa_ref, wo_ref, out_ref,
                  oa_ref, tok_ref, *, dils):
    o_refs = (o0_ref, o1_ref, o2_ref)
    tm = out_ref.shape[0]
    for gi, dil in enumerate(dils):
        if dil == 1:
            continue
        n = tm // dil
        for r in range(dil):
            for c in range(ATTN_OUT_TILES):
                tok_ref[gi, c, pl.ds(r, n, stride=dil), :] = o_refs[gi][r, :, c * 128:(c + 1) * 128]

    def tile(gi, c):
        if dils[gi] == 1:
            return o_refs[gi][0, :, c * 128:(c + 1) * 128]
        return tok_ref[gi, c]

    lses = [tile(gi, HEADS) for gi in range(N_GROUPS)]
    mx = jnp.maximum(jnp.maximum(lses[0], lses[1]), lses[2])
    es = [jnp.exp(l - mx) for l in lses]
    inv = 1.0 / (es[0] + es[1] + es[2])
    wts = [e * inv for e in es]
    for hh in range(HEADS):
        acc = wts[0][:, hh:hh + 1] * tile(0, hh)
        for gi in range(1, N_GROUPS):
            acc = acc + wts[gi][:, hh:hh + 1] * tile(gi, hh)
        oa_ref[:, _head_cols(hh)] = acc.astype(BF16)
    a_attn = jnp.dot(oa_ref[...], wa_ref[...], preferred_element_type=F32)
    a_pool = jnp.dot(y_ref[...], wp_ref[...], preferred_element_type=F32)
    d = out_ref.shape[1]
    mix = gate_ref[:, :d].astype(F32) * a_pool + gate_ref[:, d:].astype(F32) * a_attn
    out_ref[...] = h_ref[...] + jnp.dot(mix.astype(BF16), wo_ref[...], preferred_element_type=F32)


def _merge(h, y_pool, o_groups, gates, w_pool_out, w_attn_out, w_out, *, tm=256):
    m, d = h.shape
    dils = tuple(o.shape[1] for o in o_groups)
    seq = o_groups[0].shape[1] * o_groups[0].shape[2]
    tm = _row_tile(seq, tm)
    per_seq = seq // tm
    row = lambda i: (i, 0)
    fixed = lambda i: (0, 0)
    o_specs = [pl.BlockSpec((None, dil, tm // dil, ATTN_OUT_COLS), lambda i: (i // per_seq, 0, i % per_seq, 0))
               for dil in dils]
    return pl.pallas_call(
        functools.partial(_merge_kernel, dils=dils),
        grid=(m // tm,),
        in_specs=[
            pl.BlockSpec((tm, d), row),
            pl.BlockSpec((tm, POOL_WIDTH), row),
            *o_specs,
            pl.BlockSpec((tm, 2 * d), row),
            _resident(w_pool_out.shape, fixed),
            _resident(w_attn_out.shape, fixed),
            _resident(w_out.shape, fixed),
        ],
        out_specs=pl.BlockSpec((tm, d), row),
        out_shape=jax.ShapeDtypeStruct((m, d), F32),
        scratch_shapes=[pltpu.VMEM((tm, GROUP_WIDTH), BF16),
                        pltpu.VMEM((N_GROUPS, ATTN_OUT_TILES, tm, 128), F32)],
        compiler_params=_params("parallel"),
        name="merge",
    )(h, y_pool, *o_groups, gates, w_pool_out, w_attn_out, w_out)


def _ple_kernel(h_ref, p_ref, g_ref, wg_ref, wp_ref, out_ref):
    h = h_ref[...]
    hn = _rms(h, g_ref[...]).astype(BF16)
    gate = jax.nn.sigmoid(jnp.dot(hn, wg_ref[...], preferred_element_type=F32))
    emb = jnp.dot(p_ref[...].astype(BF16), wp_ref[...], preferred_element_type=F32)
    out_ref[...] = h + gate * emb


def _ple(h, p, g, w_gate, w_proj, *, tm=512):
    m, d = h.shape
    tm = _row_tile(m, tm)
    row = lambda i: (i, 0)
    fixed = lambda i: (0, 0)
    return pl.pallas_call(
        _ple_kernel,
        grid=(m // tm,),
        in_specs=[
            pl.BlockSpec((tm, d), row),
            pl.BlockSpec((tm, p.shape[1]), row),
            _resident((1, d), fixed),
            _resident(w_gate.shape, fixed),
            _resident(w_proj.shape, fixed),
        ],
        out_specs=pl.BlockSpec((tm, d), row),
        out_shape=jax.ShapeDtypeStruct((m, d), F32),
        compiler_params=_params("parallel"),
        name="ple",
    )(h, p, g, w_gate, w_proj)


def _layer(x, p, w, *, prompt, pool_prev=None, caches=None):
    b, t, d = x.shape
    m = b * t
    h1, u = _ffn(x.reshape(m, d), w["ffn1_norm"], w["ffn1_w_in"], w["ffn1_w_out"], w["mix_norm"])
    outs_a = _proj_a(u, w["w_zq"], w["w_gate"], w["q_norm"], batch=b, seq=t, prompt=prompt)
    outs_b = _proj_b(u, w["w_kv"], w["k_norm"], batch=b, seq=t, prompt=prompt)
    z, gates = outs_a[0], outs_a[-1]
    z3 = z.reshape(b, t, POOL_WIDTH)
    if prompt:
        qs = outs_a[1:1 + N_GROUPS]
        kbs, vbs = outs_b[0:N_GROUPS], outs_b[N_GROUPS:2 * N_GROUPS]
        kf01, kf2, vf01, vf2 = outs_b[2 * N_GROUPS:]
        y_pool = _pool(z3, z3, w["pool_map"], w["pool_scale"], tp=512, pos0=0, zero_first_halo=True)
        new_pool = z3[:, t - POOL_STATE:]
        o_groups = [_band_attn(qs[g], kbs[g], vbs[g], g) for g in range(N_GROUPS)]
        new_kv = []
        for g, (window, _) in enumerate(ATTN_GROUPS):
            keep = min(window, t)
            for pair, whole in ((kf01, kf2), (vf01, vf2)):
                rows = (whole if g == N_GROUPS - 1 else pair[g]).reshape(b, t, HEADS, HEAD_DIM)
                new_kv.append(rows[:, t - keep:])
    else:
        qf = outs_a[1].reshape(N_GROUPS, b, t, HEADS, HEAD_DIM)
        kn = outs_b[0].reshape(N_GROUPS, b, t, HEADS, HEAD_DIM)
        vn = outs_b[1].reshape(N_GROUPS, b, t, HEADS, HEAD_DIM)
        halo = jnp.pad(pool_prev, ((0, 0), (POOL_HALO - POOL_STATE, 0), (0, 0)))
        y_pool = _pool(z3, halo, w["pool_map"], w["pool_scale"], tp=t, pos0=PAST_LEN, zero_first_halo=False)
        new_pool = jnp.concatenate([pool_prev, z3], axis=1)[:, t:]
        o_groups, new_kv = [], []
        for g in range(N_GROUPS):
            kc, vc = caches[2 * g], caches[2 * g + 1]
            o_groups.append(_gather_attn(qf, kc, vc, kn, vn, g).reshape(1, 1, m, ATTN_OUT_COLS))
            new_kv.extend(_roll_cache(kc, vc, kn, vn, g))
    h2 = _merge(h1, y_pool.reshape(m, POOL_WIDTH), o_groups, gates,
                w["w_pool_out"], w["w_attn_out"], w["w_out"])
    h3 = _ffn(h2, w["ffn2_norm"], w["ffn2_w_in"], w["ffn2_w_out"])
    h4 = _ple(h3, p.reshape(m, -1), w["ple_norm"], w["ple_gate"], w["ple_proj"])
    return h4.reshape(b, t, d), new_pool, new_kv


def kernel(x_prompt, x_sample, p_prompt, p_sample, state_pool, cache_k0, cache_v0, cache_k1, cache_v1, cache_k2, cache_v2, ffn1_norm, ffn1_w_in, ffn1_w_out, mix_norm, w_in, q_norm, k_norm, pool_map, pool_scale, w_pool_out, w_attn_out, w_out, ffn2_norm, ffn2_w_in, ffn2_w_out, ple_norm, ple_gate, ple_proj):
    caches = (cache_k0, cache_v0, cache_k1, cache_v1, cache_k2, cache_v2)
    depth = ffn1_norm.shape[0]
    kv0 = POOL_WIDTH + N_GROUPS * GROUP_WIDTH
    gate0 = kv0 + 2 * N_GROUPS * GROUP_WIDTH
    hp, hs = x_prompt, x_sample
    pool_p, pool_s, kv_p, kv_s = [], [], [], []
    for i in range(depth):
        row = lambda a: a[i][None, :]
        w = dict(
            ffn1_norm=row(ffn1_norm), ffn1_w_in=ffn1_w_in[i].astype(BF16), ffn1_w_out=ffn1_w_out[i].astype(BF16),
            mix_norm=row(mix_norm), q_norm=row(q_norm), k_norm=row(k_norm),
            w_zq=w_in[i, :, :kv0].astype(BF16), w_kv=w_in[i, :, kv0:gate0].astype(BF16),
            w_gate=w_in[i, :, gate0:].astype(BF16),
            pool_map=pool_map[i].astype(BF16), pool_scale=row(pool_scale),
            w_pool_out=w_pool_out[i].astype(BF16), w_attn_out=w_attn_out[i].astype(BF16),
            w_out=w_out[i].astype(BF16),
            ffn2_norm=row(ffn2_norm), ffn2_w_in=ffn2_w_in[i].astype(BF16), ffn2_w_out=ffn2_w_out[i].astype(BF16),
            ple_norm=row(ple_norm), ple_gate=ple_gate[i].astype(BF16), ple_proj=ple_proj[i].astype(BF16),
        )
        hp, pp, kvp = _layer(hp, p_prompt[i], w, prompt=True)
        hs, ps, kvs = _layer(hs, p_sample[i], w, prompt=False, pool_prev=state_pool[i],
                             caches=tuple(c[i] for c in caches))
        pool_p.append(pp)
        pool_s.append(ps)
        kv_p.append(kvp)
        kv_s.append(kvs)
    stack = lambda rows: jnp.stack(rows)
    kp = [stack([r[j] for r in kv_p]) for j in range(2 * N_GROUPS)]
    ks = [stack([r[j] for r in kv_s]) for j in range(2 * N_GROUPS)]
    return (hp, hs, stack(pool_p), *kp, stack(pool_s), *ks)
```

```python
import functools

import jax
import jax.numpy as jnp
from jax import lax
from jax.experimental import pallas as pl
from jax.experimental.pallas import tpu as pltpu

F32 = jnp.float32
BF16 = jnp.bfloat16

POOL_WINDOWS = (2, 4, 8, 16)
POOL_GROUP_WIDTH = 256
POOL_WIDTH = POOL_GROUP_WIDTH * len(POOL_WINDOWS)
POOL_STATE = max(POOL_WINDOWS) - 1
POOL_HALO = 16
HEAD_DIM = 128
HEADS = 8
GROUP_WIDTH = HEADS * HEAD_DIM
ATTN_GROUPS = ((128, 1), (512, 4), (2048, 16))
DILS = tuple(d for _, d in ATTN_GROUPS)
N_GROUPS = len(ATTN_GROUPS)
SLAB_OF = tuple(sum(1 for d in DILS[:g] if d > 1) for g in range(N_GROUPS))
N_SLABS = sum(1 for d in DILS if d > 1)
BAND = 128
LSE_LANES = 128
ATTN_OUT_COLS = GROUP_WIDTH + LSE_LANES
ATTN_OUT_TILES = ATTN_OUT_COLS // 128
RMS_EPS = 1e-6
NEG_INF = -1e30
PAST_LEN = 16384

VMEM_LIMIT = 56 * 1024 * 1024


def _params(*semantics):
    return pltpu.CompilerParams(dimension_semantics=semantics, vmem_limit_bytes=VMEM_LIMIT)


def _resident(shape, index_map):
    return pl.BlockSpec(shape, index_map, pipeline_mode=pl.Buffered(1))


def _rms(x, g):
    return x * lax.rsqrt(jnp.mean(x * x, axis=-1, keepdims=True) + RMS_EPS) * g


def _row_tile(m, want):
    t = min(m, want)
    assert m % t == 0, (m, t)
    return t


def _head_cols(hh):
    return slice(hh * HEAD_DIM, (hh + 1) * HEAD_DIM)


def _ffn_kernel(*refs, emit_norm):
    if emit_norm:
        x_ref, g_ref, wa_ref, wb_ref, wo_ref, gn_ref, o_ref, u_ref, xn_ref = refs
    else:
        x_ref, g_ref, wa_ref, wb_ref, wo_ref, o_ref, xn_ref = refs
    k = pl.program_id(1)

    @pl.when(k == 0)
    def _():
        x = x_ref[...]
        xn_ref[...] = _rms(x, g_ref[...]).astype(BF16)
        o_ref[...] = x

    xn = xn_ref[...]
    a = jnp.dot(xn, wa_ref[...], preferred_element_type=F32)
    b = jnp.dot(xn, wb_ref[...], preferred_element_type=F32)
    hid = (0.5 * (a * jax.nn.sigmoid(a)) * b).astype(BF16)
    o_ref[...] += jnp.dot(hid, wo_ref[...], preferred_element_type=F32)

    if emit_norm:
        @pl.when(k == pl.num_programs(1) - 1)
        def _():
            u_ref[...] = _rms(o_ref[...], gn_ref[...]).astype(BF16)


def _ffn(x, g, w_in, w_out, g_next=None, *, tm=512, tf=512):
    m, d = x.shape
    d_ff = w_out.shape[0]
    tm = _row_tile(m, tm)
    tf = _row_tile(d_ff, tf)
    nk = d_ff // tf
    emit_norm = g_next is not None
    row = pl.BlockSpec((tm, d), lambda i, k: (i, 0))
    vec = _resident((1, d), lambda i, k: (0, 0))
    in_specs = [row, vec,
                pl.BlockSpec((d, tf), lambda i, k: (0, k)),
                pl.BlockSpec((d, tf), lambda i, k: (0, k + nk)),
                pl.BlockSpec((tf, d), lambda i, k: (k, 0))]
    args = [x, g, w_in, w_in, w_out]
    out_specs, out_shape = row, jax.ShapeDtypeStruct((m, d), F32)
    if emit_norm:
        in_specs.append(vec)
        args.append(g_next)
        out_specs = [row, row]
        out_shape = [out_shape, jax.ShapeDtypeStruct((m, d), BF16)]
    return pl.pallas_call(
        functools.partial(_ffn_kernel, emit_norm=emit_norm),
        grid=(m // tm, nk),
        in_specs=in_specs,
        out_specs=out_specs,
        out_shape=out_shape,
        scratch_shapes=[pltpu.VMEM((tm, d), BF16)],
        compiler_params=_params("parallel", "arbitrary"),
        name="ffn",
    )(*args)


def _head_rms(x, g):
    return jnp.concatenate([_rms(x[:, _head_cols(hh)], g) for hh in range(HEADS)], axis=-1)


def _store_head_rows(ref, val):
    tm = val.shape[0]
    for hh in range(HEADS):
        ref[pl.ds(hh, tm, stride=HEADS), :] = val[:, _head_cols(hh)]


def _store_residue_major(ref, slab_ref, val, dil):
    if dil == 1:
        ref[0] = val.astype(BF16)
        return
    n = val.shape[0] // dil
    for hh in range(HEADS):
        slab_ref[hh] = val[:, _head_cols(hh)]
    for r in range(dil):
        for hh in range(HEADS):
            ref[r, :, _head_cols(hh)] = slab_ref[hh, pl.ds(r, n, stride=dil), :].astype(BF16)


def _tile(u_ref, w_ref, t):
    return jnp.dot(u_ref[...], w_ref[:, t * GROUP_WIDTH:(t + 1) * GROUP_WIDTH], preferred_element_type=F32)


_HEAVY_FIRST = tuple(sorted(range(N_GROUPS), key=lambda g: -DILS[g]))


def _proj_a_kernel(*refs, prompt):
    if prompt:
        u_ref, wzq_ref, wg_ref, qn_ref, z_ref, q0_ref, q1_ref, q2_ref, gate_ref, slab_ref = refs
        q_refs = (q0_ref, q1_ref, q2_ref)
    else:
        u_ref, wzq_ref, wg_ref, qn_ref, z_ref, q_ref, gate_ref = refs
    for g in _HEAVY_FIRST:
        q = _head_rms(_tile(u_ref, wzq_ref, 1 + g), qn_ref[...]) * (HEAD_DIM ** -0.5)
        if prompt:
            _store_residue_major(q_refs[g], slab_ref.at[SLAB_OF[g]], q, DILS[g])
        else:
            _store_head_rows(q_ref.at[g], q)
    for t in range(wg_ref.shape[1] // GROUP_WIDTH):
        gate = jax.nn.sigmoid(_tile(u_ref, wg_ref, t))
        gate_ref[:, t * GROUP_WIDTH:(t + 1) * GROUP_WIDTH] = gate.astype(BF16)
    z_ref[...] = _tile(u_ref, wzq_ref, 0)


def _proj_b_kernel(*refs, prompt):
    if prompt:
        (u_ref, w_ref, kn_ref, kb0_ref, kb1_ref, kb2_ref, vb0_ref, vb1_ref, vb2_ref,
         kf0_ref, kf1_ref, kf2_ref, vf0_ref, vf1_ref, vf2_ref, slab_ref) = refs
        kb_refs, vb_refs = (kb0_ref, kb1_ref, kb2_ref), (vb0_ref, vb1_ref, vb2_ref)
        kf_refs, vf_refs = (kf0_ref, kf1_ref, kf2_ref), (vf0_ref, vf1_ref, vf2_ref)
    else:
        u_ref, w_ref, kn_ref, kf_ref, vf_ref = refs
        kf_refs = [kf_ref.at[g] for g in range(N_GROUPS)]
        vf_refs = [vf_ref.at[g] for g in range(N_GROUPS)]
    tm = u_ref.shape[0]
    for g in _HEAVY_FIRST:
        k = _head_rms(_tile(u_ref, w_ref, g), kn_ref[...])
        v = _tile(u_ref, w_ref, N_GROUPS + g)
        kept = kf_refs[g].shape[0] // HEADS
        _store_head_rows(kf_refs[g], k[tm - kept:])
        _store_head_rows(vf_refs[g], v[tm - kept:])
        if prompt:
            _store_residue_major(kb_refs[g], slab_ref.at[0, SLAB_OF[g]], k, DILS[g])
            _store_residue_major(vb_refs[g], slab_ref.at[1, SLAB_OF[g]], v, DILS[g])


def _residue_specs(batch, seq, tm, dtype):
    per_seq = seq // tm
    specs, shapes = [], []
    for dil in DILS:
        specs.append(pl.BlockSpec((None, dil, tm // dil, GROUP_WIDTH),
                                  lambda i: (i // per_seq, 0, i % per_seq, 0)))
        shapes.append(jax.ShapeDtypeStruct((batch, dil, seq // dil, GROUP_WIDTH), dtype))
    return specs, shapes


def _head_rows_spec(n_arrays, m, tm):
    return (pl.BlockSpec((n_arrays, tm * HEADS, HEAD_DIM), lambda i: (0, i, 0)),
            jax.ShapeDtypeStruct((n_arrays, m * HEADS, HEAD_DIM), F32))


def _suffix_head_rows_spec(batch, seq, tm, keep):
    rows = min(keep, tm)
    assert keep % rows == 0
    blocks, per_seq = keep // rows, seq // tm
    spec = pl.BlockSpec((None, None, rows * HEADS, HEAD_DIM),
                        lambda i: (i // per_seq, jnp.maximum(i % per_seq - (per_seq - blocks), 0), 0, 0))
    return spec, jax.ShapeDtypeStruct((batch, blocks, rows * HEADS, HEAD_DIM), F32)


def _proj_a(u, w_zq, w_gate, qn, *, batch, seq, prompt, tm=256):
    m, d = u.shape
    tm = _row_tile(seq if prompt else m, tm)
    fixed = lambda i: (0, 0)
    rows = lambda width: pl.BlockSpec((tm, width), lambda i: (i, 0))
    scratch = []
    if prompt:
        q_specs, q_shapes = _residue_specs(batch, seq, tm, BF16)
        scratch = [pltpu.VMEM((N_SLABS, HEADS, tm, HEAD_DIM), F32)]
    else:
        spec, shape = _head_rows_spec(N_GROUPS, m, tm)
        q_specs, q_shapes = [spec], [shape]
    return pl.pallas_call(
        functools.partial(_proj_a_kernel, prompt=prompt),
        grid=(m // tm,),
        in_specs=[rows(d), _resident(w_zq.shape, fixed), _resident(w_gate.shape, fixed),
                  _resident((1, HEAD_DIM), fixed)],
        out_specs=[rows(GROUP_WIDTH), *q_specs, rows(w_gate.shape[1])],
        out_shape=[jax.ShapeDtypeStruct((m, GROUP_WIDTH), F32), *q_shapes,
                   jax.ShapeDtypeStruct((m, w_gate.shape[1]), BF16)],
        scratch_shapes=scratch,
        compiler_params=_params("parallel"),
        name="proj_a",
    )(u, w_zq, w_gate, qn)


def _proj_b(u, w_kv, kn, *, batch, seq, prompt, tm=256):
    m, d = u.shape
    tm = _row_tile(seq if prompt else m, tm)
    fixed = lambda i: (0, 0)
    scratch = []
    if prompt:
        kb_specs, kb_shapes = _residue_specs(batch, seq, tm, BF16)
        kf = [_suffix_head_rows_spec(batch, seq, tm, min(window, seq)) for window, _ in ATTN_GROUPS]
        kf_specs, kf_shapes = [s for s, _ in kf], [s for _, s in kf]
        out_specs = [*kb_specs, *kb_specs, *kf_specs, *kf_specs]
        out_shape = [*kb_shapes, *kb_shapes, *kf_shapes, *kf_shapes]
        scratch = [pltpu.VMEM((2, N_SLABS, HEADS, tm, HEAD_DIM), F32)]
    else:
        spec, shape = _head_rows_spec(N_GROUPS, m, tm)
        out_specs, out_shape = [spec, spec], [shape, shape]
    return pl.pallas_call(
        functools.partial(_proj_b_kernel, prompt=prompt),
        grid=(m // tm,),
        in_specs=[pl.BlockSpec((tm, d), lambda i: (i, 0)), _resident(w_kv.shape, fixed),
                  _resident((1, HEAD_DIM), fixed)],
        out_specs=out_specs,
        out_shape=out_shape,
        scratch_shapes=scratch,
        compiler_params=_params("arbitrary"),
        name="proj_b",
    )(u, w_kv, kn)


def _pool_kernel(z_ref, halo_ref, pm_ref, ps_ref, y_ref, ext_ref, *, tp, pos0, zero_first_halo):
    i = pl.program_id(1)
    halo = halo_ref[...]
    if zero_first_halo:
        halo = jnp.where(i == 0, 0.0, halo)
    ext_ref[0:POOL_HALO, :] = halo
    ext_ref[POOL_HALO:, :] = z_ref[...]
    pos = pos0 + i * tp + lax.broadcasted_iota(jnp.int32, (tp, 1), 0)
    for gi, w in enumerate(POOL_WINDOWS):
        cols = slice(gi * POOL_GROUP_WIDTH, (gi + 1) * POOL_GROUP_WIDTH)
        zc = ext_ref[POOL_HALO:POOL_HALO + tp, cols]
        wsum = zc
        for back in range(1, w):
            wsum = wsum + ext_ref[POOL_HALO - back:POOL_HALO - back + tp, cols]
        cnt = jnp.minimum(w, pos + 1).astype(F32)
        pooled = (wsum / cnt - zc).astype(BF16)
        y = jnp.dot(pooled, pm_ref[gi], preferred_element_type=F32) * ps_ref[:, cols]
        y_ref[:, cols] = y.astype(BF16)


def _pool(z, halo, pool_map, pool_scale, *, tp, pos0, zero_first_halo):
    b, t, p = z.shape
    tp = _row_tile(t, tp)
    hb = tp // POOL_HALO
    if halo is z:
        halo_map = lambda bi, i: (bi, jnp.maximum(i * hb - 1, 0), 0)
    else:
        assert t == tp and halo.shape == (b, POOL_HALO, p)
        halo_map = lambda bi, i: (bi, 0, 0)
    return pl.pallas_call(
        functools.partial(_pool_kernel, tp=tp, pos0=pos0, zero_first_halo=zero_first_halo),
        grid=(b, t // tp),
        in_specs=[
            pl.BlockSpec((None, tp, p), lambda bi, i: (bi, i, 0)),
            pl.BlockSpec((None, POOL_HALO, p), halo_map),
            _resident(pool_map.shape, lambda bi, i: (0, 0, 0)),
            _resident((1, p), lambda bi, i: (0, 0)),
        ],
        out_specs=pl.BlockSpec((None, tp, p), lambda bi, i: (bi, i, 0)),
        out_shape=jax.ShapeDtypeStruct((b, t, p), BF16),
        scratch_shapes=[pltpu.VMEM((POOL_HALO + tp, p), F32)],
        compiler_params=_params("parallel", "arbitrary"),
        name="pool",
    )(z, halo, pool_map, pool_scale)


def _band_attn_kernel(q_ref, k_ref, v_ref, o_ref, *, nblk, kw, dil):
    rb = q_ref.shape[0]
    i = pl.program_id(2)

    def body(it, carry):
        rl, nl = it // nblk, it % nblk
        n = i * nblk + nl
        if kw == BAND:
            kstart, off = 0, 0
        else:
            kstart = pl.multiple_of(jnp.maximum(n - 1, 0) * BAND, BAND)
            off = n * BAND - kstart
        qstart = pl.multiple_of(nl * BAND, BAND)
        steps = (off + lax.broadcasted_iota(jnp.int32, (BAND, kw), 0)
                 - lax.broadcasted_iota(jnp.int32, (BAND, kw), 1))
        valid = jnp.abs(steps - BAND // 2) <= BAND // 2
        dist = (steps * dil).astype(F32)
        lane = lax.broadcasted_iota(jnp.int32, (BAND, LSE_LANES), 1)
        lse_tile = jnp.zeros((BAND, LSE_LANES), F32)
        scores = []
        for hh in range(HEADS):
            qh = q_ref[rl, pl.ds(qstart, BAND), _head_cols(hh)]
            kh = k_ref[rl, pl.ds(kstart, kw), _head_cols(hh)]
            scores.append(lax.dot_general(qh, kh, (((1,), (1,)), ((), ())), preferred_element_type=F32))
        probs, dens = [], []
        for hh in range(HEADS):
            logits = jnp.where(valid, scores[hh] - (2.0 ** -(hh + 1)) * dist, NEG_INF)
            mx = jnp.max(logits, axis=-1, keepdims=True)
            p = jnp.exp(logits - mx)
            den = jnp.sum(p, axis=-1, keepdims=True)
            probs.append(p.astype(BF16))
            dens.append(den)
            lse_tile = jnp.where(lane == hh, mx + jnp.log(den), lse_tile)
        for hh in range(HEADS):
            vh = v_ref[rl, pl.ds(kstart, kw), _head_cols(hh)]
            o = jnp.dot(probs[hh], vh, preferred_element_type=F32) / dens[hh]
            o_ref[rl, pl.ds(qstart, BAND), _head_cols(hh)] = o
        o_ref[rl, pl.ds(qstart, BAND), GROUP_WIDTH:] = lse_tile
        return carry

    lax.fori_loop(0, rb * nblk, body, 0)


def _band_attn(q, k, v, g, *, blocks_per_step=4):
    batch, dil, mr, _ = q.shape
    assert dil == DILS[g] and mr % BAND == 0
    tq = min(mr, blocks_per_step * BAND)
    nblk = tq // BAND
    rb = min(dil, blocks_per_step // nblk)
    kw = min(mr, 2 * BAND)
    return pl.pallas_call(
        functools.partial(_band_attn_kernel, nblk=nblk, kw=kw, dil=dil),
        grid=(batch, dil // rb, mr // tq),
        in_specs=[
            pl.BlockSpec((None, rb, tq, GROUP_WIDTH), lambda b, r, i: (b, r, i, 0)),
            pl.BlockSpec((None, rb, mr, GROUP_WIDTH), lambda b, r, i: (b, r, 0, 0)),
            pl.BlockSpec((None, rb, mr, GROUP_WIDTH), lambda b, r, i: (b, r, 0, 0)),
        ],
        out_specs=pl.BlockSpec((None, rb, tq, ATTN_OUT_COLS), lambda b, r, i: (b, r, i, 0)),
        out_shape=jax.ShapeDtypeStruct((batch, dil, mr, ATTN_OUT_COLS), F32),
        compiler_params=_params("parallel", "parallel", "arbitrary"),
        name="band_attn_g%d" % g,
    )(q, k, v)


def _roll_copies(b, cache_ref, new_ref, out_hbm, sems, which):
    length, t_new = cache_ref.shape[1], new_ref.shape[2]
    kept = length - t_new
    return (pltpu.make_async_copy(cache_ref.at[0, pl.ds(t_new, kept)],
                                  out_hbm.at[b, pl.ds(0, kept)], sems.at[which, 0]),
            pltpu.make_async_copy(new_ref.at[0, 0], out_hbm.at[b, pl.ds(kept, t_new)], sems.at[which, 1]))


def _gather_attn_kernel(q_ref, kc_ref, vc_ref, kn_ref, vn_ref, o_ref, ko_hbm, vo_hbm, oh_ref, lh_ref, sems,
                        *, dil, t_new):
    b = pl.program_id(0)
    copies = (*_roll_copies(b, kc_ref, kn_ref, ko_hbm, sems, 0),
              *_roll_copies(b, vc_ref, vn_ref, vo_hbm, sems, 1))
    for copy in copies:
        copy.start()
    length = kc_ref.shape[1]
    head = lax.broadcasted_iota(jnp.int32, (HEADS, 1), 0)
    slope = lax.bitcast_convert_type((126 - head) << 23, F32)
    for i in range(t_new):
        qi = q_ref[0, 0, i]
        n_new = i // dil + 1
        n_old = BAND + 1 - n_new
        first = length + i - BAND * dil
        assert first >= 0 and first + (n_old - 1) * dil < length
        kc = kc_ref[0, pl.ds(first, n_old, stride=dil)]
        vc = vc_ref[0, pl.ds(first, n_old, stride=dil)]
        steps = BAND - lax.broadcasted_iota(jnp.int32, (n_old, 1, 1), 0)
        sc = jnp.sum(kc * qi[None], axis=-1, keepdims=True) - slope[None] * (steps * dil).astype(F32)
        sn = [jnp.sum(kn_ref[0, 0, i - jj * dil] * qi, axis=-1, keepdims=True) - slope * float(jj * dil)
              for jj in range(n_new)]
        mx = jnp.max(sc, axis=0)
        for s in sn:
            mx = jnp.maximum(mx, s)
        pc = jnp.exp(sc - mx[None])
        den = jnp.sum(pc, axis=0)
        acc = jnp.sum(pc * vc, axis=0)
        for jj, s in enumerate(sn):
            pn = jnp.exp(s - mx)
            den = den + pn
            acc = acc + pn * vn_ref[0, 0, i - jj * dil]
        oh_ref[i] = acc / den
        lh_ref[i] = jnp.broadcast_to(mx + jnp.log(den), (HEADS, LSE_LANES))
    lane = lax.broadcasted_iota(jnp.int32, (t_new, LSE_LANES), 1)
    lse_tile = jnp.zeros((t_new, LSE_LANES), F32)
    for hh in range(HEADS):
        o_ref[:, _head_cols(hh)] = oh_ref[:, hh, :]
        lse_tile = jnp.where(lane == hh, lh_ref[:, hh, :], lse_tile)
    o_ref[:, GROUP_WIDTH:] = lse_tile
    for copy in copies:
        copy.wait()


def _gather_attn(q, kc, vc, kn, vn, g):
    win, dil = ATTN_GROUPS[g]
    b, length = kc.shape[:2]
    t_new = q.shape[2]
    assert length == win and t_new <= dil * BAND
    new = pl.BlockSpec((1, 1, t_new, HEADS, HEAD_DIM), lambda bi: (g, bi, 0, 0, 0))
    cache = pl.BlockSpec((1, length, HEADS, HEAD_DIM), lambda bi: (bi, 0, 0, 0))
    whole_hbm = pl.BlockSpec(memory_space=pl.ANY)
    return pl.pallas_call(
        functools.partial(_gather_attn_kernel, dil=dil, t_new=t_new),
        grid=(b,),
        in_specs=[new, cache, cache, new, new],
        out_specs=[pl.BlockSpec((None, t_new, ATTN_OUT_COLS), lambda bi: (bi, 0, 0)), whole_hbm, whole_hbm],
        out_shape=[jax.ShapeDtypeStruct((b, t_new, ATTN_OUT_COLS), F32),
                   jax.ShapeDtypeStruct(kc.shape, F32), jax.ShapeDtypeStruct(vc.shape, F32)],
        scratch_shapes=[pltpu.VMEM((t_new, HEADS, HEAD_DIM), F32),
                        pltpu.VMEM((t_new, HEADS, LSE_LANES), F32),
                        pltpu.SemaphoreType.DMA((2, 2))],
        compiler_params=_params("arbitrary"),
        name="gather_attn_g%d" % g,
    )(q, kc, vc, kn, vn)


def _merge_kernel(h_ref, y_ref, o0_ref, o1_ref, o2_ref, gate_ref, wp_ref, wa_ref, wo_ref, out_ref,
                  oa_ref, tok_ref, *, dils):
    o_refs = (o0_ref, o1_ref, o2_ref)
    tm = out_ref.shape[0]
    for gi, dil in enumerate(dils):
        if dil == 1:
            continue
        n = tm // dil
        for r in range(dil):
            for c in range(ATTN_OUT_TILES):
                tok_ref[gi, c, pl.ds(r, n, stride=dil), :] = o_refs[gi][r, :, c * 128:(c + 1) * 128]

    def tile(gi, c):
        if dils[gi] == 1:
            return o_refs[gi][0, :, c * 128:(c + 1) * 128]
        return tok_ref[gi, c]

    lses = [tile(gi, HEADS) for gi in range(N_GROUPS)]
    mx = jnp.maximum(jnp.maximum(lses[0], lses[1]), lses[2])
    es = [jnp.exp(l - mx) for l in lses]
    inv = 1.0 / (es[0] + es[1] + es[2])
    wts = [e * inv for e in es]
    for hh in range(HEADS):
        acc = wts[0][:, hh:hh + 1] * tile(0, hh)
        for gi in range(1, N_GROUPS):
            acc = acc + wts[gi][:, hh:hh + 1] * tile(gi, hh)
        oa_ref[:, _head_cols(hh)] = acc.astype(BF16)
    a_attn = jnp.dot(oa_ref[...], wa_ref[...], preferred_element_type=F32)
    a_pool = jnp.dot(y_ref[...], wp_ref[...], preferred_element_type=F32)
    d = out_ref.shape[1]
    mix = gate_ref[:, :d].astype(F32) * a_pool + gate_ref[:, d:].astype(F32) * a_attn
    out_ref[...] = h_ref[...] + jnp.dot(mix.astype(BF16), wo_ref[...], preferred_element_type=F32)


def _merge(h, y_pool, o_groups, gates, w_pool_out, w_attn_out, w_out, *, tm=256):
    m, d = h.shape
    dils = tuple(o.shape[1] for o in o_groups)
    seq = o_groups[0].shape[1] * o_groups[0].shape[2]
    tm = _row_tile(seq, tm)
    per_seq = seq // tm
    row = lambda i: (i, 0)
    fixed = lambda i: (0, 0)
    o_specs = [pl.BlockSpec((None, dil, tm // dil, ATTN_OUT_COLS), lambda i: (i // per_seq, 0, i % per_seq, 0))
               for dil in dils]
    return pl.pallas_call(
        functools.partial(_merge_kernel, dils=dils),
        grid=(m // tm,),
        in_specs=[
            pl.BlockSpec((tm, d), row),
            pl.BlockSpec((tm, POOL_WIDTH), row),
            *o_specs,
            pl.BlockSpec((tm, 2 * d), row),
            _resident(w_pool_out.shape, fixed),
            _resident(w_attn_out.shape, fixed),
            _resident(w_out.shape, fixed),
        ],
        out_specs=pl.BlockSpec((tm, d), row),
        out_shape=jax.ShapeDtypeStruct((m, d), F32),
        scratch_shapes=[pltpu.VMEM((tm, GROUP_WIDTH), BF16),
                        pltpu.VMEM((N_GROUPS, ATTN_OUT_TILES, tm, 128), F32)],
        compiler_params=_params("parallel"),
        name="merge",
    )(h, y_pool, *o_groups, gates, w_pool_out, w_attn_out, w_out)


def _ple_kernel(h_ref, p_ref, g_ref, wg_ref, wp_ref, out_ref):
    h = h_ref[...]
    hn = _rms(h, g_ref[...]).astype(BF16)
    gate = jax.nn.sigmoid(jnp.dot(hn, wg_ref[...], preferred_element_type=F32))
    emb = jnp.dot(p_ref[...].astype(BF16), wp_ref[...], preferred_element_type=F32)
    out_ref[...] = h + gate * emb


def _ple(h, p, g, w_gate, w_proj, *, tm=512):
    m, d = h.shape
    tm = _row_tile(m, tm)
    row = lambda i: (i, 0)
    fixed = lambda i: (0, 0)
    return pl.pallas_call(
        _ple_kernel,
        grid=(m // tm,),
        in_specs=[
            pl.BlockSpec((tm, d), row),
            pl.BlockSpec((tm, p.shape[1]), row),
            _resident((1, d), fixed),
            _resident(w_gate.shape, fixed),
            _resident(w_proj.shape, fixed),
        ],
        out_specs=pl.BlockSpec((tm, d), row),
        out_shape=jax.ShapeDtypeStruct((m, d), F32),
        compiler_params=_params("parallel"),
        name="ple",
    )(h, p, g, w_gate, w_proj)


def _layer(x, p, w, *, prompt, pool_prev=None, caches=None):
    b, t, d = x.shape
    m = b * t
    h1, u = _ffn(x.reshape(m, d), w["ffn1_norm"], w["ffn1_w_in"], w["ffn1_w_out"], w["mix_norm"])
    outs_a = _proj_a(u, w["w_zq"], w["w_gate"], w["q_norm"], batch=b, seq=t, prompt=prompt)
    outs_b = _proj_b(u, w["w_kv"], w["k_norm"], batch=b, seq=t, prompt=prompt)
    z, gates = outs_a[0], outs_a[-1]
    z3 = z.reshape(b, t, POOL_WIDTH)
    if prompt:
        qs = outs_a[1:1 + N_GROUPS]
        kbs, vbs = outs_b[0:N_GROUPS], outs_b[N_GROUPS:2 * N_GROUPS]
        kfs, vfs = outs_b[2 * N_GROUPS:3 * N_GROUPS], outs_b[3 * N_GROUPS:]
        y_pool = _pool(z3, z3, w["pool_map"], w["pool_scale"], tp=512, pos0=0, zero_first_halo=True)
        new_pool = z3[:, t - POOL_STATE:]
        o_groups = [_band_attn(qs[g], kbs[g], vbs[g], g) for g in range(N_GROUPS)]
        new_kv = []
        for g in range(N_GROUPS):
            new_kv += [kfs[g].reshape(b, -1, HEADS, HEAD_DIM), vfs[g].reshape(b, -1, HEADS, HEAD_DIM)]
    else:
        qf = outs_a[1].reshape(N_GROUPS, b, t, HEADS, HEAD_DIM)
        kn = outs_b[0].reshape(N_GROUPS, b, t, HEADS, HEAD_DIM)
        vn = outs_b[1].reshape(N_GROUPS, b, t, HEADS, HEAD_DIM)
        halo = jnp.pad(pool_prev, ((0, 0), (POOL_HALO - POOL_STATE, 0), (0, 0)))
        y_pool = _pool(z3, halo, w["pool_map"], w["pool_scale"], tp=t, pos0=PAST_LEN, zero_first_halo=False)
        new_pool = jnp.concatenate([pool_prev, z3], axis=1)[:, t:]
        o_groups, new_kv = [], []
        for g in range(N_GROUPS):
            kc, vc = caches[2 * g], caches[2 * g + 1]
            o, k_new, v_new = _gather_attn(qf, kc, vc, kn, vn, g)
            o_groups.append(o.reshape(1, 1, m, ATTN_OUT_COLS))
            new_kv += [k_new, v_new]
    h2 = _merge(h1, y_pool.reshape(m, POOL_WIDTH), o_groups, gates,
                w["w_pool_out"], w["w_attn_out"], w["w_out"])
    h3 = _ffn(h2, w["ffn2_norm"], w["ffn2_w_in"], w["ffn2_w_out"])
    h4 = _ple(h3, p.reshape(m, -1), w["ple_norm"], w["ple_gate"], w["ple_proj"])
    return h4.reshape(b, t, d), new_pool, new_kv


def kernel(x_prompt, x_sample, p_prompt, p_sample, state_pool, cache_k0, cache_v0, cache_k1, cache_v1, cache_k2, cache_v2, ffn1_norm, ffn1_w_in, ffn1_w_out, mix_norm, w_in, q_norm, k_norm, pool_map, pool_scale, w_pool_out, w_attn_out, w_out, ffn2_norm, ffn2_w_in, ffn2_w_out, ple_norm, ple_gate, ple_proj):
    caches = (cache_k0, cache_v0, cache_k1, cache_v1, cache_k2, cache_v2)
    depth = ffn1_norm.shape[0]
    kv0 = POOL_WIDTH + N_GROUPS * GROUP_WIDTH
    gate0 = kv0 + 2 * N_GROUPS * GROUP_WIDTH
    hp, hs = x_prompt, x_sample
    pool_p, pool_s, kv_p, kv_s = [], [], [], []
    for i in range(depth):
        row = lambda a: a[i][None, :]
        w = dict(
            ffn1_norm=row(ffn1_norm), ffn1_w_in=ffn1_w_in[i].astype(BF16), ffn1_w_out=ffn1_w_out[i].astype(BF16),
            mix_norm=row(mix_norm), q_norm=row(q_norm), k_norm=row(k_norm),
            w_zq=w_in[i, :, :kv0].astype(BF16), w_kv=w_in[i, :, kv0:gate0].astype(BF16),
            w_gate=w_in[i, :, gate0:].astype(BF16),
            pool_map=pool_map[i].astype(BF16), pool_scale=row(pool_scale),
            w_pool_out=w_pool_out[i].astype(BF16), w_attn_out=w_attn_out[i].astype(BF16),
            w_out=w_out[i].astype(BF16),
            ffn2_norm=row(ffn2_norm), ffn2_w_in=ffn2_w_in[i].astype(BF16), ffn2_w_out=ffn2_w_out[i].astype(BF16),
            ple_norm=row(ple_norm), ple_gate=ple_gate[i].astype(BF16), ple_proj=ple_proj[i].astype(BF16),
        )
        hp, pp, kvp = _layer(hp, p_prompt[i], w, prompt=True)
        hs, ps, kvs = _layer(hs, p_sample[i], w, prompt=False, pool_prev=state_pool[i],
                             caches=tuple(c[i] for c in caches))
        pool_p.append(pp)
        pool_s.append(ps)
        kv_p.append(kvp)
        kv_s.append(kvs)
    stack = lambda rows: jnp.stack(rows)
    kp = [stack([r[j] for r in kv_p]) for j in range(2 * N_GROUPS)]
    ks = [stack([r[j] for r in kv_s]) for j in range(2 * N_GROUPS)]
    return (hp, hs, stack(pool_p), *kp, stack(pool_s), *ks)
```

```python
import functools

import jax
import jax.numpy as jnp
from jax import lax
from jax.experimental import pallas as pl
from jax.experimental.pallas import tpu as pltpu

F32 = jnp.float32
BF16 = jnp.bfloat16

POOL_WINDOWS = (2, 4, 8, 16)
POOL_GROUP_WIDTH = 256
POOL_WIDTH = POOL_GROUP_WIDTH * len(POOL_WINDOWS)
POOL_STATE = max(POOL_WINDOWS) - 1
POOL_HALO = 16
HEAD_DIM = 128
HEADS = 8
GROUP_WIDTH = HEADS * HEAD_DIM
ATTN_GROUPS = ((128, 1), (512, 4), (2048, 16))
DILS = tuple(d for _, d in ATTN_GROUPS)
N_GROUPS = len(ATTN_GROUPS)
SLAB_OF = tuple(sum(1 for d in DILS[:g] if d > 1) for g in range(N_GROUPS))
N_SLABS = sum(1 for d in DILS if d > 1)
BAND = 128
LSE_LANES = 128
ATTN_OUT_COLS = GROUP_WIDTH + LSE_LANES
ATTN_OUT_TILES = ATTN_OUT_COLS // 128
RMS_EPS = 1e-6
NEG_INF = -1e30
LOG2_E = 1.4426950408889634
LN_2 = 0.6931471805599453
Q_SCALE = HEAD_DIM ** -0.5 * LOG2_E
PAST_LEN = 16384

VMEM_LIMIT = 56 * 1024 * 1024


def _params(*semantics):
    return pltpu.CompilerParams(dimension_semantics=semantics, vmem_limit_bytes=VMEM_LIMIT)


def _resident(shape, index_map):
    return pl.BlockSpec(shape, index_map, pipeline_mode=pl.Buffered(1))


def _rms(x, g):
    return x * lax.rsqrt(jnp.mean(x * x, axis=-1, keepdims=True) + RMS_EPS) * g


def _row_tile(m, want):
    t = min(m, want)
    assert m % t == 0, (m, t)
    return t


def _head_cols(hh):
    return slice(hh * HEAD_DIM, (hh + 1) * HEAD_DIM)


def _ffn_kernel(*refs, emit_norm):
    if emit_norm:
        x_ref, g_ref, wa_ref, wb_ref, wo_ref, gn_ref, o_ref, u_ref, xn_ref = refs
    else:
        x_ref, g_ref, wa_ref, wb_ref, wo_ref, o_ref, xn_ref = refs
    k = pl.program_id(1)

    @pl.when(k == 0)
    def _():
        x = x_ref[...]
        xn_ref[...] = _rms(x, g_ref[...]).astype(BF16)
        o_ref[...] = x

    xn = xn_ref[...]
    a = jnp.dot(xn, wa_ref[...], preferred_element_type=F32)
    b = jnp.dot(xn, wb_ref[...], preferred_element_type=F32)
    hid = (0.5 * (a * jax.nn.sigmoid(a)) * b).astype(BF16)
    o_ref[...] += jnp.dot(hid, wo_ref[...], preferred_element_type=F32)

    if emit_norm:
        @pl.when(k == pl.num_programs(1) - 1)
        def _():
            u_ref[...] = _rms(o_ref[...], gn_ref[...]).astype(BF16)


def _ffn(x, g, w_in, w_out, g_next=None, *, tm=512, tf=512):
    m, d = x.shape
    d_ff = w_out.shape[0]
    tm = _row_tile(m, tm)
    tf = _row_tile(d_ff, tf)
    nk = d_ff // tf
    emit_norm = g_next is not None
    row = pl.BlockSpec((tm, d), lambda i, k: (i, 0))
    vec = _resident((1, d), lambda i, k: (0, 0))
    in_specs = [row, vec,
                pl.BlockSpec((d, tf), lambda i, k: (0, k)),
                pl.BlockSpec((d, tf), lambda i, k: (0, k + nk)),
                pl.BlockSpec((tf, d), lambda i, k: (k, 0))]
    args = [x, g, w_in, w_in, w_out]
    out_specs, out_shape = row, jax.ShapeDtypeStruct((m, d), F32)
    if emit_norm:
        in_specs.append(vec)
        args.append(g_next)
        out_specs = [row, row]
        out_shape = [out_shape, jax.ShapeDtypeStruct((m, d), BF16)]
    return pl.pallas_call(
        functools.partial(_ffn_kernel, emit_norm=emit_norm),
        grid=(m // tm, nk),
        in_specs=in_specs,
        out_specs=out_specs,
        out_shape=out_shape,
        scratch_shapes=[pltpu.VMEM((tm, d), BF16)],
        compiler_params=_params("parallel", "arbitrary"),
        name="ffn",
    )(*args)


MXU_COLS = 512


def _tile_heads(u_ref, w_ref, t):
    heads = []
    for c in range(t * GROUP_WIDTH, (t + 1) * GROUP_WIDTH, MXU_COLS):
        part = jnp.dot(u_ref[...], w_ref[:, c:c + MXU_COLS], preferred_element_type=F32)
        heads += [part[:, i:i + HEAD_DIM] for i in range(0, MXU_COLS, HEAD_DIM)]
    return heads


def _store_cols(ref, heads, dtype):
    for hh, x in enumerate(heads):
        ref[:, _head_cols(hh)] = x.astype(dtype)


def _store_head_rows(ref, heads):
    for hh, x in enumerate(heads):
        ref[pl.ds(hh, x.shape[0], stride=HEADS), :] = x


def _store_residue_major(ref, slab_ref, heads, dil):
    if dil == 1:
        _store_cols(ref.at[0], heads, BF16)
        return
    n = heads[0].shape[0] // dil
    for hh, x in enumerate(heads):
        slab_ref[hh] = x
    for r in range(dil):
        for hh in range(HEADS):
            ref[r, :, _head_cols(hh)] = slab_ref[hh, pl.ds(r, n, stride=dil), :].astype(BF16)


_HEAVY_FIRST = tuple(sorted(range(N_GROUPS), key=lambda g: -DILS[g]))


def _proj_a_kernel(*refs, prompt):
    if prompt:
        u_ref, wzq_ref, wg_ref, qn_ref, z_ref, q0_ref, q1_ref, q2_ref, gate_ref, slab_ref = refs
        q_refs = (q0_ref, q1_ref, q2_ref)
    else:
        u_ref, wzq_ref, wg_ref, qn_ref, z_ref, q_ref, gate_ref = refs
    for g in _HEAVY_FIRST:
        q = [_rms(x, qn_ref[...]) * Q_SCALE for x in _tile_heads(u_ref, wzq_ref, 1 + g)]
        if prompt:
            _store_residue_major(q_refs[g], slab_ref.at[SLAB_OF[g]], q, DILS[g])
        else:
            _store_head_rows(q_ref.at[g], q)
    for t in range(wg_ref.shape[1] // GROUP_WIDTH):
        gate = [jax.nn.sigmoid(x) for x in _tile_heads(u_ref, wg_ref, t)]
        _store_cols(gate_ref.at[:, t * GROUP_WIDTH:(t + 1) * GROUP_WIDTH], gate, BF16)
    _store_cols(z_ref, _tile_heads(u_ref, wzq_ref, 0), F32)


def _proj_b_kernel(*refs, prompt):
    if prompt:
        (u_ref, w_ref, kn_ref, kb0_ref, kb1_ref, kb2_ref, vb0_ref, vb1_ref, vb2_ref,
         kf0_ref, kf1_ref, kf2_ref, vf0_ref, vf1_ref, vf2_ref, slab_ref) = refs
        kb_refs, vb_refs = (kb0_ref, kb1_ref, kb2_ref), (vb0_ref, vb1_ref, vb2_ref)
        kf_refs, vf_refs = (kf0_ref, kf1_ref, kf2_ref), (vf0_ref, vf1_ref, vf2_ref)
    else:
        u_ref, w_ref, kn_ref, kf_ref, vf_ref = refs
        kf_refs = [kf_ref.at[g] for g in range(N_GROUPS)]
        vf_refs = [vf_ref.at[g] for g in range(N_GROUPS)]
    tm = u_ref.shape[0]
    for g in _HEAVY_FIRST:
        k = [_rms(x, kn_ref[...]) for x in _tile_heads(u_ref, w_ref, g)]
        v = _tile_heads(u_ref, w_ref, N_GROUPS + g)
        kept = kf_refs[g].shape[0] // HEADS
        _store_head_rows(kf_refs[g], [x[tm - kept:] for x in k])
        _store_head_rows(vf_refs[g], [x[tm - kept:] for x in v])
        if prompt:
            _store_residue_major(kb_refs[g], slab_ref.at[0, SLAB_OF[g]], k, DILS[g])
            _store_residue_major(vb_refs[g], slab_ref.at[1, SLAB_OF[g]], v, DILS[g])


def _residue_specs(batch, seq, tm, dtype):
    per_seq = seq // tm
    specs, shapes = [], []
    for dil in DILS:
        specs.append(pl.BlockSpec((None, dil, tm // dil, GROUP_WIDTH),
                                  lambda i: (i // per_seq, 0, i % per_seq, 0)))
        shapes.append(jax.ShapeDtypeStruct((batch, dil, seq // dil, GROUP_WIDTH), dtype))
    return specs, shapes


def _head_rows_spec(n_arrays, m, tm):
    return (pl.BlockSpec((n_arrays, tm * HEADS, HEAD_DIM), lambda i: (0, i, 0)),
            jax.ShapeDtypeStruct((n_arrays, m * HEADS, HEAD_DIM), F32))


def _suffix_head_rows_spec(batch, seq, tm, keep):
    rows = min(keep, tm)
    assert keep % rows == 0
    blocks, per_seq = keep // rows, seq // tm
    spec = pl.BlockSpec((None, None, rows * HEADS, HEAD_DIM),
                        lambda i: (i // per_seq, jnp.maximum(i % per_seq - (per_seq - blocks), 0), 0, 0))
    return spec, jax.ShapeDtypeStruct((batch, blocks, rows * HEADS, HEAD_DIM), F32)


def _proj_a(u, w_zq, w_gate, qn, *, batch, seq, prompt, tm=256):
    m, d = u.shape
    tm = _row_tile(seq if prompt else m, tm)
    fixed = lambda i: (0, 0)
    rows = lambda width: pl.BlockSpec((tm, width), lambda i: (i, 0))
    scratch = []
    if prompt:
        q_specs, q_shapes = _residue_specs(batch, seq, tm, BF16)
        scratch = [pltpu.VMEM((N_SLABS, HEADS, tm, HEAD_DIM), F32)]
    else:
        spec, shape = _head_rows_spec(N_GROUPS, m, tm)
        q_specs, q_shapes = [spec], [shape]
    return pl.pallas_call(
        functools.partial(_proj_a_kernel, prompt=prompt),
        grid=(m // tm,),
        in_specs=[rows(d), _resident(w_zq.shape, fixed), _resident(w_gate.shape, fixed),
                  _resident((1, HEAD_DIM), fixed)],
        out_specs=[rows(GROUP_WIDTH), *q_specs, rows(w_gate.shape[1])],
        out_shape=[jax.ShapeDtypeStruct((m, GROUP_WIDTH), F32), *q_shapes,
                   jax.ShapeDtypeStruct((m, w_gate.shape[1]), BF16)],
        scratch_shapes=scratch,
        compiler_params=_params("parallel"),
        name="proj_a",
    )(u, w_zq, w_gate, qn)


def _proj_b(u, w_kv, kn, *, batch, seq, prompt, tm=256):
    m, d = u.shape
    tm = _row_tile(seq if prompt else m, tm)
    fixed = lambda i: (0, 0)
    scratch = []
    if prompt:
        kb_specs, kb_shapes = _residue_specs(batch, seq, tm, BF16)
        kf = [_suffix_head_rows_spec(batch, seq, tm, min(window, seq)) for window, _ in ATTN_GROUPS]
        kf_specs, kf_shapes = [s for s, _ in kf], [s for _, s in kf]
        out_specs = [*kb_specs, *kb_specs, *kf_specs, *kf_specs]
        out_shape = [*kb_shapes, *kb_shapes, *kf_shapes, *kf_shapes]
        scratch = [pltpu.VMEM((2, N_SLABS, HEADS, tm, HEAD_DIM), F32)]
    else:
        spec, shape = _head_rows_spec(N_GROUPS, m, tm)
        out_specs, out_shape = [spec, spec], [shape, shape]
    return pl.pallas_call(
        functools.partial(_proj_b_kernel, prompt=prompt),
        grid=(m // tm,),
        in_specs=[pl.BlockSpec((tm, d), lambda i: (i, 0)), _resident(w_kv.shape, fixed),
                  _resident((1, HEAD_DIM), fixed)],
        out_specs=out_specs,
        out_shape=out_shape,
        scratch_shapes=scratch,
        compiler_params=_params("arbitrary"),
        name="proj_b",
    )(u, w_kv, kn)


def _pool_kernel(z_ref, halo_ref, pm_ref, ps_ref, y_ref, ext_ref, *, tp, pos0, zero_first_halo):
    i = pl.program_id(1)
    halo = halo_ref[...]
    if zero_first_halo:
        halo = jnp.where(i == 0, 0.0, halo)
    ext_ref[0:POOL_HALO, :] = halo
    ext_ref[POOL_HALO:, :] = z_ref[...]
    pos = pos0 + i * tp + lax.broadcasted_iota(jnp.int32, (tp, 1), 0)
    for gi, w in enumerate(POOL_WINDOWS):
        cols = slice(gi * POOL_GROUP_WIDTH, (gi + 1) * POOL_GROUP_WIDTH)
        zc = ext_ref[POOL_HALO:POOL_HALO + tp, cols]
        wsum = zc
        for back in range(1, w):
            wsum = wsum + ext_ref[POOL_HALO - back:POOL_HALO - back + tp, cols]
        cnt = jnp.minimum(w, pos + 1).astype(F32)
        pooled = (wsum / cnt - zc).astype(BF16)
        y = jnp.dot(pooled, pm_ref[gi], preferred_element_type=F32) * ps_ref[:, cols]
        y_ref[:, cols] = y.astype(BF16)


def _pool(z, halo, pool_map, pool_scale, *, tp, pos0, zero_first_halo):
    b, t, p = z.shape
    tp = _row_tile(t, tp)
    hb = tp // POOL_HALO
    if halo is z:
        halo_map = lambda bi, i: (bi, jnp.maximum(i * hb - 1, 0), 0)
    else:
        assert t == tp and halo.shape == (b, POOL_HALO, p)
        halo_map = lambda bi, i: (bi, 0, 0)
    return pl.pallas_call(
        functools.partial(_pool_kernel, tp=tp, pos0=pos0, zero_first_halo=zero_first_halo),
        grid=(b, t // tp),
        in_specs=[
            pl.BlockSpec((None, tp, p), lambda bi, i: (bi, i, 0)),
            pl.BlockSpec((None, POOL_HALO, p), halo_map),
            _resident(pool_map.shape, lambda bi, i: (0, 0, 0)),
            _resident((1, p), lambda bi, i: (0, 0)),
        ],
        out_specs=pl.BlockSpec((None, tp, p), lambda bi, i: (bi, i, 0)),
        out_shape=jax.ShapeDtypeStruct((b, t, p), BF16),
        scratch_shapes=[pltpu.VMEM((POOL_HALO + tp, p), F32)],
        compiler_params=_params("parallel", "arbitrary"),
        name="pool",
    )(z, halo, pool_map, pool_scale)


def _band_attn_kernel(q_ref, k_ref, v_ref, o_ref, *, nblk, kw, dil):
    rb = q_ref.shape[0]
    i = pl.program_id(2)

    def body(it, carry):
        rl, nl = it // nblk, it % nblk
        n = i * nblk + nl
        if kw == BAND:
            kstart, off = 0, 0
        else:
            kstart = pl.multiple_of(jnp.maximum(n - 1, 0) * BAND, BAND)
            off = n * BAND - kstart
        qstart = pl.multiple_of(nl * BAND, BAND)
        steps = (off + lax.broadcasted_iota(jnp.int32, (BAND, kw), 0)
                 - lax.broadcasted_iota(jnp.int32, (BAND, kw), 1))
        valid = jnp.abs(steps - BAND // 2) <= BAND // 2
        neg_dist = jnp.where(valid, (steps * dil).astype(F32) * -LOG2_E, NEG_INF)
        lane = lax.broadcasted_iota(jnp.int32, (BAND, LSE_LANES), 1)
        lse_tile = jnp.zeros((BAND, LSE_LANES), F32)
        scores = []
        for hh in range(HEADS):
            qh = q_ref[rl, pl.ds(qstart, BAND), _head_cols(hh)]
            kh = k_ref[rl, pl.ds(kstart, kw), _head_cols(hh)]
            scores.append(lax.dot_general(qh, kh, (((1,), (1,)), ((), ())), preferred_element_type=F32))
        probs, dens = [], []
        for hh in range(HEADS):
            logits = scores[hh] + (2.0 ** -(hh + 1)) * neg_dist
            mx = jnp.max(logits, axis=-1, keepdims=True)
            p = jnp.exp2(logits - mx)
            den = jnp.sum(p, axis=-1, keepdims=True)
            probs.append(p.astype(BF16))
            dens.append(den)
            lse_tile = jnp.where(lane == hh, (mx + jnp.log2(den)) * LN_2, lse_tile)
        for hh in range(HEADS):
            vh = v_ref[rl, pl.ds(kstart, kw), _head_cols(hh)]
            o = jnp.dot(probs[hh], vh, preferred_element_type=F32) / dens[hh]
            o_ref[rl, pl.ds(qstart, BAND), _head_cols(hh)] = o
        o_ref[rl, pl.ds(qstart, BAND), GROUP_WIDTH:] = lse_tile
        return carry

    lax.fori_loop(0, rb * nblk, body, 0)


def _band_attn(q, k, v, g, *, blocks_per_step=4):
    batch, dil, mr, _ = q.shape
    assert dil == DILS[g] and mr % BAND == 0
    tq = min(mr, blocks_per_step * BAND)
    nblk = tq // BAND
    rb = min(dil, blocks_per_step // nblk)
    kw = min(mr, 2 * BAND)
    return pl.pallas_call(
        functools.partial(_band_attn_kernel, nblk=nblk, kw=kw, dil=dil),
        grid=(batch, dil // rb, mr // tq),
        in_specs=[
            pl.BlockSpec((None, rb, tq, GROUP_WIDTH), lambda b, r, i: (b, r, i, 0)),
            pl.BlockSpec((None, rb, mr, GROUP_WIDTH), lambda b, r, i: (b, r, 0, 0)),
            pl.BlockSpec((None, rb, mr, GROUP_WIDTH), lambda b, r, i: (b, r, 0, 0)),
        ],
        out_specs=pl.BlockSpec((None, rb, tq, ATTN_OUT_COLS), lambda b, r, i: (b, r, i, 0)),
        out_shape=jax.ShapeDtypeStruct((batch, dil, mr, ATTN_OUT_COLS), F32),
        compiler_params=_params("parallel", "parallel", "arbitrary"),
        name="band_attn_g%d" % g,
    )(q, k, v)


def _roll_copies(b, cache_ref, new_ref, out_hbm, sems, which):
    length, t_new = cache_ref.shape[1], new_ref.shape[2]
    kept = length - t_new
    return (pltpu.make_async_copy(cache_ref.at[0, pl.ds(t_new, kept)],
                                  out_hbm.at[b, pl.ds(0, kept)], sems.at[which, 0]),
            pltpu.make_async_copy(new_ref.at[0, 0], out_hbm.at[b, pl.ds(kept, t_new)], sems.at[which, 1]))


def _gather_attn_kernel(q_ref, kc_ref, vc_ref, kn_ref, vn_ref, o_ref, ko_hbm, vo_hbm, oh_ref, lh_ref, sems,
                        *, dil, t_new):
    b = pl.program_id(0)
    copies = (*_roll_copies(b, kc_ref, kn_ref, ko_hbm, sems, 0),
              *_roll_copies(b, vc_ref, vn_ref, vo_hbm, sems, 1))
    for copy in copies:
        copy.start()
    length = kc_ref.shape[1]
    head = lax.broadcasted_iota(jnp.int32, (HEADS, 1), 0)
    slope = lax.bitcast_convert_type((126 - head) << 23, F32) * LOG2_E
    for i in range(t_new):
        qi = q_ref[0, 0, i]
        n_new = i // dil + 1
        n_old = BAND + 1 - n_new
        first = length + i - BAND * dil
        assert first >= 0 and first + (n_old - 1) * dil < length
        kc = kc_ref[0, pl.ds(first, n_old, stride=dil)]
        vc = vc_ref[0, pl.ds(first, n_old, stride=dil)]
        steps = BAND - lax.broadcasted_iota(jnp.int32, (n_old, 1, 1), 0)
        sc = jnp.sum(kc * qi[None], axis=-1, keepdims=True) - slope[None] * (steps * dil).astype(F32)
        sn = [jnp.sum(kn_ref[0, 0, i - jj * dil] * qi, axis=-1, keepdims=True) - slope * float(jj * dil)
              for jj in range(n_new)]
        mx = jnp.max(sc, axis=0)
        for s in sn:
            mx = jnp.maximum(mx, s)
        pc = jnp.exp2(sc - mx[None])
        den = jnp.sum(pc, axis=0)
        acc = jnp.sum(pc * vc, axis=0)
        for jj, s in enumerate(sn):
            pn = jnp.exp2(s - mx)
            den = den + pn
            acc = acc + pn * vn_ref[0, 0, i - jj * dil]
        oh_ref[i] = acc / den
        lh_ref[i] = jnp.broadcast_to((mx + jnp.log2(den)) * LN_2, (HEADS, LSE_LANES))
    lane = lax.broadcasted_iota(jnp.int32, (t_new, LSE_LANES), 1)
    lse_tile = jnp.zeros((t_new, LSE_LANES), F32)
    for hh in range(HEADS):
        o_ref[:, _head_cols(hh)] = oh_ref[:, hh, :]
        lse_tile = jnp.where(lane == hh, lh_ref[:, hh, :], lse_tile)
    o_ref[:, GROUP_WIDTH:] = lse_tile
    for copy in copies:
        copy.wait()


def _gather_attn(q, kc, vc, kn, vn, g):
    win, dil = ATTN_GROUPS[g]
    b, length = kc.shape[:2]
    t_new = q.shape[2]
    assert length == win and t_new <= dil * BAND
    new = pl.BlockSpec((1, 1, t_new, HEADS, HEAD_DIM), lambda bi: (g, bi, 0, 0, 0))
    cache = pl.BlockSpec((1, length, HEADS, HEAD_DIM), lambda bi: (bi, 0, 0, 0))
    whole_hbm = pl.BlockSpec(memory_space=pl.ANY)
    return pl.pallas_call(
        functools.partial(_gather_attn_kernel, dil=dil, t_new=t_new),
        grid=(b,),
        in_specs=[new, cache, cache, new, new],
        out_specs=[pl.BlockSpec((None, t_new, ATTN_OUT_COLS), lambda bi: (bi, 0, 0)), whole_hbm, whole_hbm],
        out_shape=[jax.ShapeDtypeStruct((b, t_new, ATTN_OUT_COLS), F32),
                   jax.ShapeDtypeStruct(kc.shape, F32), jax.ShapeDtypeStruct(vc.shape, F32)],
        scratch_shapes=[pltpu.VMEM((t_new, HEADS, HEAD_DIM), F32),
                        pltpu.VMEM((t_new, HEADS, LSE_LANES), F32),
                        pltpu.SemaphoreType.DMA((2, 2))],
        compiler_params=_params("arbitrary"),
        name="gather_attn_g%d" % g,
    )(q, kc, vc, kn, vn)


def _merge_kernel(h_ref, y_ref, o0_ref, o1_ref, o2_ref, gate_ref, wp_ref, wa_ref, wo_ref, out_ref,
                  oa_ref, mix_ref, tok_ref, *, dils):
    o_refs = (o0_ref, o1_ref, o2_ref)
    tm, d = out_ref.shape
    s = pl.program_id(0)

    @pl.when(s == 0)
    def _():
        oa_ref[1] = jnp.zeros(oa_ref.shape[1:], BF16)
        mix_ref[1] = jnp.zeros(mix_ref.shape[1:], BF16)

    def tile(gi, c):
        if dils[gi] == 1:
            return o_refs[gi][0, :, c * 128:(c + 1) * 128]
        return tok_ref[gi, c]

    def step(fill, drain):
        for gi, dil in enumerate(dils):
            if dil == 1:
                continue
            n = tm // dil
            for r in range(dil):
                for c in range(ATTN_OUT_TILES):
                    tok_ref[gi, c, pl.ds(r, n, stride=dil), :] = o_refs[gi][r, :, c * 128:(c + 1) * 128]
        lses = [tile(gi, HEADS) for gi in range(N_GROUPS)]
        mx = jnp.maximum(jnp.maximum(lses[0], lses[1]), lses[2])
        es = [jnp.exp(l - mx) for l in lses]
        inv = 1.0 / (es[0] + es[1] + es[2])
        wts = [e * inv for e in es]
        for hh in range(HEADS):
            acc = wts[0][:, hh:hh + 1] * tile(0, hh)
            for gi in range(1, N_GROUPS):
                acc = acc + wts[gi][:, hh:hh + 1] * tile(gi, hh)
            oa_ref[fill, :, _head_cols(hh)] = acc.astype(BF16)

        a_attn = jnp.dot(oa_ref[drain], wa_ref[...], preferred_element_type=F32)
        a_pool = jnp.dot(y_ref[...], wp_ref[...], preferred_element_type=F32)
        mix = gate_ref[:, :d].astype(F32) * a_pool + gate_ref[:, d:].astype(F32) * a_attn
        mix_ref[fill] = mix.astype(BF16)

        out_ref[...] = h_ref[...] + jnp.dot(mix_ref[drain], wo_ref[...], preferred_element_type=F32)

    pl.when(s % 2 == 0)(functools.partial(step, 0, 1))
    pl.when(s % 2 == 1)(functools.partial(step, 1, 0))


def _merge(h, y_pool, o_groups, gates, w_pool_out, w_attn_out, w_out, *, tm=256):
    m, d = h.shape
    dils = tuple(o.shape[1] for o in o_groups)
    seq = o_groups[0].shape[1] * o_groups[0].shape[2]
    tm = _row_tile(seq, tm)
    per_seq, n_tiles = seq // tm, m // tm
    last = n_tiles - 1
    row1 = lambda s: (jnp.clip(s - 1, 0, last), 0)
    row2 = lambda s: (jnp.maximum(s - 2, 0), 0)
    fixed = lambda s: (0, 0)

    def attn_tile(s):
        i = jnp.minimum(s, last)
        return (i // per_seq, 0, i % per_seq, 0)

    o_specs = [pl.BlockSpec((None, dil, tm // dil, ATTN_OUT_COLS), attn_tile) for dil in dils]
    return pl.pallas_call(
        functools.partial(_merge_kernel, dils=dils),
        grid=(n_tiles + 2,),
        in_specs=[
            pl.BlockSpec((tm, d), row2),
            pl.BlockSpec((tm, POOL_WIDTH), row1),
            *o_specs,
            pl.BlockSpec((tm, 2 * d), row1),
            _resident(w_pool_out.shape, fixed),
            _resident(w_attn_out.shape, fixed),
            _resident(w_out.shape, fixed),
        ],
        out_specs=pl.BlockSpec((tm, d), row2),
        out_shape=jax.ShapeDtypeStruct((m, d), F32),
        scratch_shapes=[pltpu.VMEM((2, tm, GROUP_WIDTH), BF16),
                        pltpu.VMEM((2, tm, d), BF16),
                        pltpu.VMEM((N_GROUPS, ATTN_OUT_TILES, tm, 128), F32)],
        compiler_params=_params("arbitrary"),
        name="merge",
    )(h, y_pool, *o_groups, gates, w_pool_out, w_attn_out, w_out)


def _ple_kernel(h_ref, p_ref, g_ref, wg_ref, wp_ref, out_ref):
    h = h_ref[...]
    hn = _rms(h, g_ref[...]).astype(BF16)
    gate = jax.nn.sigmoid(jnp.dot(hn, wg_ref[...], preferred_element_type=F32))
    emb = jnp.dot(p_ref[...].astype(BF16), wp_ref[...], preferred_element_type=F32)
    out_ref[...] = h + gate * emb


def _ple(h, p, g, w_gate, w_proj, *, tm=512):
    m, d = h.shape
    tm = _row_tile(m, tm)
    row = lambda i: (i, 0)
    fixed = lambda i: (0, 0)
    return pl.pallas_call(
        _ple_kernel,
        grid=(m // tm,),
        in_specs=[
            pl.BlockSpec((tm, d), row),
            pl.BlockSpec((tm, p.shape[1]), row),
            _resident((1, d), fixed),
            _resident(w_gate.shape, fixed),
            _resident(w_proj.shape, fixed),
        ],
        out_specs=pl.BlockSpec((tm, d), row),
        out_shape=jax.ShapeDtypeStruct((m, d), F32),
        compiler_params=_params("parallel"),
        name="ple",
    )(h, p, g, w_gate, w_proj)


def _layer(x, p, w, *, prompt, pool_prev=None, caches=None):
    b, t, d = x.shape
    m = b * t
    h1, u = _ffn(x.reshape(m, d), w["ffn1_norm"], w["ffn1_w_in"], w["ffn1_w_out"], w["mix_norm"])
    outs_a = _proj_a(u, w["w_zq"], w["w_gate"], w["q_norm"], batch=b, seq=t, prompt=prompt)
    outs_b = _proj_b(u, w["w_kv"], w["k_norm"], batch=b, seq=t, prompt=prompt)
    z, gates = outs_a[0], outs_a[-1]
    z3 = z.reshape(b, t, POOL_WIDTH)
    if prompt:
        qs = outs_a[1:1 + N_GROUPS]
        kbs, vbs = outs_b[0:N_GROUPS], outs_b[N_GROUPS:2 * N_GROUPS]
        kfs, vfs = outs_b[2 * N_GROUPS:3 * N_GROUPS], outs_b[3 * N_GROUPS:]
        y_pool = _pool(z3, z3, w["pool_map"], w["pool_scale"], tp=512, pos0=0, zero_first_halo=True)
        new_pool = z3[:, t - POOL_STATE:]
        o_groups = [_band_attn(qs[g], kbs[g], vbs[g], g) for g in range(N_GROUPS)]
        new_kv = []
        for g in range(N_GROUPS):
            new_kv += [kfs[g].reshape(b, -1, HEADS, HEAD_DIM), vfs[g].reshape(b, -1, HEADS, HEAD_DIM)]
    else:
        qf = outs_a[1].reshape(N_GROUPS, b, t, HEADS, HEAD_DIM)
        kn = outs_b[0].reshape(N_GROUPS, b, t, HEADS, HEAD_DIM)
        vn = outs_b[1].reshape(N_GROUPS, b, t, HEADS, HEAD_DIM)
        halo = jnp.pad(pool_prev, ((0, 0), (POOL_HALO - POOL_STATE, 0), (0, 0)))
        y_pool = _pool(z3, halo, w["pool_map"], w["pool_scale"], tp=t, pos0=PAST_LEN, zero_first_halo=False)
        new_pool = jnp.concatenate([pool_prev, z3], axis=1)[:, t:]
        o_groups, new_kv = [], []
        for g in range(N_GROUPS):
            kc, vc = caches[2 * g], caches[2 * g + 1]
            o, k_new, v_new = _gather_attn(qf, kc, vc, kn, vn, g)
            o_groups.append(o.reshape(1, 1, m, ATTN_OUT_COLS))
            new_kv += [k_new, v_new]
    h2 = _merge(h1, y_pool.reshape(m, POOL_WIDTH), o_groups, gates,
                w["w_pool_out"], w["w_attn_out"], w["w_out"])
    h3 = _ffn(h2, w["ffn2_norm"], w["ffn2_w_in"], w["ffn2_w_out"])
    h4 = _ple(h3, p.reshape(m, -1), w["ple_norm"], w["ple_gate"], w["ple_proj"])
    return h4.reshape(b, t, d), new_pool, new_kv


def kernel(x_prompt, x_sample, p_prompt, p_sample, state_pool, cache_k0, cache_v0, cache_k1, cache_v1, cache_k2, cache_v2, ffn1_norm, ffn1_w_in, ffn1_w_out, mix_norm, w_in, q_norm, k_norm, pool_map, pool_scale, w_pool_out, w_attn_out, w_out, ffn2_norm, ffn2_w_in, ffn2_w_out, ple_norm, ple_gate, ple_proj):
    caches = (cache_k0, cache_v0, cache_k1, cache_v1, cache_k2, cache_v2)
    depth = ffn1_norm.shape[0]
    kv0 = POOL_WIDTH + N_GROUPS * GROUP_WIDTH
    gate0 = kv0 + 2 * N_GROUPS * GROUP_WIDTH
    hp, hs = x_prompt, x_sample
    pool_p, pool_s, kv_p, kv_s = [], [], [], []
    for i in range(depth):
        row = lambda a: a[i][None, :]
        w = dict(
            ffn1_norm=row(ffn1_norm), ffn1_w_in=ffn1_w_in[i].astype(BF16), ffn1_w_out=ffn1_w_out[i].astype(BF16),
            mix_norm=row(mix_norm), q_norm=row(q_norm), k_norm=row(k_norm),
            w_zq=w_in[i, :, :kv0].astype(BF16), w_kv=w_in[i, :, kv0:gate0].astype(BF16),
            w_gate=w_in[i, :, gate0:].astype(BF16),
            pool_map=pool_map[i].astype(BF16), pool_scale=row(pool_scale),
            w_pool_out=w_pool_out[i].astype(BF16), w_attn_out=w_attn_out[i].astype(BF16),
            w_out=w_out[i].astype(BF16),
            ffn2_norm=row(ffn2_norm), ffn2_w_in=ffn2_w_in[i].astype(BF16), ffn2_w_out=ffn2_w_out[i].astype(BF16),
            ple_norm=row(ple_norm), ple_gate=ple_gate[i].astype(BF16), ple_proj=ple_proj[i].astype(BF16),
        )
        hp, pp, kvp = _layer(hp, p_prompt[i], w, prompt=True)
        hs, ps, kvs = _layer(hs, p_sample[i], w, prompt=False, pool_prev=state_pool[i],
                             caches=tuple(c[i] for c in caches))
        pool_p.append(pp)
        pool_s.append(ps)
        kv_p.append(kvp)
        kv_s.append(kvs)
    stack = lambda rows: jnp.stack(rows)
    kp = [stack([r[j] for r in kv_p]) for j in range(2 * N_GROUPS)]
    ks = [stack([r[j] for r in kv_s]) for j in range(2 * N_GROUPS)]
    return (hp, hs, stack(pool_p), *kp, stack(pool_s), *ks)
```

```python
import functools

import jax
import jax.numpy as jnp
from jax import lax
from jax.experimental import pallas as pl
from jax.experimental.pallas import tpu as pltpu

F32 = jnp.float32
BF16 = jnp.bfloat16

POOL_WINDOWS = (2, 4, 8, 16)
POOL_GROUP_WIDTH = 256
POOL_WIDTH = POOL_GROUP_WIDTH * len(POOL_WINDOWS)
POOL_STATE = max(POOL_WINDOWS) - 1
POOL_HALO = 16
HEAD_DIM = 128
HEADS = 8
GROUP_WIDTH = HEADS * HEAD_DIM
ATTN_GROUPS = ((128, 1), (512, 4), (2048, 16))
DILS = tuple(d for _, d in ATTN_GROUPS)
N_GROUPS = len(ATTN_GROUPS)
SLAB_OF = tuple(sum(1 for d in DILS[:g] if d > 1) for g in range(N_GROUPS))
N_SLABS = sum(1 for d in DILS if d > 1)
BAND = 128
LSE_LANES = 128
ATTN_OUT_COLS = GROUP_WIDTH + LSE_LANES
ATTN_OUT_TILES = ATTN_OUT_COLS // 128
RMS_EPS = 1e-6
NEG_INF = -1e30
LOG2_E = 1.4426950408889634
LN_2 = 0.6931471805599453
Q_SCALE = HEAD_DIM ** -0.5 * LOG2_E
PAST_LEN = 16384

VMEM_LIMIT = 56 * 1024 * 1024


def _params(*semantics):
    return pltpu.CompilerParams(dimension_semantics=semantics, vmem_limit_bytes=VMEM_LIMIT)


def _resident(shape, index_map):
    return pl.BlockSpec(shape, index_map, pipeline_mode=pl.Buffered(1))


def _rms(x, g):
    return x * lax.rsqrt(jnp.mean(x * x, axis=-1, keepdims=True) + RMS_EPS) * g


def _row_tile(m, want):
    t = min(m, want)
    assert m % t == 0, (m, t)
    return t


def _head_cols(hh):
    return slice(hh * HEAD_DIM, (hh + 1) * HEAD_DIM)


def _roll_copies(cache_hbm, new_hbm, g, out_hbm, sems, which):
    length, t_new = cache_hbm.shape[1], new_hbm.shape[2]
    kept = length - t_new
    return (pltpu.make_async_copy(cache_hbm.at[:, pl.ds(t_new, kept)], out_hbm.at[:, pl.ds(0, kept)],
                                  sems.at[which, 0]),
            pltpu.make_async_copy(new_hbm.at[g], out_hbm.at[:, pl.ds(kept, t_new)], sems.at[which, 1]))


def _ffn_kernel(*refs, emit_norm, n_roll):
    refs = list(refs)
    x_ref, g_ref, wa_ref, wb_ref, wo_ref = (refs.pop(0) for _ in range(5))
    gn_ref = refs.pop(0) if emit_norm else None
    caches = [refs.pop(0) for _ in range(n_roll)]
    new_rows = [refs.pop(0) for _ in range(2)] if n_roll else []
    o_ref = refs.pop(0)
    u_ref = refs.pop(0) if emit_norm else None
    rolled = [refs.pop(0) for _ in range(n_roll)]
    xn_ref = refs.pop(0)
    sems = refs.pop(0) if n_roll else None
    i, k = pl.program_id(0), pl.program_id(1)
    last_k = k == pl.num_programs(1) - 1
    copies = []
    for j in range(n_roll):
        copies += _roll_copies(caches[j], new_rows[j % 2], j // 2, rolled[j], sems, j)

    if copies:
        @pl.when((i == 0) & (k == 0))
        def _():
            for copy in copies:
                copy.start()

    @pl.when(k == 0)
    def _():
        x = x_ref[...]
        xn_ref[...] = _rms(x, g_ref[...]).astype(BF16)
        o_ref[...] = x

    xn = xn_ref[...]
    a = jnp.dot(xn, wa_ref[...], preferred_element_type=F32)
    b = jnp.dot(xn, wb_ref[...], preferred_element_type=F32)
    hid = (0.5 * (a * jax.nn.sigmoid(a)) * b).astype(BF16)
    o_ref[...] += jnp.dot(hid, wo_ref[...], preferred_element_type=F32)

    if emit_norm:
        @pl.when(last_k)
        def _():
            u_ref[...] = _rms(o_ref[...], gn_ref[...]).astype(BF16)

    if copies:
        @pl.when((i == pl.num_programs(0) - 1) & last_k)
        def _():
            for copy in copies:
                copy.wait()


def _ffn(x, g, w_in, w_out, g_next=None, roll=None, *, tm=512, tf=512):
    m, d = x.shape
    d_ff = w_out.shape[0]
    tm = _row_tile(m, tm)
    tf = _row_tile(d_ff, tf)
    nk = d_ff // tf
    emit_norm = g_next is not None
    row = pl.BlockSpec((tm, d), lambda i, k: (i, 0))
    vec = _resident((1, d), lambda i, k: (0, 0))
    whole_hbm = pl.BlockSpec(memory_space=pl.ANY)
    in_specs = [row, vec,
                pl.BlockSpec((d, tf), lambda i, k: (0, k)),
                pl.BlockSpec((d, tf), lambda i, k: (0, k + nk)),
                pl.BlockSpec((tf, d), lambda i, k: (k, 0))]
    args = [x, g, w_in, w_in, w_out]
    out_specs, out_shape = [row], [jax.ShapeDtypeStruct((m, d), F32)]
    scratch = [pltpu.VMEM((tm, d), BF16)]
    if emit_norm:
        in_specs.append(vec)
        args.append(g_next)
        out_specs.append(row)
        out_shape.append(jax.ShapeDtypeStruct((m, d), BF16))
    n_roll = 0
    if roll is not None:
        caches, new_k, new_v = roll
        n_roll = len(caches)
        in_specs += [whole_hbm] * (n_roll + 2)
        args += [*caches, new_k, new_v]
        out_specs += [whole_hbm] * n_roll
        out_shape += [jax.ShapeDtypeStruct(c.shape, c.dtype) for c in caches]
        scratch.append(pltpu.SemaphoreType.DMA((n_roll, 2)))
    outs = pl.pallas_call(
        functools.partial(_ffn_kernel, emit_norm=emit_norm, n_roll=n_roll),
        grid=(m // tm, nk),
        in_specs=in_specs,
        out_specs=out_specs,
        out_shape=out_shape,
        scratch_shapes=scratch,
        compiler_params=_params("arbitrary" if n_roll else "parallel", "arbitrary"),
        name="ffn_roll" if n_roll else "ffn",
    )(*args)
    return outs[0] if len(outs) == 1 else outs


MXU_COLS = 512


def _tile_heads(u_ref, w_ref, t):
    heads = []
    for c in range(t * GROUP_WIDTH, (t + 1) * GROUP_WIDTH, MXU_COLS):
        part = jnp.dot(u_ref[...], w_ref[:, c:c + MXU_COLS], preferred_element_type=F32)
        heads += [part[:, i:i + HEAD_DIM] for i in range(0, MXU_COLS, HEAD_DIM)]
    return heads


def _store_cols(ref, heads, dtype):
    for hh, x in enumerate(heads):
        ref[:, _head_cols(hh)] = x.astype(dtype)


def _store_head_rows(ref, heads):
    for hh, x in enumerate(heads):
        ref[pl.ds(hh, x.shape[0], stride=HEADS), :] = x


def _store_residue_major(ref, slab_ref, heads, dil):
    if dil == 1:
        _store_cols(ref.at[0], heads, BF16)
        return
    n = heads[0].shape[0] // dil
    for hh, x in enumerate(heads):
        slab_ref[hh] = x
    for r in range(dil):
        for hh in range(HEADS):
            ref[r, :, _head_cols(hh)] = slab_ref[hh, pl.ds(r, n, stride=dil), :].astype(BF16)


_HEAVY_FIRST = tuple(sorted(range(N_GROUPS), key=lambda g: -DILS[g]))


def _proj_a_kernel(*refs, prompt):
    if prompt:
        u_ref, wzq_ref, wg_ref, qn_ref, z_ref, q0_ref, q1_ref, q2_ref, gate_ref, slab_ref = refs
        q_refs = (q0_ref, q1_ref, q2_ref)
    else:
        u_ref, wzq_ref, wg_ref, qn_ref, z_ref, q_ref, gate_ref = refs
    for g in _HEAVY_FIRST:
        q = [_rms(x, qn_ref[...]) * Q_SCALE for x in _tile_heads(u_ref, wzq_ref, 1 + g)]
        if prompt:
            _store_residue_major(q_refs[g], slab_ref.at[SLAB_OF[g]], q, DILS[g])
        else:
            _store_head_rows(q_ref.at[g], q)
    for t in range(wg_ref.shape[1] // GROUP_WIDTH):
        gate = [jax.nn.sigmoid(x) for x in _tile_heads(u_ref, wg_ref, t)]
        _store_cols(gate_ref.at[:, t * GROUP_WIDTH:(t + 1) * GROUP_WIDTH], gate, BF16)
    _store_cols(z_ref, _tile_heads(u_ref, wzq_ref, 0), F32)


def _proj_b_kernel(*refs, prompt):
    if prompt:
        (u_ref, w_ref, kn_ref, kb0_ref, kb1_ref, kb2_ref, vb0_ref, vb1_ref, vb2_ref,
         kf0_ref, kf1_ref, kf2_ref, vf0_ref, vf1_ref, vf2_ref, slab_ref) = refs
        kb_refs, vb_refs = (kb0_ref, kb1_ref, kb2_ref), (vb0_ref, vb1_ref, vb2_ref)
        kf_refs, vf_refs = (kf0_ref, kf1_ref, kf2_ref), (vf0_ref, vf1_ref, vf2_ref)
    else:
        u_ref, w_ref, kn_ref, kf_ref, vf_ref = refs
        kf_refs = [kf_ref.at[g] for g in range(N_GROUPS)]
        vf_refs = [vf_ref.at[g] for g in range(N_GROUPS)]
    tm = u_ref.shape[0]
    for g in _HEAVY_FIRST:
        k = [_rms(x, kn_ref[...]) for x in _tile_heads(u_ref, w_ref, g)]
        v = _tile_heads(u_ref, w_ref, N_GROUPS + g)
        kept = kf_refs[g].shape[0] // HEADS
        _store_head_rows(kf_refs[g], [x[tm - kept:] for x in k])
        _store_head_rows(vf_refs[g], [x[tm - kept:] for x in v])
        if prompt:
            _store_residue_major(kb_refs[g], slab_ref.at[0, SLAB_OF[g]], k, DILS[g])
            _store_residue_major(vb_refs[g], slab_ref.at[1, SLAB_OF[g]], v, DILS[g])


def _residue_specs(batch, seq, tm, dtype):
    per_seq = seq // tm
    specs, shapes = [], []
    for dil in DILS:
        specs.append(pl.BlockSpec((None, dil, tm // dil, GROUP_WIDTH),
                                  lambda i: (i // per_seq, 0, i % per_seq, 0)))
        shapes.append(jax.ShapeDtypeStruct((batch, dil, seq // dil, GROUP_WIDTH), dtype))
    return specs, shapes


def _head_rows_spec(n_arrays, m, tm):
    return (pl.BlockSpec((n_arrays, tm * HEADS, HEAD_DIM), lambda i: (0, i, 0)),
            jax.ShapeDtypeStruct((n_arrays, m * HEADS, HEAD_DIM), F32))


def _suffix_head_rows_spec(batch, seq, tm, keep):
    rows = min(keep, tm)
    assert keep % rows == 0
    blocks, per_seq = keep // rows, seq // tm
    spec = pl.BlockSpec((None, None, rows * HEADS, HEAD_DIM),
                        lambda i: (i // per_seq, jnp.maximum(i % per_seq - (per_seq - blocks), 0), 0, 0))
    return spec, jax.ShapeDtypeStruct((batch, blocks, rows * HEADS, HEAD_DIM), F32)


def _proj_a(u, w_zq, w_gate, qn, *, batch, seq, prompt, tm=256):
    m, d = u.shape
    tm = _row_tile(seq if prompt else m, tm)
    fixed = lambda i: (0, 0)
    rows = lambda width: pl.BlockSpec((tm, width), lambda i: (i, 0))
    scratch = []
    if prompt:
        q_specs, q_shapes = _residue_specs(batch, seq, tm, BF16)
        scratch = [pltpu.VMEM((N_SLABS, HEADS, tm, HEAD_DIM), F32)]
    else:
        spec, shape = _head_rows_spec(N_GROUPS, m, tm)
        q_specs, q_shapes = [spec], [shape]
    return pl.pallas_call(
        functools.partial(_proj_a_kernel, prompt=prompt),
        grid=(m // tm,),
        in_specs=[rows(d), _resident(w_zq.shape, fixed), _resident(w_gate.shape, fixed),
                  _resident((1, HEAD_DIM), fixed)],
        out_specs=[rows(GROUP_WIDTH), *q_specs, rows(w_gate.shape[1])],
        out_shape=[jax.ShapeDtypeStruct((m, GROUP_WIDTH), F32), *q_shapes,
                   jax.ShapeDtypeStruct((m, w_gate.shape[1]), BF16)],
        scratch_shapes=scratch,
        compiler_params=_params("parallel"),
        name="proj_a",
    )(u, w_zq, w_gate, qn)


def _proj_b(u, w_kv, kn, *, batch, seq, prompt, tm=256):
    m, d = u.shape
    tm = _row_tile(seq if prompt else m, tm)
    fixed = lambda i: (0, 0)
    scratch = []
    if prompt:
        kb_specs, kb_shapes = _residue_specs(batch, seq, tm, BF16)
        kf = [_suffix_head_rows_spec(batch, seq, tm, min(window, seq)) for window, _ in ATTN_GROUPS]
        kf_specs, kf_shapes = [s for s, _ in kf], [s for _, s in kf]
        out_specs = [*kb_specs, *kb_specs, *kf_specs, *kf_specs]
        out_shape = [*kb_shapes, *kb_shapes, *kf_shapes, *kf_shapes]
        scratch = [pltpu.VMEM((2, N_SLABS, HEADS, tm, HEAD_DIM), F32)]
    else:
        spec, shape = _head_rows_spec(N_GROUPS, m, tm)
        out_specs, out_shape = [spec, spec], [shape, shape]
    return pl.pallas_call(
        functools.partial(_proj_b_kernel, prompt=prompt),
        grid=(m // tm,),
        in_specs=[pl.BlockSpec((tm, d), lambda i: (i, 0)), _resident(w_kv.shape, fixed),
                  _resident((1, HEAD_DIM), fixed)],
        out_specs=out_specs,
        out_shape=out_shape,
        scratch_shapes=scratch,
        compiler_params=_params("arbitrary"),
        name="proj_b",
    )(u, w_kv, kn)


def _pool_kernel(z_ref, halo_ref, pm_ref, ps_ref, y_ref, ext_ref, *, tp, pos0, zero_first_halo):
    i = pl.program_id(1)
    halo = halo_ref[...]
    if zero_first_halo:
        halo = jnp.where(i == 0, 0.0, halo)
    ext_ref[0:POOL_HALO, :] = halo
    ext_ref[POOL_HALO:, :] = z_ref[...]
    pos = pos0 + i * tp + lax.broadcasted_iota(jnp.int32, (tp, 1), 0)
    for gi, w in enumerate(POOL_WINDOWS):
        cols = slice(gi * POOL_GROUP_WIDTH, (gi + 1) * POOL_GROUP_WIDTH)
        zc = ext_ref[POOL_HALO:POOL_HALO + tp, cols]
        wsum = zc
        for back in range(1, w):
            wsum = wsum + ext_ref[POOL_HALO - back:POOL_HALO - back + tp, cols]
        cnt = jnp.minimum(w, pos + 1).astype(F32)
        pooled = (wsum / cnt - zc).astype(BF16)
        y = jnp.dot(pooled, pm_ref[gi], preferred_element_type=F32) * ps_ref[:, cols]
        y_ref[:, cols] = y.astype(BF16)


def _pool(z, halo, pool_map, pool_scale, *, tp, pos0, zero_first_halo):
    b, t, p = z.shape
    tp = _row_tile(t, tp)
    hb = tp // POOL_HALO
    if halo is z:
        halo_map = lambda bi, i: (bi, jnp.maximum(i * hb - 1, 0), 0)
    else:
        assert t == tp and halo.shape == (b, POOL_HALO, p)
        halo_map = lambda bi, i: (bi, 0, 0)
    return pl.pallas_call(
        functools.partial(_pool_kernel, tp=tp, pos0=pos0, zero_first_halo=zero_first_halo),
        grid=(b, t // tp),
        in_specs=[
            pl.BlockSpec((None, tp, p), lambda bi, i: (bi, i, 0)),
            pl.BlockSpec((None, POOL_HALO, p), halo_map),
            _resident(pool_map.shape, lambda bi, i: (0, 0, 0)),
            _resident((1, p), lambda bi, i: (0, 0)),
        ],
        out_specs=pl.BlockSpec((None, tp, p), lambda bi, i: (bi, i, 0)),
        out_shape=jax.ShapeDtypeStruct((b, t, p), BF16),
        scratch_shapes=[pltpu.VMEM((POOL_HALO + tp, p), F32)],
        compiler_params=_params("parallel", "arbitrary"),
        name="pool",
    )(z, halo, pool_map, pool_scale)


def _band_attn_kernel(q_ref, k_ref, v_ref, o_ref, *, nblk, kw, dil):
    rb = q_ref.shape[0]
    i = pl.program_id(2)

    def body(it, carry):
        rl, nl = it // nblk, it % nblk
        n = i * nblk + nl
        if kw == BAND:
            kstart, off = 0, 0
        else:
            kstart = pl.multiple_of(jnp.maximum(n - 1, 0) * BAND, BAND)
            off = n * BAND - kstart
        qstart = pl.multiple_of(nl * BAND, BAND)
        steps = (off + lax.broadcasted_iota(jnp.int32, (BAND, kw), 0)
                 - lax.broadcasted_iota(jnp.int32, (BAND, kw), 1))
        valid = jnp.abs(steps - BAND // 2) <= BAND // 2
        neg_dist = jnp.where(valid, (steps * dil).astype(F32) * -LOG2_E, NEG_INF)
        lane = lax.broadcasted_iota(jnp.int32, (BAND, LSE_LANES), 1)
        lse_tile = jnp.zeros((BAND, LSE_LANES), F32)
        scores = []
        for hh in range(HEADS):
            qh = q_ref[rl, pl.ds(qstart, BAND), _head_cols(hh)]
            kh = k_ref[rl, pl.ds(kstart, kw), _head_cols(hh)]
            scores.append(lax.dot_general(qh, kh, (((1,), (1,)), ((), ())), preferred_element_type=F32))
        probs, dens = [], []
        for hh in range(HEADS):
            logits = scores[hh] + (2.0 ** -(hh + 1)) * neg_dist
            mx = jnp.max(logits, axis=-1, keepdims=True)
            p = jnp.exp2(logits - mx)
            den = jnp.sum(p, axis=-1, keepdims=True)
            probs.append(p.astype(BF16))
            dens.append(den)
            lse_tile = jnp.where(lane == hh, (mx + jnp.log2(den)) * LN_2, lse_tile)
        for hh in range(HEADS):
            vh = v_ref[rl, pl.ds(kstart, kw), _head_cols(hh)]
            o = jnp.dot(probs[hh], vh, preferred_element_type=F32) / dens[hh]
            o_ref[rl, pl.ds(qstart, BAND), _head_cols(hh)] = o
        o_ref[rl, pl.ds(qstart, BAND), GROUP_WIDTH:] = lse_tile
        return carry

    lax.fori_loop(0, rb * nblk, body, 0)


def _band_attn(q, k, v, g, *, blocks_per_step=4):
    batch, dil, mr, _ = q.shape
    assert dil == DILS[g] and mr % BAND == 0
    tq = min(mr, blocks_per_step * BAND)
    nblk = tq // BAND
    rb = min(dil, blocks_per_step // nblk)
    kw = min(mr, 2 * BAND)
    return pl.pallas_call(
        functools.partial(_band_attn_kernel, nblk=nblk, kw=kw, dil=dil),
        grid=(batch, dil // rb, mr // tq),
        in_specs=[
            pl.BlockSpec((None, rb, tq, GROUP_WIDTH), lambda b, r, i: (b, r, i, 0)),
            pl.BlockSpec((None, rb, mr, GROUP_WIDTH), lambda b, r, i: (b, r, 0, 0)),
            pl.BlockSpec((None, rb, mr, GROUP_WIDTH), lambda b, r, i: (b, r, 0, 0)),
        ],
        out_specs=pl.BlockSpec((None, rb, tq, ATTN_OUT_COLS), lambda b, r, i: (b, r, i, 0)),
        out_shape=jax.ShapeDtypeStruct((batch, dil, mr, ATTN_OUT_COLS), F32),
        compiler_params=_params("parallel", "parallel", "arbitrary"),
        name="band_attn_g%d" % g,
    )(q, k, v)


def _gather_attn_kernel(q_ref, kc_ref, vc_ref, kn_ref, vn_ref, o_ref, oh_ref, lh_ref, *, dil, t_new):
    length = kc_ref.shape[1]
    head = lax.broadcasted_iota(jnp.int32, (HEADS, 1), 0)
    slope = lax.bitcast_convert_type((126 - head) << 23, F32) * LOG2_E
    for i in range(t_new):
        qi = q_ref[0, 0, i]
        n_new = i // dil + 1
        n_old = BAND + 1 - n_new
        first = length + i - BAND * dil
        assert first >= 0 and first + (n_old - 1) * dil < length
        kc = kc_ref[0, pl.ds(first, n_old, stride=dil)]
        vc = vc_ref[0, pl.ds(first, n_old, stride=dil)]
        steps = BAND - lax.broadcasted_iota(jnp.int32, (n_old, 1, 1), 0)
        sc = jnp.sum(kc * qi[None], axis=-1, keepdims=True) - slope[None] * (steps * dil).astype(F32)
        sn = [jnp.sum(kn_ref[0, 0, i - jj * dil] * qi, axis=-1, keepdims=True) - slope * float(jj * dil)
              for jj in range(n_new)]
        mx = jnp.max(sc, axis=0)
        for s in sn:
            mx = jnp.maximum(mx, s)
        pc = jnp.exp2(sc - mx[None])
        den = jnp.sum(pc, axis=0)
        acc = jnp.sum(pc * vc, axis=0)
        for jj, s in enumerate(sn):
            pn = jnp.exp2(s - mx)
            den = den + pn
            acc = acc + pn * vn_ref[0, 0, i - jj * dil]
        oh_ref[i] = acc / den
        lh_ref[i] = jnp.broadcast_to((mx + jnp.log2(den)) * LN_2, (HEADS, LSE_LANES))
    lane = lax.broadcasted_iota(jnp.int32, (t_new, LSE_LANES), 1)
    lse_tile = jnp.zeros((t_new, LSE_LANES), F32)
    for hh in range(HEADS):
        o_ref[:, _head_cols(hh)] = oh_ref[:, hh, :]
        lse_tile = jnp.where(lane == hh, lh_ref[:, hh, :], lse_tile)
    o_ref[:, GROUP_WIDTH:] = lse_tile


def _gather_attn(q, kc, vc, kn, vn, g):
    win, dil = ATTN_GROUPS[g]
    b, length = kc.shape[:2]
    t_new = q.shape[2]
    assert length == win and t_new <= dil * BAND
    new = pl.BlockSpec((1, 1, t_new, HEADS, HEAD_DIM), lambda bi: (g, bi, 0, 0, 0))
    cache = pl.BlockSpec((1, length, HEADS, HEAD_DIM), lambda bi: (bi, 0, 0, 0))
    return pl.pallas_call(
        functools.partial(_gather_attn_kernel, dil=dil, t_new=t_new),
        grid=(b,),
        in_specs=[new, cache, cache, new, new],
        out_specs=pl.BlockSpec((None, t_new, ATTN_OUT_COLS), lambda bi: (bi, 0, 0)),
        out_shape=jax.ShapeDtypeStruct((b, t_new, ATTN_OUT_COLS), F32),
        scratch_shapes=[pltpu.VMEM((t_new, HEADS, HEAD_DIM), F32),
                        pltpu.VMEM((t_new, HEADS, LSE_LANES), F32)],
        compiler_params=_params("parallel"),
        name="gather_attn_g%d" % g,
    )(q, kc, vc, kn, vn)


def _merge_kernel(h_ref, y_ref, o0_ref, o1_ref, o2_ref, gate_ref, wp_ref, wa_ref, wo_ref, out_ref,
                  oa_ref, mix_ref, tok_ref, *, dils):
    o_refs = (o0_ref, o1_ref, o2_ref)
    tm, d = out_ref.shape
    s = pl.program_id(0)

    @pl.when(s == 0)
    def _():
        oa_ref[1] = jnp.zeros(oa_ref.shape[1:], BF16)
        mix_ref[1] = jnp.zeros(mix_ref.shape[1:], BF16)

    def tile(gi, c):
        if dils[gi] == 1:
            return o_refs[gi][0, :, c * 128:(c + 1) * 128]
        return tok_ref[gi, c]

    def step(fill, drain):
        for gi, dil in enumerate(dils):
            if dil == 1:
                continue
            n = tm // dil
            for r in range(dil):
                for c in range(ATTN_OUT_TILES):
                    tok_ref[gi, c, pl.ds(r, n, stride=dil), :] = o_refs[gi][r, :, c * 128:(c + 1) * 128]
        lses = [tile(gi, HEADS) for gi in range(N_GROUPS)]
        mx = jnp.maximum(jnp.maximum(lses[0], lses[1]), lses[2])
        es = [jnp.exp(l - mx) for l in lses]
        inv = 1.0 / (es[0] + es[1] + es[2])
        wts = [e * inv for e in es]
        for hh in range(HEADS):
            acc = wts[0][:, hh:hh + 1] * tile(0, hh)
            for gi in range(1, N_GROUPS):
                acc = acc + wts[gi][:, hh:hh + 1] * tile(gi, hh)
            oa_ref[fill, :, _head_cols(hh)] = acc.astype(BF16)

        a_attn = jnp.dot(oa_ref[drain], wa_ref[...], preferred_element_type=F32)
        a_pool = jnp.dot(y_ref[...], wp_ref[...], preferred_element_type=F32)
        mix = gate_ref[:, :d].astype(F32) * a_pool + gate_ref[:, d:].astype(F32) * a_attn
        mix_ref[fill] = mix.astype(BF16)

        out_ref[...] = h_ref[...] + jnp.dot(mix_ref[drain], wo_ref[...], preferred_element_type=F32)

    pl.when(s % 2 == 0)(functools.partial(step, 0, 1))
    pl.when(s % 2 == 1)(functools.partial(step, 1, 0))


def _merge(h, y_pool, o_groups, gates, w_pool_out, w_attn_out, w_out, *, tm=256):
    m, d = h.shape
    dils = tuple(o.shape[1] for o in o_groups)
    seq = o_groups[0].shape[1] * o_groups[0].shape[2]
    tm = _row_tile(seq, tm)
    per_seq, n_tiles = seq // tm, m // tm
    last = n_tiles - 1
    row1 = lambda s: (jnp.clip(s - 1, 0, last), 0)
    row2 = lambda s: (jnp.maximum(s - 2, 0), 0)
    fixed = lambda s: (0, 0)

    def attn_tile(s):
        i = jnp.minimum(s, last)
        return (i // per_seq, 0, i % per_seq, 0)

    o_specs = [pl.BlockSpec((None, dil, tm // dil, ATTN_OUT_COLS), attn_tile) for dil in dils]
    return pl.pallas_call(
        functools.partial(_merge_kernel, dils=dils),
        grid=(n_tiles + 2,),
        in_specs=[
            pl.BlockSpec((tm, d), row2),
            pl.BlockSpec((tm, POOL_WIDTH), row1),
            *o_specs,
            pl.BlockSpec((tm, 2 * d), row1),
            _resident(w_pool_out.shape, fixed),
            _resident(w_attn_out.shape, fixed),
            _resident(w_out.shape, fixed),
        ],
        out_specs=pl.BlockSpec((tm, d), row2),
        out_shape=jax.ShapeDtypeStruct((m, d), F32),
        scratch_shapes=[pltpu.VMEM((2, tm, GROUP_WIDTH), BF16),
                        pltpu.VMEM((2, tm, d), BF16),
                        pltpu.VMEM((N_GROUPS, ATTN_OUT_TILES, tm, 128), F32)],
        compiler_params=_params("arbitrary"),
        name="merge",
    )(h, y_pool, *o_groups, gates, w_pool_out, w_attn_out, w_out)


def _ple_kernel(h_ref, p_ref, g_ref, wg_ref, wp_ref, out_ref):
    h = h_ref[...]
    hn = _rms(h, g_ref[...]).astype(BF16)
    gate = jax.nn.sigmoid(jnp.dot(hn, wg_ref[...], preferred_element_type=F32))
    emb = jnp.dot(p_ref[...].astype(BF16), wp_ref[...], preferred_element_type=F32)
    out_ref[...] = h + gate * emb


def _ple(h, p, g, w_gate, w_proj, *, tm=512):
    m, d = h.shape
    tm = _row_tile(m, tm)
    row = lambda i: (i, 0)
    fixed = lambda i: (0, 0)
    return pl.pallas_call(
        _ple_kernel,
        grid=(m // tm,),
        in_specs=[
            pl.BlockSpec((tm, d), row),
            pl.BlockSpec((tm, p.shape[1]), row),
            _resident((1, d), fixed),
            _resident(w_gate.shape, fixed),
            _resident(w_proj.shape, fixed),
        ],
        out_specs=pl.BlockSpec((tm, d), row),
        out_shape=jax.ShapeDtypeStruct((m, d), F32),
        compiler_params=_params("parallel"),
        name="ple",
    )(h, p, g, w_gate, w_proj)


def _layer(x, p, w, *, prompt, pool_prev=None, caches=None, roll=None):
    b, t, d = x.shape
    m = b * t
    h1, u = _ffn(x.reshape(m, d), w["ffn1_norm"], w["ffn1_w_in"], w["ffn1_w_out"], w["mix_norm"])
    outs_a = _proj_a(u, w["w_zq"], w["w_gate"], w["q_norm"], batch=b, seq=t, prompt=prompt)
    outs_b = _proj_b(u, w["w_kv"], w["k_norm"], batch=b, seq=t, prompt=prompt)
    z, gates = outs_a[0], outs_a[-1]
    z3 = z.reshape(b, t, POOL_WIDTH)
    if prompt:
        qs = outs_a[1:1 + N_GROUPS]
        kbs, vbs = outs_b[0:N_GROUPS], outs_b[N_GROUPS:2 * N_GROUPS]
        kfs, vfs = outs_b[2 * N_GROUPS:3 * N_GROUPS], outs_b[3 * N_GROUPS:]
        y_pool = _pool(z3, z3, w["pool_map"], w["pool_scale"], tp=512, pos0=0, zero_first_halo=True)
        new_pool = z3[:, t - POOL_STATE:]
        o_groups = [_band_attn(qs[g], kbs[g], vbs[g], g) for g in range(N_GROUPS)]
        new_kv = []
        for g in range(N_GROUPS):
            new_kv += [kfs[g].reshape(b, -1, HEADS, HEAD_DIM), vfs[g].reshape(b, -1, HEADS, HEAD_DIM)]
    else:
        qf = outs_a[1].reshape(N_GROUPS, b, t, HEADS, HEAD_DIM)
        kn = outs_b[0].reshape(N_GROUPS, b, t, HEADS, HEAD_DIM)
        vn = outs_b[1].reshape(N_GROUPS, b, t, HEADS, HEAD_DIM)
        halo = jnp.pad(pool_prev, ((0, 0), (POOL_HALO - POOL_STATE, 0), (0, 0)))
        y_pool = _pool(z3, halo, w["pool_map"], w["pool_scale"], tp=t, pos0=PAST_LEN, zero_first_halo=False)
        new_pool = jnp.concatenate([pool_prev, z3], axis=1)[:, t:]
        o_groups = [_gather_attn(qf, caches[2 * g], caches[2 * g + 1], kn, vn, g).reshape(1, 1, m, ATTN_OUT_COLS)
                    for g in range(N_GROUPS)]
    h2 = _merge(h1, y_pool.reshape(m, POOL_WIDTH), o_groups, gates,
                w["w_pool_out"], w["w_attn_out"], w["w_out"])
    if prompt:
        h3, *rolled = _ffn(h2, w["ffn2_norm"], w["ffn2_w_in"], w["ffn2_w_out"], roll=roll, tm=1024)
    else:
        h3 = _ffn(h2, w["ffn2_norm"], w["ffn2_w_in"], w["ffn2_w_out"])
    h4 = _ple(h3, p.reshape(m, -1), w["ple_norm"], w["ple_gate"], w["ple_proj"]).reshape(b, t, d)
    if prompt:
        return h4, new_pool, new_kv, rolled
    return h4, new_pool, (list(caches), kn, vn)


def kernel(x_prompt, x_sample, p_prompt, p_sample, state_pool, cache_k0, cache_v0, cache_k1, cache_v1, cache_k2, cache_v2, ffn1_norm, ffn1_w_in, ffn1_w_out, mix_norm, w_in, q_norm, k_norm, pool_map, pool_scale, w_pool_out, w_attn_out, w_out, ffn2_norm, ffn2_w_in, ffn2_w_out, ple_norm, ple_gate, ple_proj):
    caches = (cache_k0, cache_v0, cache_k1, cache_v1, cache_k2, cache_v2)
    depth = ffn1_norm.shape[0]
    kv0 = POOL_WIDTH + N_GROUPS * GROUP_WIDTH
    gate0 = kv0 + 2 * N_GROUPS * GROUP_WIDTH
    hp, hs = x_prompt, x_sample
    pool_p, pool_s, kv_p, kv_s = [], [], [], []
    for i in range(depth):
        row = lambda a: a[i][None, :]
        w = dict(
            ffn1_norm=row(ffn1_norm), ffn1_w_in=ffn1_w_in[i].astype(BF16), ffn1_w_out=ffn1_w_out[i].astype(BF16),
            mix_norm=row(mix_norm), q_norm=row(q_norm), k_norm=row(k_norm),
            w_zq=w_in[i, :, :kv0].astype(BF16), w_kv=w_in[i, :, kv0:gate0].astype(BF16),
            w_gate=w_in[i, :, gate0:].astype(BF16),
            pool_map=pool_map[i].astype(BF16), pool_scale=row(pool_scale),
            w_pool_out=w_pool_out[i].astype(BF16), w_attn_out=w_attn_out[i].astype(BF16),
            w_out=w_out[i].astype(BF16),
            ffn2_norm=row(ffn2_norm), ffn2_w_in=ffn2_w_in[i].astype(BF16), ffn2_w_out=ffn2_w_out[i].astype(BF16),
            ple_norm=row(ple_norm), ple_gate=ple_gate[i].astype(BF16), ple_proj=ple_proj[i].astype(BF16),
        )
        hs, ps, roll = _layer(hs, p_sample[i], w, prompt=False, pool_prev=state_pool[i],
                              caches=tuple(c[i] for c in caches))
        hp, pp, kvp, kvs = _layer(hp, p_prompt[i], w, prompt=True, roll=roll)
        pool_p.append(pp)
        pool_s.append(ps)
        kv_p.append(kvp)
        kv_s.append(kvs)
    stack = lambda rows: jnp.stack(rows)
    kp = [stack([r[j] for r in kv_p]) for j in range(2 * N_GROUPS)]
    ks = [stack([r[j] for r in kv_s]) for j in range(2 * N_GROUPS)]
    return (hp, hs, stack(pool_p), *kp, stack(pool_s), *ks)
```

```python
import functools

import jax
import jax.numpy as jnp
from jax import lax
from jax.experimental import pallas as pl
from jax.experimental.pallas import tpu as pltpu

F32 = jnp.float32
BF16 = jnp.bfloat16

POOL_WINDOWS = (2, 4, 8, 16)
POOL_GROUP_WIDTH = 256
POOL_WIDTH = POOL_GROUP_WIDTH * len(POOL_WINDOWS)
POOL_STATE = max(POOL_WINDOWS) - 1
POOL_HALO = 16
HEAD_DIM = 128
HEADS = 8
GROUP_WIDTH = HEADS * HEAD_DIM
ATTN_GROUPS = ((128, 1), (512, 4), (2048, 16))
DILS = tuple(d for _, d in ATTN_GROUPS)
N_GROUPS = len(ATTN_GROUPS)
SLAB_OF = tuple(sum(1 for d in DILS[:g] if d > 1) for g in range(N_GROUPS))
N_SLABS = sum(1 for d in DILS if d > 1)
BAND = 128
LSE_LANES = 128
ATTN_OUT_COLS = GROUP_WIDTH + LSE_LANES
ATTN_OUT_TILES = ATTN_OUT_COLS // 128
RMS_EPS = 1e-6
NEG_INF = -1e30
LOG2_E = 1.4426950408889634
LN_2 = 0.6931471805599453
Q_SCALE = HEAD_DIM ** -0.5 * LOG2_E
PAST_LEN = 16384

VMEM_LIMIT = 60 * 1024 * 1024


def _params(*semantics):
    return pltpu.CompilerParams(dimension_semantics=semantics, vmem_limit_bytes=VMEM_LIMIT)


def _resident(shape, index_map):
    return pl.BlockSpec(shape, index_map, pipeline_mode=pl.Buffered(1))


def _rms(x, g):
    return x * lax.rsqrt(jnp.mean(x * x, axis=-1, keepdims=True) + RMS_EPS) * g


def _row_tile(m, want):
    t = min(m, want)
    assert m % t == 0, (m, t)
    return t


def _head_cols(hh):
    return slice(hh * HEAD_DIM, (hh + 1) * HEAD_DIM)


def _ffn_kernel(*refs, emit_norm):
    if emit_norm:
        x_ref, g_ref, wa_ref, wb_ref, wo_ref, gn_ref, o_ref, u_ref = refs
        xn_ref = u_ref
    else:
        x_ref, g_ref, wa_ref, wb_ref, wo_ref, o_ref, xn_ref = refs
    k = pl.program_id(1)

    @pl.when(k == 0)
    def _():
        x = x_ref[...]
        xn_ref[...] = _rms(x, g_ref[...]).astype(BF16)
        o_ref[...] = x

    xn = xn_ref[...]
    a = jnp.dot(xn, wa_ref[...], preferred_element_type=F32)
    b = jnp.dot(xn, wb_ref[...], preferred_element_type=F32)
    hid = (0.5 * (a * jax.nn.sigmoid(a)) * b).astype(BF16)
    o_ref[...] += jnp.dot(hid, wo_ref[...], preferred_element_type=F32)

    if emit_norm:
        @pl.when(k == pl.num_programs(1) - 1)
        def _():
            u_ref[...] = _rms(o_ref[...], gn_ref[...]).astype(BF16)


def _ffn(x, g, w_in, w_out, g_next=None, *, tm=1024, tf=512):
    m, d = x.shape
    d_ff = w_out.shape[0]
    tm = _row_tile(m, tm)
    tf = _row_tile(d_ff, tf)
    nk = d_ff // tf
    emit_norm = g_next is not None
    row = pl.BlockSpec((tm, d), lambda i, k: (i, 0))
    vec = _resident((1, d), lambda i, k: (0, 0))
    in_specs = [row, vec,
                pl.BlockSpec((d, tf), lambda i, k: (0, k)),
                pl.BlockSpec((d, tf), lambda i, k: (0, k + nk)),
                pl.BlockSpec((tf, d), lambda i, k: (k, 0))]
    args = [x, g, w_in, w_in, w_out]
    out_specs, out_shape = row, jax.ShapeDtypeStruct((m, d), F32)
    if emit_norm:
        in_specs.append(vec)
        args.append(g_next)
        out_specs = [row, row]
        out_shape = [out_shape, jax.ShapeDtypeStruct((m, d), BF16)]
    return pl.pallas_call(
        functools.partial(_ffn_kernel, emit_norm=emit_norm),
        grid=(m // tm, nk),
        in_specs=in_specs,
        out_specs=out_specs,
        out_shape=out_shape,
        scratch_shapes=[] if emit_norm else [pltpu.VMEM((tm, d), BF16)],
        compiler_params=_params("parallel", "arbitrary"),
        name="ffn",
    )(*args)


MXU_COLS = 512


def _tile_heads(u_ref, w_ref, t):
    heads = []
    for c in range(t * GROUP_WIDTH, (t + 1) * GROUP_WIDTH, MXU_COLS):
        part = jnp.dot(u_ref[...], w_ref[:, c:c + MXU_COLS], preferred_element_type=F32)
        heads += [part[:, i:i + HEAD_DIM] for i in range(0, MXU_COLS, HEAD_DIM)]
    return heads


def _store_cols(ref, heads, dtype):
    for hh, x in enumerate(heads):
        ref[:, _head_cols(hh)] = x.astype(dtype)


def _store_head_rows(ref, heads):
    for hh, x in enumerate(heads):
        ref[pl.ds(hh, x.shape[0], stride=HEADS), :] = x


def _store_residue_major(ref, slab_ref, heads, dil):
    if dil == 1:
        _store_cols(ref.at[0], heads, BF16)
        return
    n = heads[0].shape[0] // dil
    for hh, x in enumerate(heads):
        slab_ref[hh] = x
    for r in range(dil):
        for hh in range(HEADS):
            ref[r, :, _head_cols(hh)] = slab_ref[hh, pl.ds(r, n, stride=dil), :].astype(BF16)


_HEAVY_FIRST = tuple(sorted(range(N_GROUPS), key=lambda g: -DILS[g]))


def _proj_a_kernel(*refs, prompt):
    if prompt:
        u_ref, wzq_ref, wg_ref, qn_ref, z_ref, q0_ref, q1_ref, q2_ref, gate_ref, slab_ref = refs
        q_refs = (q0_ref, q1_ref, q2_ref)
    else:
        u_ref, wzq_ref, wg_ref, qn_ref, z_ref, q_ref, gate_ref = refs
    for g in _HEAVY_FIRST:
        q = [_rms(x, qn_ref[...]) * Q_SCALE for x in _tile_heads(u_ref, wzq_ref, 1 + g)]
        if prompt:
            _store_residue_major(q_refs[g], slab_ref.at[SLAB_OF[g]], q, DILS[g])
        else:
            _store_head_rows(q_ref.at[g], q)
    for t in range(wg_ref.shape[1] // GROUP_WIDTH):
        gate = [jax.nn.sigmoid(x) for x in _tile_heads(u_ref, wg_ref, t)]
        _store_cols(gate_ref.at[:, t * GROUP_WIDTH:(t + 1) * GROUP_WIDTH], gate, BF16)
    _store_cols(z_ref, _tile_heads(u_ref, wzq_ref, 0), F32)


def _proj_b_kernel(*refs, prompt):
    if prompt:
        (u_ref, w_ref, kn_ref, kb0_ref, kb1_ref, kb2_ref, vb0_ref, vb1_ref, vb2_ref,
         kf0_ref, kf1_ref, kf2_ref, vf0_ref, vf1_ref, vf2_ref, slab_ref) = refs
        kb_refs, vb_refs = (kb0_ref, kb1_ref, kb2_ref), (vb0_ref, vb1_ref, vb2_ref)
        kf_refs, vf_refs = (kf0_ref, kf1_ref, kf2_ref), (vf0_ref, vf1_ref, vf2_ref)
    else:
        u_ref, w_ref, kn_ref, kf_ref, vf_ref = refs
        kf_refs = [kf_ref.at[g] for g in range(N_GROUPS)]
        vf_refs = [vf_ref.at[g] for g in range(N_GROUPS)]
    tm = u_ref.shape[0]
    for g in _HEAVY_FIRST:
        k = [_rms(x, kn_ref[...]) for x in _tile_heads(u_ref, w_ref, g)]
        v = _tile_heads(u_ref, w_ref, N_GROUPS + g)
        kept = kf_refs[g].shape[0] // HEADS
        _store_head_rows(kf_refs[g], [x[tm - kept:] for x in k])
        _store_head_rows(vf_refs[g], [x[tm - kept:] for x in v])
        if prompt:
            _store_residue_major(kb_refs[g], slab_ref.at[0, SLAB_OF[g]], k, DILS[g])
            _store_residue_major(vb_refs[g], slab_ref.at[1, SLAB_OF[g]], v, DILS[g])


def _residue_specs(batch, seq, tm, dtype):
    per_seq = seq // tm
    specs, shapes = [], []
    for dil in DILS:
        specs.append(pl.BlockSpec((None, dil, tm // dil, GROUP_WIDTH),
                                  lambda i: (i // per_seq, 0, i % per_seq, 0)))
        shapes.append(jax.ShapeDtypeStruct((batch, dil, seq // dil, GROUP_WIDTH), dtype))
    return specs, shapes


def _head_rows_spec(n_arrays, m, tm):
    return (pl.BlockSpec((n_arrays, tm * HEADS, HEAD_DIM), lambda i: (0, i, 0)),
            jax.ShapeDtypeStruct((n_arrays, m * HEADS, HEAD_DIM), F32))


def _suffix_head_rows_spec(batch, seq, tm, keep):
    rows = min(keep, tm)
    assert keep % rows == 0
    blocks, per_seq = keep // rows, seq // tm
    spec = pl.BlockSpec((None, None, rows * HEADS, HEAD_DIM),
                        lambda i: (i // per_seq, jnp.maximum(i % per_seq - (per_seq - blocks), 0), 0, 0))
    return spec, jax.ShapeDtypeStruct((batch, blocks, rows * HEADS, HEAD_DIM), F32)


def _proj_a(u, w_zq, w_gate, qn, *, batch, seq, prompt, tm=256):
    m, d = u.shape
    tm = _row_tile(seq if prompt else m, tm)
    fixed = lambda i: (0, 0)
    rows = lambda width: pl.BlockSpec((tm, width), lambda i: (i, 0))
    scratch = []
    if prompt:
        q_specs, q_shapes = _residue_specs(batch, seq, tm, BF16)
        scratch = [pltpu.VMEM((N_SLABS, HEADS, tm, HEAD_DIM), F32)]
    else:
        spec, shape = _head_rows_spec(N_GROUPS, m, tm)
        q_specs, q_shapes = [spec], [shape]
    return pl.pallas_call(
        functools.partial(_proj_a_kernel, prompt=prompt),
        grid=(m // tm,),
        in_specs=[rows(d), _resident(w_zq.shape, fixed), _resident(w_gate.shape, fixed),
                  _resident((1, HEAD_DIM), fixed)],
        out_specs=[rows(GROUP_WIDTH), *q_specs, rows(w_gate.shape[1])],
        out_shape=[jax.ShapeDtypeStruct((m, GROUP_WIDTH), F32), *q_shapes,
                   jax.ShapeDtypeStruct((m, w_gate.shape[1]), BF16)],
        scratch_shapes=scratch,
        compiler_params=_params("parallel"),
        name="proj_a",
    )(u, w_zq, w_gate, qn)


def _proj_b(u, w_kv, kn, *, batch, seq, prompt, tm=256):
    m, d = u.shape
    tm = _row_tile(seq if prompt else m, tm)
    fixed = lambda i: (0, 0)
    scratch = []
    if prompt:
        kb_specs, kb_shapes = _residue_specs(batch, seq, tm, BF16)
        kf = [_suffix_head_rows_spec(batch, seq, tm, min(window, seq)) for window, _ in ATTN_GROUPS]
        kf_specs, kf_shapes = [s for s, _ in kf], [s for _, s in kf]
        out_specs = [*kb_specs, *kb_specs, *kf_specs, *kf_specs]
        out_shape = [*kb_shapes, *kb_shapes, *kf_shapes, *kf_shapes]
        scratch = [pltpu.VMEM((2, N_SLABS, HEADS, tm, HEAD_DIM), F32)]
    else:
        spec, shape = _head_rows_spec(N_GROUPS, m, tm)
        out_specs, out_shape = [spec, spec], [shape, shape]
    return pl.pallas_call(
        functools.partial(_proj_b_kernel, prompt=prompt),
        grid=(m // tm,),
        in_specs=[pl.BlockSpec((tm, d), lambda i: (i, 0)), _resident(w_kv.shape, fixed),
                  _resident((1, HEAD_DIM), fixed)],
        out_specs=out_specs,
        out_shape=out_shape,
        scratch_shapes=scratch,
        compiler_params=_params("arbitrary"),
        name="proj_b",
    )(u, w_kv, kn)


def _pool_kernel(z_ref, halo_ref, pm_ref, ps_ref, y_ref, ext_ref, *, tp, pos0, zero_first_halo):
    i = pl.program_id(1)
    halo = halo_ref[...]
    if zero_first_halo:
        halo = jnp.where(i == 0, 0.0, halo)
    ext_ref[0:POOL_HALO, :] = halo
    ext_ref[POOL_HALO:, :] = z_ref[...]
    pos = pos0 + i * tp + lax.broadcasted_iota(jnp.int32, (tp, 1), 0)
    for gi, w in enumerate(POOL_WINDOWS):
        cols = slice(gi * POOL_GROUP_WIDTH, (gi + 1) * POOL_GROUP_WIDTH)
        zc = ext_ref[POOL_HALO:POOL_HALO + tp, cols]
        wsum = zc
        for back in range(1, w):
            wsum = wsum + ext_ref[POOL_HALO - back:POOL_HALO - back + tp, cols]
        cnt = jnp.minimum(w, pos + 1).astype(F32)
        pooled = (wsum / cnt - zc).astype(BF16)
        y = jnp.dot(pooled, pm_ref[gi], preferred_element_type=F32) * ps_ref[:, cols]
        y_ref[:, cols] = y.astype(BF16)


def _pool(z, halo, pool_map, pool_scale, *, tp, pos0, zero_first_halo):
    b, t, p = z.shape
    tp = _row_tile(t, tp)
    hb = tp // POOL_HALO
    if halo is z:
        halo_map = lambda bi, i: (bi, jnp.maximum(i * hb - 1, 0), 0)
    else:
        assert t == tp and halo.shape == (b, POOL_HALO, p)
        halo_map = lambda bi, i: (bi, 0, 0)
    return pl.pallas_call(
        functools.partial(_pool_kernel, tp=tp, pos0=pos0, zero_first_halo=zero_first_halo),
        grid=(b, t // tp),
        in_specs=[
            pl.BlockSpec((None, tp, p), lambda bi, i: (bi, i, 0)),
            pl.BlockSpec((None, POOL_HALO, p), halo_map),
            _resident(pool_map.shape, lambda bi, i: (0, 0, 0)),
            _resident((1, p), lambda bi, i: (0, 0)),
        ],
        out_specs=pl.BlockSpec((None, tp, p), lambda bi, i: (bi, i, 0)),
        out_shape=jax.ShapeDtypeStruct((b, t, p), BF16),
        scratch_shapes=[pltpu.VMEM((POOL_HALO + tp, p), F32)],
        compiler_params=_params("parallel", "arbitrary"),
        name="pool",
    )(z, halo, pool_map, pool_scale)


def _band_attn_kernel(q_ref, k_ref, v_ref, o_ref, *, nblk, kw, dil):
    rb = q_ref.shape[0]
    i = pl.program_id(2)

    def body(it, carry):
        rl, nl = it // nblk, it % nblk
        n = i * nblk + nl
        if kw == BAND:
            kstart, off = 0, 0
        else:
            kstart = pl.multiple_of(jnp.maximum(n - 1, 0) * BAND, BAND)
            off = n * BAND - kstart
        qstart = pl.multiple_of(nl * BAND, BAND)
        steps = (off + lax.broadcasted_iota(jnp.int32, (BAND, kw), 0)
                 - lax.broadcasted_iota(jnp.int32, (BAND, kw), 1))
        valid = jnp.abs(steps - BAND // 2) <= BAND // 2
        neg_dist = jnp.where(valid, (steps * dil).astype(F32) * -LOG2_E, NEG_INF)
        lane = lax.broadcasted_iota(jnp.int32, (BAND, LSE_LANES), 1)
        lse_tile = jnp.zeros((BAND, LSE_LANES), F32)
        scores = []
        for hh in range(HEADS):
            qh = q_ref[rl, pl.ds(qstart, BAND), _head_cols(hh)]
            kh = k_ref[rl, pl.ds(kstart, kw), _head_cols(hh)]
            scores.append(lax.dot_general(qh, kh, (((1,), (1,)), ((), ())), preferred_element_type=F32))
        probs, dens = [], []
        for hh in range(HEADS):
            logits = scores[hh] + (2.0 ** -(hh + 1)) * neg_dist
            mx = jnp.max(logits, axis=-1, keepdims=True)
            p = jnp.exp2(logits - mx)
            den = jnp.sum(p, axis=-1, keepdims=True)
            probs.append(p.astype(BF16))
            dens.append(den)
            lse_tile = jnp.where(lane == hh, (mx + jnp.log2(den)) * LN_2, lse_tile)
        for hh in range(HEADS):
            vh = v_ref[rl, pl.ds(kstart, kw), _head_cols(hh)]
            o = jnp.dot(probs[hh], vh, preferred_element_type=F32) / dens[hh]
            o_ref[rl, pl.ds(qstart, BAND), _head_cols(hh)] = o
        o_ref[rl, pl.ds(qstart, BAND), GROUP_WIDTH:] = lse_tile
        return carry

    lax.fori_loop(0, rb * nblk, body, 0)


def _band_attn(q, k, v, g, *, blocks_per_step=4):
    batch, dil, mr, _ = q.shape
    assert dil == DILS[g] and mr % BAND == 0
    tq = min(mr, blocks_per_step * BAND)
    nblk = tq // BAND
    rb = min(dil, blocks_per_step // nblk)
    kw = min(mr, 2 * BAND)
    return pl.pallas_call(
        functools.partial(_band_attn_kernel, nblk=nblk, kw=kw, dil=dil),
        grid=(batch, dil // rb, mr // tq),
        in_specs=[
            pl.BlockSpec((None, rb, tq, GROUP_WIDTH), lambda b, r, i: (b, r, i, 0)),
            pl.BlockSpec((None, rb, mr, GROUP_WIDTH), lambda b, r, i: (b, r, 0, 0)),
            pl.BlockSpec((None, rb, mr, GROUP_WIDTH), lambda b, r, i: (b, r, 0, 0)),
        ],
        out_specs=pl.BlockSpec((None, rb, tq, ATTN_OUT_COLS), lambda b, r, i: (b, r, i, 0)),
        out_shape=jax.ShapeDtypeStruct((batch, dil, mr, ATTN_OUT_COLS), F32),
        compiler_params=_params("parallel", "parallel", "arbitrary"),
        name="band_attn_g%d" % g,
    )(q, k, v)


def _roll_copies(b, cache_ref, new_ref, out_hbm, sems, which):
    length, t_new = cache_ref.shape[1], new_ref.shape[2]
    kept = length - t_new
    return (pltpu.make_async_copy(cache_ref.at[0, pl.ds(t_new, kept)],
                                  out_hbm.at[b, pl.ds(0, kept)], sems.at[which, 0]),
            pltpu.make_async_copy(new_ref.at[0, 0], out_hbm.at[b, pl.ds(kept, t_new)], sems.at[which, 1]))


def _gather_attn_kernel(q_ref, kc_ref, vc_ref, kn_ref, vn_ref, o_ref, ko_hbm, vo_hbm, oh_ref, lh_ref, sems,
                        *, dil, t_new):
    b = pl.program_id(0)
    copies = (*_roll_copies(b, kc_ref, kn_ref, ko_hbm, sems, 0),
              *_roll_copies(b, vc_ref, vn_ref, vo_hbm, sems, 1))
    for copy in copies:
        copy.start()
    length = kc_ref.shape[1]
    head = lax.broadcasted_iota(jnp.int32, (HEADS, 1), 0)
    slope = lax.bitcast_convert_type((126 - head) << 23, F32) * LOG2_E
    for i in range(t_new):
        qi = q_ref[0, 0, i]
        n_new = i // dil + 1
        n_old = BAND + 1 - n_new
        first = length + i - BAND * dil
        assert first >= 0 and first + (n_old - 1) * dil < length
        kc = kc_ref[0, pl.ds(first, n_old, stride=dil)]
        vc = vc_ref[0, pl.ds(first, n_old, stride=dil)]
        steps = BAND - lax.broadcasted_iota(jnp.int32, (n_old, 1, 1), 0)
        sc = jnp.sum(kc * qi[None], axis=-1, keepdims=True) - slope[None] * (steps * dil).astype(F32)
        sn = [jnp.sum(kn_ref[0, 0, i - jj * dil] * qi, axis=-1, keepdims=True) - slope * float(jj * dil)
              for jj in range(n_new)]
        mx = jnp.max(sc, axis=0)
        for s in sn:
            mx = jnp.maximum(mx, s)
        pc = jnp.exp2(sc - mx[None])
        den = jnp.sum(pc, axis=0)
        acc = jnp.sum(pc * vc, axis=0)
        for jj, s in enumerate(sn):
            pn = jnp.exp2(s - mx)
            den = den + pn
            acc = acc + pn * vn_ref[0, 0, i - jj * dil]
        oh_ref[i] = acc / den
        lh_ref[i] = jnp.broadcast_to((mx + jnp.log2(den)) * LN_2, (HEADS, LSE_LANES))
    lane = lax.broadcasted_iota(jnp.int32, (t_new, LSE_LANES), 1)
    lse_tile = jnp.zeros((t_new, LSE_LANES), F32)
    for hh in range(HEADS):
        o_ref[:, _head_cols(hh)] = oh_ref[:, hh, :]
        lse_tile = jnp.where(lane == hh, lh_ref[:, hh, :], lse_tile)
    o_ref[:, GROUP_WIDTH:] = lse_tile
    for copy in copies:
        copy.wait()


def _gather_attn(q, kc, vc, kn, vn, g):
    win, dil = ATTN_GROUPS[g]
    b, length = kc.shape[:2]
    t_new = q.shape[2]
    assert length == win and t_new <= dil * BAND
    new = pl.BlockSpec((1, 1, t_new, HEADS, HEAD_DIM), lambda bi: (g, bi, 0, 0, 0))
    cache = pl.BlockSpec((1, length, HEADS, HEAD_DIM), lambda bi: (bi, 0, 0, 0))
    whole_hbm = pl.BlockSpec(memory_space=pl.ANY)
    return pl.pallas_call(
        functools.partial(_gather_attn_kernel, dil=dil, t_new=t_new),
        grid=(b,),
        in_specs=[new, cache, cache, new, new],
        out_specs=[pl.BlockSpec((None, t_new, ATTN_OUT_COLS), lambda bi: (bi, 0, 0)), whole_hbm, whole_hbm],
        out_shape=[jax.ShapeDtypeStruct((b, t_new, ATTN_OUT_COLS), F32),
                   jax.ShapeDtypeStruct(kc.shape, F32), jax.ShapeDtypeStruct(vc.shape, F32)],
        scratch_shapes=[pltpu.VMEM((t_new, HEADS, HEAD_DIM), F32),
                        pltpu.VMEM((t_new, HEADS, LSE_LANES), F32),
                        pltpu.SemaphoreType.DMA((2, 2))],
        compiler_params=_params("arbitrary"),
        name="gather_attn_g%d" % g,
    )(q, kc, vc, kn, vn)


def _merge_kernel(h_ref, y_ref, o0_ref, o1_ref, o2_ref, gate_ref, wp_ref, wa_ref, wo_ref, out_ref,
                  oa_ref, mix_ref, tok_ref, *, dils):
    o_refs = (o0_ref, o1_ref, o2_ref)
    tm, d = out_ref.shape
    s = pl.program_id(0)

    @pl.when(s == 0)
    def _():
        oa_ref[1] = jnp.zeros(oa_ref.shape[1:], BF16)
        mix_ref[1] = jnp.zeros(mix_ref.shape[1:], BF16)

    def tile(gi, c):
        if dils[gi] == 1:
            return o_refs[gi][0, :, c * 128:(c + 1) * 128]
        return tok_ref[gi, c]

    def step(fill, drain):
        for gi, dil in enumerate(dils):
            if dil == 1:
                continue
            n = tm // dil
            for r in range(dil):
                for c in range(ATTN_OUT_TILES):
                    tok_ref[gi, c, pl.ds(r, n, stride=dil), :] = o_refs[gi][r, :, c * 128:(c + 1) * 128]
        lses = [tile(gi, HEADS) for gi in range(N_GROUPS)]
        mx = jnp.maximum(jnp.maximum(lses[0], lses[1]), lses[2])
        es = [jnp.exp(l - mx) for l in lses]
        inv = 1.0 / (es[0] + es[1] + es[2])
        wts = [e * inv for e in es]
        for hh in range(HEADS):
            acc = wts[0][:, hh:hh + 1] * tile(0, hh)
            for gi in range(1, N_GROUPS):
                acc = acc + wts[gi][:, hh:hh + 1] * tile(gi, hh)
            oa_ref[fill, :, _head_cols(hh)] = acc.astype(BF16)

        a_attn = jnp.dot(oa_ref[drain], wa_ref[...], preferred_element_type=F32)
        a_pool = jnp.dot(y_ref[...], wp_ref[...], preferred_element_type=F32)
        mix = gate_ref[:, :d].astype(F32) * a_pool + gate_ref[:, d:].astype(F32) * a_attn
        mix_ref[fill] = mix.astype(BF16)

        out_ref[...] = h_ref[...] + jnp.dot(mix_ref[drain], wo_ref[...], preferred_element_type=F32)

    pl.when(s % 2 == 0)(functools.partial(step, 0, 1))
    pl.when(s % 2 == 1)(functools.partial(step, 1, 0))


def _merge(h, y_pool, o_groups, gates, w_pool_out, w_attn_out, w_out, *, tm=256):
    m, d = h.shape
    dils = tuple(o.shape[1] for o in o_groups)
    seq = o_groups[0].shape[1] * o_groups[0].shape[2]
    tm = _row_tile(seq, tm)
    per_seq, n_tiles = seq // tm, m // tm
    last = n_tiles - 1
    row1 = lambda s: (jnp.clip(s - 1, 0, last), 0)
    row2 = lambda s: (jnp.maximum(s - 2, 0), 0)
    fixed = lambda s: (0, 0)

    def attn_tile(s):
        i = jnp.minimum(s, last)
        return (i // per_seq, 0, i % per_seq, 0)

    o_specs = [pl.BlockSpec((None, dil, tm // dil, ATTN_OUT_COLS), attn_tile) for dil in dils]
    return pl.pallas_call(
        functools.partial(_merge_kernel, dils=dils),
        grid=(n_tiles + 2,),
        in_specs=[
            pl.BlockSpec((tm, d), row2),
            pl.BlockSpec((tm, POOL_WIDTH), row1),
            *o_specs,
            pl.BlockSpec((tm, 2 * d), row1),
            _resident(w_pool_out.shape, fixed),
            _resident(w_attn_out.shape, fixed),
            _resident(w_out.shape, fixed),
        ],
        out_specs=pl.BlockSpec((tm, d), row2),
        out_shape=jax.ShapeDtypeStruct((m, d), F32),
        scratch_shapes=[pltpu.VMEM((2, tm, GROUP_WIDTH), BF16),
                        pltpu.VMEM((2, tm, d), BF16),
                        pltpu.VMEM((N_GROUPS, ATTN_OUT_TILES, tm, 128), F32)],
        compiler_params=_params("arbitrary"),
        name="merge",
    )(h, y_pool, *o_groups, gates, w_pool_out, w_attn_out, w_out)


def _ple_kernel(h_ref, p_ref, g_ref, wg_ref, wp_ref, out_ref):
    h = h_ref[...]
    hn = _rms(h, g_ref[...]).astype(BF16)
    gate = jax.nn.sigmoid(jnp.dot(hn, wg_ref[...], preferred_element_type=F32))
    emb = jnp.dot(p_ref[...].astype(BF16), wp_ref[...], preferred_element_type=F32)
    out_ref[...] = h + gate * emb


def _ple(h, p, g, w_gate, w_proj, *, tm=512):
    m, d = h.shape
    tm = _row_tile(m, tm)
    row = lambda i: (i, 0)
    fixed = lambda i: (0, 0)
    return pl.pallas_call(
        _ple_kernel,
        grid=(m // tm,),
        in_specs=[
            pl.BlockSpec((tm, d), row),
            pl.BlockSpec((tm, p.shape[1]), row),
            _resident((1, d), fixed),
            _resident(w_gate.shape, fixed),
            _resident(w_proj.shape, fixed),
        ],
        out_specs=pl.BlockSpec((tm, d), row),
        out_shape=jax.ShapeDtypeStruct((m, d), F32),
        compiler_params=_params("parallel"),
        name="ple",
    )(h, p, g, w_gate, w_proj)


def _layer(x, p, w, *, prompt, pool_prev=None, caches=None):
    b, t, d = x.shape
    m = b * t
    h1, u = _ffn(x.reshape(m, d), w["ffn1_norm"], w["ffn1_w_in"], w["ffn1_w_out"], w["mix_norm"])
    outs_a = _proj_a(u, w["w_zq"], w["w_gate"], w["q_norm"], batch=b, seq=t, prompt=prompt)
    outs_b = _proj_b(u, w["w_kv"], w["k_norm"], batch=b, seq=t, prompt=prompt)
    z, gates = outs_a[0], outs_a[-1]
    z3 = z.reshape(b, t, POOL_WIDTH)
    if prompt:
        qs = outs_a[1:1 + N_GROUPS]
        kbs, vbs = outs_b[0:N_GROUPS], outs_b[N_GROUPS:2 * N_GROUPS]
        kfs, vfs = outs_b[2 * N_GROUPS:3 * N_GROUPS], outs_b[3 * N_GROUPS:]
        y_pool = _pool(z3, z3, w["pool_map"], w["pool_scale"], tp=512, pos0=0, zero_first_halo=True)
        new_pool = z3[:, t - POOL_STATE:]
        o_groups = [_band_attn(qs[g], kbs[g], vbs[g], g) for g in range(N_GROUPS)]
        new_kv = []
        for g in range(N_GROUPS):
            new_kv += [kfs[g].reshape(b, -1, HEADS, HEAD_DIM), vfs[g].reshape(b, -1, HEADS, HEAD_DIM)]
    else:
        qf = outs_a[1].reshape(N_GROUPS, b, t, HEADS, HEAD_DIM)
        kn = outs_b[0].reshape(N_GROUPS, b, t, HEADS, HEAD_DIM)
        vn = outs_b[1].reshape(N_GROUPS, b, t, HEADS, HEAD_DIM)
        halo = jnp.pad(pool_prev, ((0, 0), (POOL_HALO - POOL_STATE, 0), (0, 0)))
        y_pool = _pool(z3, halo, w["pool_map"], w["pool_scale"], tp=t, pos0=PAST_LEN, zero_first_halo=False)
        new_pool = jnp.concatenate([pool_prev, z3], axis=1)[:, t:]
        o_groups, new_kv = [], []
        for g in range(N_GROUPS):
            kc, vc = caches[2 * g], caches[2 * g + 1]
            o, k_new, v_new = _gather_attn(qf, kc, vc, kn, vn, g)
            o_groups.append(o.reshape(1, 1, m, ATTN_OUT_COLS))
            new_kv += [k_new, v_new]
    h2 = _merge(h1, y_pool.reshape(m, POOL_WIDTH), o_groups, gates,
                w["w_pool_out"], w["w_attn_out"], w["w_out"])
    h3 = _ffn(h2, w["ffn2_norm"], w["ffn2_w_in"], w["ffn2_w_out"])
    h4 = _ple(h3, p.reshape(m, -1), w["ple_norm"], w["ple_gate"], w["ple_proj"])
    return h4.reshape(b, t, d), new_pool, new_kv


def kernel(x_prompt, x_sample, p_prompt, p_sample, state_pool, cache_k0, cache_v0, cache_k1, cache_v1, cache_k2, cache_v2, ffn1_norm, ffn1_w_in, ffn1_w_out, mix_norm, w_in, q_norm, k_norm, pool_map, pool_scale, w_pool_out, w_attn_out, w_out, ffn2_norm, ffn2_w_in, ffn2_w_out, ple_norm, ple_gate, ple_proj):
    caches = (cache_k0, cache_v0, cache_k1, cache_v1, cache_k2, cache_v2)
    depth = ffn1_norm.shape[0]
    kv0 = POOL_WIDTH + N_GROUPS * GROUP_WIDTH
    gate0 = kv0 + 2 * N_GROUPS * GROUP_WIDTH
    hp, hs = x_prompt, x_sample
    pool_p, pool_s, kv_p, kv_s = [], [], [], []
    for i in range(depth):
        row = lambda a: a[i][None, :]
        w = dict(
            ffn1_norm=row(ffn1_norm), ffn1_w_in=ffn1_w_in[i].astype(BF16), ffn1_w_out=ffn1_w_out[i].astype(BF16),
            mix_norm=row(mix_norm), q_norm=row(q_norm), k_norm=row(k_norm),
            w_zq=w_in[i, :, :kv0].astype(BF16), w_kv=w_in[i, :, kv0:gate0].astype(BF16),
            w_gate=w_in[i, :, gate0:].astype(BF16),
            pool_map=pool_map[i].astype(BF16), pool_scale=row(pool_scale),
            w_pool_out=w_pool_out[i].astype(BF16), w_attn_out=w_attn_out[i].astype(BF16),
            w_out=w_out[i].astype(BF16),
            ffn2_norm=row(ffn2_norm), ffn2_w_in=ffn2_w_in[i].astype(BF16), ffn2_w_out=ffn2_w_out[i].astype(BF16),
            ple_norm=row(ple_norm), ple_gate=ple_gate[i].astype(BF16), ple_proj=ple_proj[i].astype(BF16),
        )
        hp, pp, kvp = _layer(hp, p_prompt[i], w, prompt=True)
        hs, ps, kvs = _layer(hs, p_sample[i], w, prompt=False, pool_prev=state_pool[i],
                             caches=tuple(c[i] for c in caches))
        pool_p.append(pp)
        pool_s.append(ps)
        kv_p.append(kvp)
        kv_s.append(kvs)
    stack = lambda rows: jnp.stack(rows)
    kp = [stack([r[j] for r in kv_p]) for j in range(2 * N_GROUPS)]
    ks = [stack([r[j] for r in kv_s]) for j in range(2 * N_GROUPS)]
    return (hp, hs, stack(pool_p), *kp, stack(pool_s), *ks)
```

```python
import functools
import math

import jax
import jax.numpy as jnp
from jax import lax
from jax.experimental import pallas as pl
from jax.experimental.pallas import tpu as pltpu

F32 = jnp.float32
BF16 = jnp.bfloat16

POOL_WINDOWS = (2, 4, 8, 16)
POOL_GROUP_WIDTH = 256
POOL_WIDTH = POOL_GROUP_WIDTH * len(POOL_WINDOWS)
POOL_STATE = max(POOL_WINDOWS) - 1
POOL_HALO = 16
HEAD_DIM = 128
HEADS = 8
GROUP_WIDTH = HEADS * HEAD_DIM
ATTN_GROUPS = ((128, 1), (512, 4), (2048, 16))
DILS = tuple(d for _, d in ATTN_GROUPS)
N_GROUPS = len(ATTN_GROUPS)
REORDER_STRIDE = 4
REORDER_PASSES = tuple(round(math.log(d, REORDER_STRIDE)) for d in DILS)
SLAB_OF = tuple(sum(REORDER_PASSES[:g]) for g in range(N_GROUPS))
N_SLABS = sum(REORDER_PASSES)
BAND = 128
LSE_LANES = 128
ATTN_OUT_COLS = GROUP_WIDTH + LSE_LANES
ATTN_OUT_TILES = ATTN_OUT_COLS // 128
RMS_EPS = 1e-6
NEG_INF = -1e30
LOG2_E = 1.4426950408889634
LN_2 = 0.6931471805599453
Q_SCALE = HEAD_DIM ** -0.5 * LOG2_E
PAST_LEN = 16384

VMEM_LIMIT = 60 * 1024 * 1024


def _params(*semantics):
    return pltpu.CompilerParams(dimension_semantics=semantics, vmem_limit_bytes=VMEM_LIMIT)


def _resident(shape, index_map):
    return pl.BlockSpec(shape, index_map, pipeline_mode=pl.Buffered(1))


def _rms(x, g):
    return x * lax.rsqrt(jnp.mean(x * x, axis=-1, keepdims=True) + RMS_EPS) * g


def _row_tile(m, want):
    t = min(m, want)
    assert m % t == 0, (m, t)
    return t


def _head_cols(hh):
    return slice(hh * HEAD_DIM, (hh + 1) * HEAD_DIM)


def _ffn_kernel(*refs, emit_norm):
    if emit_norm:
        x_ref, g_ref, wa_ref, wb_ref, wo_ref, gn_ref, o_ref, u_ref = refs
        xn_ref = u_ref
    else:
        x_ref, g_ref, wa_ref, wb_ref, wo_ref, o_ref, xn_ref = refs
    k = pl.program_id(1)

    @pl.when(k == 0)
    def _():
        x = x_ref[...]
        xn_ref[...] = _rms(x, g_ref[...]).astype(BF16)
        o_ref[...] = x

    xn = xn_ref[...]
    a = jnp.dot(xn, wa_ref[...], preferred_element_type=F32)
    b = jnp.dot(xn, wb_ref[...], preferred_element_type=F32)
    hid = (0.5 * (a * jax.nn.sigmoid(a)) * b).astype(BF16)
    o_ref[...] += jnp.dot(hid, wo_ref[...], preferred_element_type=F32)

    if emit_norm:
        @pl.when(k == pl.num_programs(1) - 1)
        def _():
            u_ref[...] = _rms(o_ref[...], gn_ref[...]).astype(BF16)


def _ffn(x, g, w_in, w_out, g_next=None, *, tm=1024, tf=512):
    m, d = x.shape
    d_ff = w_out.shape[0]
    tm = _row_tile(m, tm)
    tf = _row_tile(d_ff, tf)
    nk = d_ff // tf
    emit_norm = g_next is not None
    row = pl.BlockSpec((tm, d), lambda i, k: (i, 0))
    vec = _resident((1, d), lambda i, k: (0, 0))
    in_specs = [row, vec,
                pl.BlockSpec((d, tf), lambda i, k: (0, k)),
                pl.BlockSpec((d, tf), lambda i, k: (0, k + nk)),
                pl.BlockSpec((tf, d), lambda i, k: (k, 0))]
    args = [x, g, w_in, w_in, w_out]
    out_specs, out_shape = row, jax.ShapeDtypeStruct((m, d), F32)
    if emit_norm:
        in_specs.append(vec)
        args.append(g_next)
        out_specs = [row, row]
        out_shape = [out_shape, jax.ShapeDtypeStruct((m, d), BF16)]
    return pl.pallas_call(
        functools.partial(_ffn_kernel, emit_norm=emit_norm),
        grid=(m // tm, nk),
        in_specs=in_specs,
        out_specs=out_specs,
        out_shape=out_shape,
        scratch_shapes=[] if emit_norm else [pltpu.VMEM((tm, d), BF16)],
        compiler_params=_params("parallel", "arbitrary"),
        name="ffn",
    )(*args)


MXU_COLS = 512


def _tile_heads(u_ref, w_ref, t):
    heads = []
    for c in range(t * GROUP_WIDTH, (t + 1) * GROUP_WIDTH, MXU_COLS):
        part = jnp.dot(u_ref[...], w_ref[:, c:c + MXU_COLS], preferred_element_type=F32)
        heads += [part[:, i:i + HEAD_DIM] for i in range(0, MXU_COLS, HEAD_DIM)]
    return heads


def _store_cols(ref, heads, dtype):
    for hh, x in enumerate(heads):
        ref[:, _head_cols(hh)] = x.astype(dtype)


def _store_head_rows(ref, heads):
    for hh, x in enumerate(heads):
        ref[pl.ds(hh, x.shape[0], stride=HEADS), :] = x


def _store_residue_major(ref, slab_ref, heads, dil):
    if dil == 1:
        _store_cols(ref.at[0], heads, BF16)
        return
    passes = slab_ref.shape[0]
    assert REORDER_STRIDE ** passes == dil
    for hh, x in enumerate(heads):
        slab_ref[0, hh] = x
    chunk, residues = heads[0].shape[0], [0]
    for p in range(passes):
        sub, weight = chunk // REORDER_STRIDE, REORDER_STRIDE ** p
        for c, res in enumerate(residues):
            for r in range(REORDER_STRIDE):
                for hh in range(HEADS):
                    rows = slab_ref[p, hh, pl.ds(c * chunk + r, sub, stride=REORDER_STRIDE), :]
                    if p + 1 < passes:
                        slab_ref[p + 1, hh, pl.ds(c * chunk + r * sub, sub), :] = rows
                    else:
                        ref[res + r * weight, :, _head_cols(hh)] = rows.astype(BF16)
        residues = [res + r * weight for res in residues for r in range(REORDER_STRIDE)]
        chunk = sub


_HEAVY_FIRST = tuple(sorted(range(N_GROUPS), key=lambda g: -DILS[g]))


def _proj_a_kernel(*refs, prompt):
    if prompt:
        u_ref, wzq_ref, wg_ref, qn_ref, z_ref, q0_ref, q1_ref, q2_ref, gate_ref, slab_ref = refs
        q_refs = (q0_ref, q1_ref, q2_ref)
    else:
        u_ref, wzq_ref, wg_ref, qn_ref, z_ref, q_ref, gate_ref = refs
    for g in _HEAVY_FIRST:
        q = [_rms(x, qn_ref[...]) * Q_SCALE for x in _tile_heads(u_ref, wzq_ref, 1 + g)]
        if prompt:
            _store_residue_major(q_refs[g], slab_ref.at[pl.ds(SLAB_OF[g], REORDER_PASSES[g])], q, DILS[g])
        else:
            _store_head_rows(q_ref.at[g], q)
    for t in range(wg_ref.shape[1] // GROUP_WIDTH):
        gate = [jax.nn.sigmoid(x) for x in _tile_heads(u_ref, wg_ref, t)]
        _store_cols(gate_ref.at[:, t * GROUP_WIDTH:(t + 1) * GROUP_WIDTH], gate, BF16)
    _store_cols(z_ref, _tile_heads(u_ref, wzq_ref, 0), F32)


def _proj_b_kernel(*refs, prompt):
    if prompt:
        (u_ref, w_ref, kn_ref, kb0_ref, kb1_ref, kb2_ref, vb0_ref, vb1_ref, vb2_ref,
         kf0_ref, kf1_ref, kf2_ref, vf0_ref, vf1_ref, vf2_ref, slab_ref) = refs
        kb_refs, vb_refs = (kb0_ref, kb1_ref, kb2_ref), (vb0_ref, vb1_ref, vb2_ref)
        kf_refs, vf_refs = (kf0_ref, kf1_ref, kf2_ref), (vf0_ref, vf1_ref, vf2_ref)
    else:
        u_ref, w_ref, kn_ref, kf_ref, vf_ref = refs
        kf_refs = [kf_ref.at[g] for g in range(N_GROUPS)]
        vf_refs = [vf_ref.at[g] for g in range(N_GROUPS)]
    tm = u_ref.shape[0]
    for g in _HEAVY_FIRST:
        k = [_rms(x, kn_ref[...]) for x in _tile_heads(u_ref, w_ref, g)]
        v = _tile_heads(u_ref, w_ref, N_GROUPS + g)
        kept = kf_refs[g].shape[0] // HEADS
        _store_head_rows(kf_refs[g], [x[tm - kept:] for x in k])
        _store_head_rows(vf_refs[g], [x[tm - kept:] for x in v])
        if prompt:
            slabs = pl.ds(SLAB_OF[g], REORDER_PASSES[g])
            _store_residue_major(kb_refs[g], slab_ref.at[0, slabs], k, DILS[g])
            _store_residue_major(vb_refs[g], slab_ref.at[1, slabs], v, DILS[g])


def _residue_specs(batch, seq, tm, dtype):
    per_seq = seq // tm
    specs, shapes = [], []
    for dil in DILS:
        specs.append(pl.BlockSpec((None, dil, tm // dil, GROUP_WIDTH),
                                  lambda i: (i // per_seq, 0, i % per_seq, 0)))
        shapes.append(jax.ShapeDtypeStruct((batch, dil, seq // dil, GROUP_WIDTH), dtype))
    return specs, shapes


def _head_rows_spec(n_arrays, m, tm):
    return (pl.BlockSpec((n_arrays, tm * HEADS, HEAD_DIM), lambda i: (0, i, 0)),
            jax.ShapeDtypeStruct((n_arrays, m * HEADS, HEAD_DIM), F32))


def _suffix_head_rows_spec(batch, seq, tm, keep):
    rows = min(keep, tm)
    assert keep % rows == 0
    blocks, per_seq = keep // rows, seq // tm
    spec = pl.BlockSpec((None, None, rows * HEADS, HEAD_DIM),
                        lambda i: (i // per_seq, jnp.maximum(i % per_seq - (per_seq - blocks), 0), 0, 0))
    return spec, jax.ShapeDtypeStruct((batch, blocks, rows * HEADS, HEAD_DIM), F32)


def _proj_a(u, w_zq, w_gate, qn, *, batch, seq, prompt, tm=256):
    m, d = u.shape
    tm = _row_tile(seq if prompt else m, tm)
    fixed = lambda i: (0, 0)
    rows = lambda width: pl.BlockSpec((tm, width), lambda i: (i, 0))
    scratch = []
    if prompt:
        q_specs, q_shapes = _residue_specs(batch, seq, tm, BF16)
        scratch = [pltpu.VMEM((N_SLABS, HEADS, tm, HEAD_DIM), F32)]
    else:
        spec, shape = _head_rows_spec(N_GROUPS, m, tm)
        q_specs, q_shapes = [spec], [shape]
    return pl.pallas_call(
        functools.partial(_proj_a_kernel, prompt=prompt),
        grid=(m // tm,),
        in_specs=[rows(d), _resident(w_zq.shape, fixed), _resident(w_gate.shape, fixed),
                  _resident((1, HEAD_DIM), fixed)],
        out_specs=[rows(GROUP_WIDTH), *q_specs, rows(w_gate.shape[1])],
        out_shape=[jax.ShapeDtypeStruct((m, GROUP_WIDTH), F32), *q_shapes,
                   jax.ShapeDtypeStruct((m, w_gate.shape[1]), BF16)],
        scratch_shapes=scratch,
        compiler_params=_params("parallel"),
        name="proj_a",
    )(u, w_zq, w_gate, qn)


def _proj_b(u, w_kv, kn, *, batch, seq, prompt, tm=256):
    m, d = u.shape
    tm = _row_tile(seq if prompt else m, tm)
    fixed = lambda i: (0, 0)
    scratch = []
    if prompt:
        kb_specs, kb_shapes = _residue_specs(batch, seq, tm, BF16)
        kf = [_suffix_head_rows_spec(batch, seq, tm, min(window, seq)) for window, _ in ATTN_GROUPS]
        kf_specs, kf_shapes = [s for s, _ in kf], [s for _, s in kf]
        out_specs = [*kb_specs, *kb_specs, *kf_specs, *kf_specs]
        out_shape = [*kb_shapes, *kb_shapes, *kf_shapes, *kf_shapes]
        scratch = [pltpu.VMEM((2, N_SLABS, HEADS, tm, HEAD_DIM), F32)]
    else:
        spec, shape = _head_rows_spec(N_GROUPS, m, tm)
        out_specs, out_shape = [spec, spec], [shape, shape]
    return pl.pallas_call(
        functools.partial(_proj_b_kernel, prompt=prompt),
        grid=(m // tm,),
        in_specs=[pl.BlockSpec((tm, d), lambda i: (i, 0)), _resident(w_kv.shape, fixed),
                  _resident((1, HEAD_DIM), fixed)],
        out_specs=out_specs,
        out_shape=out_shape,
        scratch_shapes=scratch,
        compiler_params=_params("arbitrary"),
        name="proj_b",
    )(u, w_kv, kn)


def _pool_kernel(z_ref, halo_ref, pm_ref, ps_ref, y_ref, ext_ref, *, tp, pos0, zero_first_halo):
    i = pl.program_id(1)
    halo = halo_ref[...]
    if zero_first_halo:
        halo = jnp.where(i == 0, 0.0, halo)
    ext_ref[0:POOL_HALO, :] = halo
    ext_ref[POOL_HALO:, :] = z_ref[...]
    pos = pos0 + i * tp + lax.broadcasted_iota(jnp.int32, (tp, 1), 0)
    for gi, w in enumerate(POOL_WINDOWS):
        cols = slice(gi * POOL_GROUP_WIDTH, (gi + 1) * POOL_GROUP_WIDTH)
        zc = ext_ref[POOL_HALO:POOL_HALO + tp, cols]
        wsum = zc
        for back in range(1, w):
            wsum = wsum + ext_ref[POOL_HALO - back:POOL_HALO - back + tp, cols]
        cnt = jnp.minimum(w, pos + 1).astype(F32)
        pooled = (wsum / cnt - zc).astype(BF16)
        y = jnp.dot(pooled, pm_ref[gi], preferred_element_type=F32) * ps_ref[:, cols]
        y_ref[:, cols] = y.astype(BF16)


def _pool(z, halo, pool_map, pool_scale, *, tp, pos0, zero_first_halo):
    b, t, p = z.shape
    tp = _row_tile(t, tp)
    hb = tp // POOL_HALO
    if halo is z:
        halo_map = lambda bi, i: (bi, jnp.maximum(i * hb - 1, 0), 0)
    else:
        assert t == tp and halo.shape == (b, POOL_HALO, p)
        halo_map = lambda bi, i: (bi, 0, 0)
    return pl.pallas_call(
        functools.partial(_pool_kernel, tp=tp, pos0=pos0, zero_first_halo=zero_first_halo),
        grid=(b, t // tp),
        in_specs=[
            pl.BlockSpec((None, tp, p), lambda bi, i: (bi, i, 0)),
            pl.BlockSpec((None, POOL_HALO, p), halo_map),
            _resident(pool_map.shape, lambda bi, i: (0, 0, 0)),
            _resident((1, p), lambda bi, i: (0, 0)),
        ],
        out_specs=pl.BlockSpec((None, tp, p), lambda bi, i: (bi, i, 0)),
        out_shape=jax.ShapeDtypeStruct((b, t, p), BF16),
        scratch_shapes=[pltpu.VMEM((POOL_HALO + tp, p), F32)],
        compiler_params=_params("parallel", "arbitrary"),
        name="pool",
    )(z, halo, pool_map, pool_scale)


def _band_attn_kernel(q_ref, k_ref, v_ref, o_ref, *, nblk, kw, dil):
    rb = q_ref.shape[0]
    i = pl.program_id(2)

    def body(it, carry):
        rl, nl = it // nblk, it % nblk
        n = i * nblk + nl
        if kw == BAND:
            kstart, off = 0, 0
        else:
            kstart = pl.multiple_of(jnp.maximum(n - 1, 0) * BAND, BAND)
            off = n * BAND - kstart
        qstart = pl.multiple_of(nl * BAND, BAND)
        steps = (off + lax.broadcasted_iota(jnp.int32, (BAND, kw), 0)
                 - lax.broadcasted_iota(jnp.int32, (BAND, kw), 1))
        valid = jnp.abs(steps - BAND // 2) <= BAND // 2
        neg_dist = jnp.where(valid, (steps * dil).astype(F32) * -LOG2_E, NEG_INF)
        lane = lax.broadcasted_iota(jnp.int32, (BAND, LSE_LANES), 1)
        lse_tile = jnp.zeros((BAND, LSE_LANES), F32)
        scores = []
        for hh in range(HEADS):
            qh = q_ref[rl, pl.ds(qstart, BAND), _head_cols(hh)]
            kh = k_ref[rl, pl.ds(kstart, kw), _head_cols(hh)]
            scores.append(lax.dot_general(qh, kh, (((1,), (1,)), ((), ())), preferred_element_type=F32))
        probs, dens = [], []
        for hh in range(HEADS):
            logits = scores[hh] + (2.0 ** -(hh + 1)) * neg_dist
            mx = jnp.max(logits, axis=-1, keepdims=True)
            p = jnp.exp2(logits - mx)
            den = jnp.sum(p, axis=-1, keepdims=True)
            probs.append(p.astype(BF16))
            dens.append(den)
            lse_tile = jnp.where(lane == hh, (mx + jnp.log2(den)) * LN_2, lse_tile)
        for hh in range(HEADS):
            vh = v_ref[rl, pl.ds(kstart, kw), _head_cols(hh)]
            o = jnp.dot(probs[hh], vh, preferred_element_type=F32) / dens[hh]
            o_ref[rl, pl.ds(qstart, BAND), _head_cols(hh)] = o
        o_ref[rl, pl.ds(qstart, BAND), GROUP_WIDTH:] = lse_tile
        return carry

    lax.fori_loop(0, rb * nblk, body, 0, unroll=True)


def _band_attn(q, k, v, g, *, blocks_per_step=4):
    batch, dil, mr, _ = q.shape
    assert dil == DILS[g] and mr % BAND == 0
    tq = min(mr, blocks_per_step * BAND)
    nblk = tq // BAND
    rb = min(dil, blocks_per_step // nblk)
    kw = min(mr, 2 * BAND)
    return pl.pallas_call(
        functools.partial(_band_attn_kernel, nblk=nblk, kw=kw, dil=dil),
        grid=(batch, dil // rb, mr // tq),
        in_specs=[
            pl.BlockSpec((None, rb, tq, GROUP_WIDTH), lambda b, r, i: (b, r, i, 0)),
            pl.BlockSpec((None, rb, mr, GROUP_WIDTH), lambda b, r, i: (b, r, 0, 0)),
            pl.BlockSpec((None, rb, mr, GROUP_WIDTH), lambda b, r, i: (b, r, 0, 0)),
        ],
        out_specs=pl.BlockSpec((None, rb, tq, ATTN_OUT_COLS), lambda b, r, i: (b, r, i, 0)),
        out_shape=jax.ShapeDtypeStruct((batch, dil, mr, ATTN_OUT_COLS), F32),
        compiler_params=_params("parallel", "parallel", "arbitrary"),
        name="band_attn_g%d" % g,
    )(q, k, v)


def _roll_copies(b, cache_ref, new_ref, out_hbm, sems, which):
    length, t_new = cache_ref.shape[1], new_ref.shape[2]
    kept = length - t_new
    return (pltpu.make_async_copy(cache_ref.at[0, pl.ds(t_new, kept)],
                                  out_hbm.at[b, pl.ds(0, kept)], sems.at[which, 0]),
            pltpu.make_async_copy(new_ref.at[0, 0], out_hbm.at[b, pl.ds(kept, t_new)], sems.at[which, 1]))


def _gather_attn_kernel(q_ref, kc_ref, vc_ref, kn_ref, vn_ref, o_ref, ko_hbm, vo_hbm, oh_ref, lh_ref, sems,
                        *, dil, t_new):
    b = pl.program_id(0)
    copies = (*_roll_copies(b, kc_ref, kn_ref, ko_hbm, sems, 0),
              *_roll_copies(b, vc_ref, vn_ref, vo_hbm, sems, 1))
    for copy in copies:
        copy.start()
    length = kc_ref.shape[1]
    head = lax.broadcasted_iota(jnp.int32, (HEADS, 1), 0)
    slope = lax.bitcast_convert_type((126 - head) << 23, F32) * LOG2_E
    for i in range(t_new):
        qi = q_ref[0, 0, i]
        n_new = i // dil + 1
        n_old = BAND + 1 - n_new
        first = length + i - BAND * dil
        assert first >= 0 and first + (n_old - 1) * dil < length
        kc = kc_ref[0, pl.ds(first, n_old, stride=dil)]
        vc = vc_ref[0, pl.ds(first, n_old, stride=dil)]
        steps = BAND - lax.broadcasted_iota(jnp.int32, (n_old, 1, 1), 0)
        sc = jnp.sum(kc * qi[None], axis=-1, keepdims=True) - slope[None] * (steps * dil).astype(F32)
        sn = [jnp.sum(kn_ref[0, 0, i - jj * dil] * qi, axis=-1, keepdims=True) - slope * float(jj * dil)
              for jj in range(n_new)]
        mx = jnp.max(sc, axis=0)
        for s in sn:
            mx = jnp.maximum(mx, s)
        pc = jnp.exp2(sc - mx[None])
        den = jnp.sum(pc, axis=0)
        acc = jnp.sum(pc * vc, axis=0)
        for jj, s in enumerate(sn):
            pn = jnp.exp2(s - mx)
            den = den + pn
            acc = acc + pn * vn_ref[0, 0, i - jj * dil]
        oh_ref[i] = acc / den
        lh_ref[i] = jnp.broadcast_to((mx + jnp.log2(den)) * LN_2, (HEADS, LSE_LANES))
    lane = lax.broadcasted_iota(jnp.int32, (t_new, LSE_LANES), 1)
    lse_tile = jnp.zeros((t_new, LSE_LANES), F32)
    for hh in range(HEADS):
        o_ref[:, _head_cols(hh)] = oh_ref[:, hh, :]
        lse_tile = jnp.where(lane == hh, lh_ref[:, hh, :], lse_tile)
    o_ref[:, GROUP_WIDTH:] = lse_tile
    for copy in copies:
        copy.wait()


def _gather_attn(q, kc, vc, kn, vn, g):
    win, dil = ATTN_GROUPS[g]
    b, length = kc.shape[:2]
    t_new = q.shape[2]
    assert length == win and t_new <= dil * BAND
    new = pl.BlockSpec((1, 1, t_new, HEADS, HEAD_DIM), lambda bi: (g, bi, 0, 0, 0))
    cache = pl.BlockSpec((1, length, HEADS, HEAD_DIM), lambda bi: (bi, 0, 0, 0))
    whole_hbm = pl.BlockSpec(memory_space=pl.ANY)
    return pl.pallas_call(
        functools.partial(_gather_attn_kernel, dil=dil, t_new=t_new),
        grid=(b,),
        in_specs=[new, cache, cache, new, new],
        out_specs=[pl.BlockSpec((None, t_new, ATTN_OUT_COLS), lambda bi: (bi, 0, 0)), whole_hbm, whole_hbm],
        out_shape=[jax.ShapeDtypeStruct((b, t_new, ATTN_OUT_COLS), F32),
                   jax.ShapeDtypeStruct(kc.shape, F32), jax.ShapeDtypeStruct(vc.shape, F32)],
        scratch_shapes=[pltpu.VMEM((t_new, HEADS, HEAD_DIM), F32),
                        pltpu.VMEM((t_new, HEADS, LSE_LANES), F32),
                        pltpu.SemaphoreType.DMA((2, 2))],
        compiler_params=_params("arbitrary"),
        name="gather_attn_g%d" % g,
    )(q, kc, vc, kn, vn)


def _merge_kernel(h_ref, y_ref, o0_ref, o1_ref, o2_ref, gate_ref, wp_ref, wa_ref, wo_ref, out_ref,
                  oa_ref, mix_ref, tok_ref, *, dils):
    o_refs = (o0_ref, o1_ref, o2_ref)
    tm, d = out_ref.shape
    s = pl.program_id(0)

    @pl.when(s == 0)
    def _():
        oa_ref[1] = jnp.zeros(oa_ref.shape[1:], BF16)
        mix_ref[1] = jnp.zeros(mix_ref.shape[1:], BF16)

    def tile(gi, c):
        if dils[gi] == 1:
            return o_refs[gi][0, :, c * 128:(c + 1) * 128]
        return tok_ref[gi, c]

    def step(fill, drain):
        for gi, dil in enumerate(dils):
            if dil == 1:
                continue
            n = tm // dil
            for r in range(dil):
                for c in range(ATTN_OUT_TILES):
                    tok_ref[gi, c, pl.ds(r, n, stride=dil), :] = o_refs[gi][r, :, c * 128:(c + 1) * 128]
        lses = [tile(gi, HEADS) for gi in range(N_GROUPS)]
        mx = jnp.maximum(jnp.maximum(lses[0], lses[1]), lses[2])
        es = [jnp.exp(l - mx) for l in lses]
        inv = 1.0 / (es[0] + es[1] + es[2])
        wts = [e * inv for e in es]
        for hh in range(HEADS):
            acc = wts[0][:, hh:hh + 1] * tile(0, hh)
            for gi in range(1, N_GROUPS):
                acc = acc + wts[gi][:, hh:hh + 1] * tile(gi, hh)
            oa_ref[fill, :, _head_cols(hh)] = acc.astype(BF16)

        a_attn = jnp.dot(oa_ref[drain], wa_ref[...], preferred_element_type=F32)
        a_pool = jnp.dot(y_ref[...], wp_ref[...], preferred_element_type=F32)
        mix = gate_ref[:, :d].astype(F32) * a_pool + gate_ref[:, d:].astype(F32) * a_attn
        mix_ref[fill] = mix.astype(BF16)

        out_ref[...] = h_ref[...] + jnp.dot(mix_ref[drain], wo_ref[...], preferred_element_type=F32)

    pl.when(s % 2 == 0)(functools.partial(step, 0, 1))
    pl.when(s % 2 == 1)(functools.partial(step, 1, 0))


def _merge(h, y_pool, o_groups, gates, w_pool_out, w_attn_out, w_out, *, tm=256):
    m, d = h.shape
    dils = tuple(o.shape[1] for o in o_groups)
    seq = o_groups[0].shape[1] * o_groups[0].shape[2]
    tm = _row_tile(seq, tm)
    per_seq, n_tiles = seq // tm, m // tm
    last = n_tiles - 1
    row1 = lambda s: (jnp.clip(s - 1, 0, last), 0)
    row2 = lambda s: (jnp.maximum(s - 2, 0), 0)
    fixed = lambda s: (0, 0)

    def attn_tile(s):
        i = jnp.minimum(s, last)
        return (i // per_seq, 0, i % per_seq, 0)

    o_specs = [pl.BlockSpec((None, dil, tm // dil, ATTN_OUT_COLS), attn_tile) for dil in dils]
    return pl.pallas_call(
        functools.partial(_merge_kernel, dils=dils),
        grid=(n_tiles + 2,),
        in_specs=[
            pl.BlockSpec((tm, d), row2),
            pl.BlockSpec((tm, POOL_WIDTH), row1),
            *o_specs,
            pl.BlockSpec((tm, 2 * d), row1),
            _resident(w_pool_out.shape, fixed),
            _resident(w_attn_out.shape, fixed),
            _resident(w_out.shape, fixed),
        ],
        out_specs=pl.BlockSpec((tm, d), row2),
        out_shape=jax.ShapeDtypeStruct((m, d), F32),
        scratch_shapes=[pltpu.VMEM((2, tm, GROUP_WIDTH), BF16),
                        pltpu.VMEM((2, tm, d), BF16),
                        pltpu.VMEM((N_GROUPS, ATTN_OUT_TILES, tm, 128), F32)],
        compiler_params=_params("arbitrary"),
        name="merge",
    )(h, y_pool, *o_groups, gates, w_pool_out, w_attn_out, w_out)


def _ple_kernel(h_ref, p_ref, g_ref, wg_ref, wp_ref, out_ref):
    h = h_ref[...]
    hn = _rms(h, g_ref[...]).astype(BF16)
    gate = jax.nn.sigmoid(jnp.dot(hn, wg_ref[...], preferred_element_type=F32))
    emb = jnp.dot(p_ref[...].astype(BF16), wp_ref[...], preferred_element_type=F32)
    out_ref[...] = h + gate * emb


def _ple(h, p, g, w_gate, w_proj, *, tm=512):
    m, d = h.shape
    tm = _row_tile(m, tm)
    row = lambda i: (i, 0)
    fixed = lambda i: (0, 0)
    return pl.pallas_call(
        _ple_kernel,
        grid=(m // tm,),
        in_specs=[
            pl.BlockSpec((tm, d), row),
            pl.BlockSpec((tm, p.shape[1]), row),
            _resident((1, d), fixed),
            _resident(w_gate.shape, fixed),
            _resident(w_proj.shape, fixed),
        ],
        out_specs=pl.BlockSpec((tm, d), row),
        out_shape=jax.ShapeDtypeStruct((m, d), F32),
        compiler_params=_params("parallel"),
        name="ple",
    )(h, p, g, w_gate, w_proj)


def _layer(x, p, w, *, prompt, pool_prev=None, caches=None):
    b, t, d = x.shape
    m = b * t
    h1, u = _ffn(x.reshape(m, d), w["ffn1_norm"], w["ffn1_w_in"], w["ffn1_w_out"], w["mix_norm"])
    outs_a = _proj_a(u, w["w_zq"], w["w_gate"], w["q_norm"], batch=b, seq=t, prompt=prompt)
    outs_b = _proj_b(u, w["w_kv"], w["k_norm"], batch=b, seq=t, prompt=prompt)
    z, gates = outs_a[0], outs_a[-1]
    z3 = z.reshape(b, t, POOL_WIDTH)
    if prompt:
        qs = outs_a[1:1 + N_GROUPS]
        kbs, vbs = outs_b[0:N_GROUPS], outs_b[N_GROUPS:2 * N_GROUPS]
        kfs, vfs = outs_b[2 * N_GROUPS:3 * N_GROUPS], outs_b[3 * N_GROUPS:]
        y_pool = _pool(z3, z3, w["pool_map"], w["pool_scale"], tp=512, pos0=0, zero_first_halo=True)
        new_pool = z3[:, t - POOL_STATE:]
        o_groups = [_band_attn(qs[g], kbs[g], vbs[g], g) for g in range(N_GROUPS)]
        new_kv = []
        for g in range(N_GROUPS):
            new_kv += [kfs[g].reshape(b, -1, HEADS, HEAD_DIM), vfs[g].reshape(b, -1, HEADS, HEAD_DIM)]
    else:
        qf = outs_a[1].reshape(N_GROUPS, b, t, HEADS, HEAD_DIM)
        kn = outs_b[0].reshape(N_GROUPS, b, t, HEADS, HEAD_DIM)
        vn = outs_b[1].reshape(N_GROUPS, b, t, HEADS, HEAD_DIM)
        halo = jnp.pad(pool_prev, ((0, 0), (POOL_HALO - POOL_STATE, 0), (0, 0)))
        y_pool = _pool(z3, halo, w["pool_map"], w["pool_scale"], tp=t, pos0=PAST_LEN, zero_first_halo=False)
        new_pool = jnp.concatenate([pool_prev, z3], axis=1)[:, t:]
        o_groups, new_kv = [], []
        for g in range(N_GROUPS):
            kc, vc = caches[2 * g], caches[2 * g + 1]
            o, k_new, v_new = _gather_attn(qf, kc, vc, kn, vn, g)
            o_groups.append(o.reshape(1, 1, m, ATTN_OUT_COLS))
            new_kv += [k_new, v_new]
    h2 = _merge(h1, y_pool.reshape(m, POOL_WIDTH), o_groups, gates,
                w["w_pool_out"], w["w_attn_out"], w["w_out"])
    h3 = _ffn(h2, w["ffn2_norm"], w["ffn2_w_in"], w["ffn2_w_out"])
    h4 = _ple(h3, p.reshape(m, -1), w["ple_norm"], w["ple_gate"], w["ple_proj"])
    return h4.reshape(b, t, d), new_pool, new_kv


def kernel(x_prompt, x_sample, p_prompt, p_sample, state_pool, cache_k0, cache_v0, cache_k1, cache_v1, cache_k2, cache_v2, ffn1_norm, ffn1_w_in, ffn1_w_out, mix_norm, w_in, q_norm, k_norm, pool_map, pool_scale, w_pool_out, w_attn_out, w_out, ffn2_norm, ffn2_w_in, ffn2_w_out, ple_norm, ple_gate, ple_proj):
    caches = (cache_k0, cache_v0, cache_k1, cache_v1, cache_k2, cache_v2)
    depth = ffn1_norm.shape[0]
    kv0 = POOL_WIDTH + N_GROUPS * GROUP_WIDTH
    gate0 = kv0 + 2 * N_GROUPS * GROUP_WIDTH
    hp, hs = x_prompt, x_sample
    pool_p, pool_s, kv_p, kv_s = [], [], [], []
    for i in range(depth):
        row = lambda a: a[i][None, :]
        w = dict(
            ffn1_norm=row(ffn1_norm), ffn1_w_in=ffn1_w_in[i].astype(BF16), ffn1_w_out=ffn1_w_out[i].astype(BF16),
            mix_norm=row(mix_norm), q_norm=row(q_norm), k_norm=row(k_norm),
            w_zq=w_in[i, :, :kv0].astype(BF16), w_kv=w_in[i, :, kv0:gate0].astype(BF16),
            w_gate=w_in[i, :, gate0:].astype(BF16),
            pool_map=pool_map[i].astype(BF16), pool_scale=row(pool_scale),
            w_pool_out=w_pool_out[i].astype(BF16), w_attn_out=w_attn_out[i].astype(BF16),
            w_out=w_out[i].astype(BF16),
            ffn2_norm=row(ffn2_norm), ffn2_w_in=ffn2_w_in[i].astype(BF16), ffn2_w_out=ffn2_w_out[i].astype(BF16),
            ple_norm=row(ple_norm), ple_gate=ple_gate[i].astype(BF16), ple_proj=ple_proj[i].astype(BF16),
        )
        hp, pp, kvp = _layer(hp, p_prompt[i], w, prompt=True)
        hs, ps, kvs = _layer(hs, p_sample[i], w, prompt=False, pool_prev=state_pool[i],
                             caches=tuple(c[i] for c in caches))
        pool_p.append(pp)
        pool_s.append(ps)
        kv_p.append(kvp)
        kv_s.append(kvs)
    stack = lambda rows: jnp.stack(rows)
    kp = [stack([r[j] for r in kv_p]) for j in range(2 * N_GROUPS)]
    ks = [stack([r[j] for r in kv_s]) for j in range(2 * N_GROUPS)]
    return (hp, hs, stack(pool_p), *kp, stack(pool_s), *ks)
```

```python
import functools
import math

import jax
import jax.numpy as jnp
from jax import lax
from jax.experimental import pallas as pl
from jax.experimental.pallas import tpu as pltpu

F32 = jnp.float32
BF16 = jnp.bfloat16

POOL_WINDOWS = (2, 4, 8, 16)
POOL_GROUP_WIDTH = 256
POOL_WIDTH = POOL_GROUP_WIDTH * len(POOL_WINDOWS)
POOL_STATE = max(POOL_WINDOWS) - 1
POOL_HALO = 16
POOL_PAD = 8
HEAD_DIM = 128
HEADS = 8
GROUP_WIDTH = HEADS * HEAD_DIM
ATTN_GROUPS = ((128, 1), (512, 4), (2048, 16))
DILS = tuple(d for _, d in ATTN_GROUPS)
N_GROUPS = len(ATTN_GROUPS)
REORDER_STRIDE = 4
REORDER_PASSES = tuple(round(math.log(d, REORDER_STRIDE)) for d in DILS)
SLAB_OF = tuple(sum(REORDER_PASSES[:g]) for g in range(N_GROUPS))
N_SLABS = sum(REORDER_PASSES)
BAND = 128
LSE_LANES = 128
ATTN_OUT_COLS = GROUP_WIDTH + LSE_LANES
ATTN_OUT_TILES = ATTN_OUT_COLS // 128
RMS_EPS = 1e-6
NEG_INF = -1e30
LOG2_E = 1.4426950408889634
LN_2 = 0.6931471805599453
Q_SCALE = HEAD_DIM ** -0.5 * LOG2_E
PAST_LEN = 16384

VMEM_LIMIT = 60 * 1024 * 1024


def _params(*semantics):
    return pltpu.CompilerParams(dimension_semantics=semantics, vmem_limit_bytes=VMEM_LIMIT)


def _resident(shape, index_map):
    return pl.BlockSpec(shape, index_map, pipeline_mode=pl.Buffered(1))


def _rms(x, g):
    return x * lax.rsqrt(jnp.mean(x * x, axis=-1, keepdims=True) + RMS_EPS) * g


def _row_tile(m, want):
    t = min(m, want)
    assert m % t == 0, (m, t)
    return t


def _head_cols(hh):
    return slice(hh * HEAD_DIM, (hh + 1) * HEAD_DIM)


def _ffn_kernel(*refs, emit_norm):
    if emit_norm:
        x_ref, g_ref, wa_ref, wb_ref, wo_ref, gn_ref, o_ref, u_ref = refs
        xn_ref = u_ref
    else:
        x_ref, g_ref, wa_ref, wb_ref, wo_ref, o_ref, xn_ref = refs
    k = pl.program_id(1)

    @pl.when(k == 0)
    def _():
        x = x_ref[...]
        xn_ref[...] = _rms(x, g_ref[...]).astype(BF16)
        o_ref[...] = x

    xn = xn_ref[...]
    a = jnp.dot(xn, wa_ref[...], preferred_element_type=F32)
    b = jnp.dot(xn, wb_ref[...], preferred_element_type=F32)
    hid = (0.5 * (a * jax.nn.sigmoid(a)) * b).astype(BF16)
    o_ref[...] += jnp.dot(hid, wo_ref[...], preferred_element_type=F32)

    if emit_norm:
        @pl.when(k == pl.num_programs(1) - 1)
        def _():
            u_ref[...] = _rms(o_ref[...], gn_ref[...]).astype(BF16)


def _ffn(x, g, w_in, w_out, g_next=None, *, tm=1024, tf=512):
    m, d = x.shape
    d_ff = w_out.shape[0]
    tm = _row_tile(m, tm)
    tf = _row_tile(d_ff, tf)
    nk = d_ff // tf
    emit_norm = g_next is not None
    row = pl.BlockSpec((tm, d), lambda i, k: (i, 0))
    vec = _resident((1, d), lambda i, k: (0, 0))
    in_specs = [row, vec,
                pl.BlockSpec((d, tf), lambda i, k: (0, k)),
                pl.BlockSpec((d, tf), lambda i, k: (0, k + nk)),
                pl.BlockSpec((tf, d), lambda i, k: (k, 0))]
    args = [x, g, w_in, w_in, w_out]
    out_specs, out_shape = row, jax.ShapeDtypeStruct((m, d), F32)
    if emit_norm:
        in_specs.append(vec)
        args.append(g_next)
        out_specs = [row, row]
        out_shape = [out_shape, jax.ShapeDtypeStruct((m, d), BF16)]
    return pl.pallas_call(
        functools.partial(_ffn_kernel, emit_norm=emit_norm),
        grid=(m // tm, nk),
        in_specs=in_specs,
        out_specs=out_specs,
        out_shape=out_shape,
        scratch_shapes=[] if emit_norm else [pltpu.VMEM((tm, d), BF16)],
        compiler_params=_params("parallel", "arbitrary"),
        name="ffn",
    )(*args)


MXU_COLS = 512


def _tile_heads(u_ref, w_ref, t):
    heads = []
    for c in range(t * GROUP_WIDTH, (t + 1) * GROUP_WIDTH, MXU_COLS):
        part = jnp.dot(u_ref[...], w_ref[:, c:c + MXU_COLS], preferred_element_type=F32)
        heads += [part[:, i:i + HEAD_DIM] for i in range(0, MXU_COLS, HEAD_DIM)]
    return heads


def _store_cols(ref, heads, dtype):
    for hh, x in enumerate(heads):
        ref[:, _head_cols(hh)] = x.astype(dtype)


def _store_head_rows(ref, heads):
    for hh, x in enumerate(heads):
        ref[pl.ds(hh, x.shape[0], stride=HEADS), :] = x


def _store_residue_major(ref, slab_ref, heads, dil):
    if dil == 1:
        _store_cols(ref.at[0], heads, BF16)
        return
    passes = slab_ref.shape[0]
    assert REORDER_STRIDE ** passes == dil
    for hh, x in enumerate(heads):
        slab_ref[0, hh] = x
    chunk, residues = heads[0].shape[0], [0]
    for p in range(passes):
        sub, weight = chunk // REORDER_STRIDE, REORDER_STRIDE ** p
        for c, res in enumerate(residues):
            for r in range(REORDER_STRIDE):
                for hh in range(HEADS):
                    rows = slab_ref[p, hh, pl.ds(c * chunk + r, sub, stride=REORDER_STRIDE), :]
                    if p + 1 < passes:
                        slab_ref[p + 1, hh, pl.ds(c * chunk + r * sub, sub), :] = rows
                    else:
                        ref[res + r * weight, :, _head_cols(hh)] = rows.astype(BF16)
        residues = [res + r * weight for res in residues for r in range(REORDER_STRIDE)]
        chunk = sub


_HEAVY_FIRST = tuple(sorted(range(N_GROUPS), key=lambda g: -DILS[g]))


def _proj_a_kernel(*refs, prompt):
    if prompt:
        u_ref, wzq_ref, wg_ref, qn_ref, z_ref, q0_ref, q1_ref, q2_ref, gate_ref, slab_ref = refs
        q_refs = (q0_ref, q1_ref, q2_ref)
    else:
        u_ref, wzq_ref, wg_ref, qn_ref, z_ref, q_ref, gate_ref = refs
    for g in _HEAVY_FIRST:
        q = [_rms(x, qn_ref[...]) * Q_SCALE for x in _tile_heads(u_ref, wzq_ref, 1 + g)]
        if prompt:
            _store_residue_major(q_refs[g], slab_ref.at[pl.ds(SLAB_OF[g], REORDER_PASSES[g])], q, DILS[g])
        else:
            _store_head_rows(q_ref.at[g], q)
    for t in range(wg_ref.shape[1] // GROUP_WIDTH):
        gate = [jax.nn.sigmoid(x) for x in _tile_heads(u_ref, wg_ref, t)]
        _store_cols(gate_ref.at[:, t * GROUP_WIDTH:(t + 1) * GROUP_WIDTH], gate, BF16)
    _store_cols(z_ref, _tile_heads(u_ref, wzq_ref, 0), F32)


def _proj_b_kernel(*refs, prompt):
    if prompt:
        (u_ref, w_ref, kn_ref, kb0_ref, kb1_ref, kb2_ref, vb0_ref, vb1_ref, vb2_ref,
         kf0_ref, kf1_ref, kf2_ref, vf0_ref, vf1_ref, vf2_ref, slab_ref) = refs
        kb_refs, vb_refs = (kb0_ref, kb1_ref, kb2_ref), (vb0_ref, vb1_ref, vb2_ref)
        kf_refs, vf_refs = (kf0_ref, kf1_ref, kf2_ref), (vf0_ref, vf1_ref, vf2_ref)
    else:
        u_ref, w_ref, kn_ref, kf_ref, vf_ref = refs
        kf_refs = [kf_ref.at[g] for g in range(N_GROUPS)]
        vf_refs = [vf_ref.at[g] for g in range(N_GROUPS)]
    tm = u_ref.shape[0]
    for g in _HEAVY_FIRST:
        k = [_rms(x, kn_ref[...]) for x in _tile_heads(u_ref, w_ref, g)]
        v = _tile_heads(u_ref, w_ref, N_GROUPS + g)
        kept = kf_refs[g].shape[0] // HEADS
        _store_head_rows(kf_refs[g], [x[tm - kept:] for x in k])
        _store_head_rows(vf_refs[g], [x[tm - kept:] for x in v])
        if prompt:
            slabs = pl.ds(SLAB_OF[g], REORDER_PASSES[g])
            _store_residue_major(kb_refs[g], slab_ref.at[0, slabs], k, DILS[g])
            _store_residue_major(vb_refs[g], slab_ref.at[1, slabs], v, DILS[g])


def _residue_specs(batch, seq, tm, dtype):
    per_seq = seq // tm
    specs, shapes = [], []
    for dil in DILS:
        specs.append(pl.BlockSpec((None, dil, tm // dil, GROUP_WIDTH),
                                  lambda i: (i // per_seq, 0, i % per_seq, 0)))
        shapes.append(jax.ShapeDtypeStruct((batch, dil, seq // dil, GROUP_WIDTH), dtype))
    return specs, shapes


def _head_rows_spec(n_arrays, m, tm):
    return (pl.BlockSpec((n_arrays, tm * HEADS, HEAD_DIM), lambda i: (0, i, 0)),
            jax.ShapeDtypeStruct((n_arrays, m * HEADS, HEAD_DIM), F32))


def _suffix_head_rows_spec(batch, seq, tm, keep):
    rows = min(keep, tm)
    assert keep % rows == 0
    blocks, per_seq = keep // rows, seq // tm
    spec = pl.BlockSpec((None, None, rows * HEADS, HEAD_DIM),
                        lambda i: (i // per_seq, jnp.maximum(i % per_seq - (per_seq - blocks), 0), 0, 0))
    return spec, jax.ShapeDtypeStruct((batch, blocks, rows * HEADS, HEAD_DIM), F32)


def _proj_a(u, w_zq, w_gate, qn, *, batch, seq, prompt, tm=256):
    m, d = u.shape
    tm = _row_tile(seq if prompt else m, tm)
    fixed = lambda i: (0, 0)
    rows = lambda width: pl.BlockSpec((tm, width), lambda i: (i, 0))
    scratch = []
    if prompt:
        q_specs, q_shapes = _residue_specs(batch, seq, tm, BF16)
        scratch = [pltpu.VMEM((N_SLABS, HEADS, tm, HEAD_DIM), F32)]
    else:
        spec, shape = _head_rows_spec(N_GROUPS, m, tm)
        q_specs, q_shapes = [spec], [shape]
    return pl.pallas_call(
        functools.partial(_proj_a_kernel, prompt=prompt),
        grid=(m // tm,),
        in_specs=[rows(d), _resident(w_zq.shape, fixed), _resident(w_gate.shape, fixed),
                  _resident((1, HEAD_DIM), fixed)],
        out_specs=[rows(GROUP_WIDTH), *q_specs, rows(w_gate.shape[1])],
        out_shape=[jax.ShapeDtypeStruct((m, GROUP_WIDTH), F32), *q_shapes,
                   jax.ShapeDtypeStruct((m, w_gate.shape[1]), BF16)],
        scratch_shapes=scratch,
        compiler_params=_params("parallel"),
        name="proj_a",
    )(u, w_zq, w_gate, qn)


def _proj_b(u, w_kv, kn, *, batch, seq, prompt, tm=256):
    m, d = u.shape
    tm = _row_tile(seq if prompt else m, tm)
    fixed = lambda i: (0, 0)
    scratch = []
    if prompt:
        kb_specs, kb_shapes = _residue_specs(batch, seq, tm, BF16)
        kf = [_suffix_head_rows_spec(batch, seq, tm, min(window, seq)) for window, _ in ATTN_GROUPS]
        kf_specs, kf_shapes = [s for s, _ in kf], [s for _, s in kf]
        out_specs = [*kb_specs, *kb_specs, *kf_specs, *kf_specs]
        out_shape = [*kb_shapes, *kb_shapes, *kf_shapes, *kf_shapes]
        scratch = [pltpu.VMEM((2, N_SLABS, HEADS, tm, HEAD_DIM), F32)]
    else:
        spec, shape = _head_rows_spec(N_GROUPS, m, tm)
        out_specs, out_shape = [spec, spec], [shape, shape]
    return pl.pallas_call(
        functools.partial(_proj_b_kernel, prompt=prompt),
        grid=(m // tm,),
        in_specs=[pl.BlockSpec((tm, d), lambda i: (i, 0)), _resident(w_kv.shape, fixed),
                  _resident((1, HEAD_DIM), fixed)],
        out_specs=out_specs,
        out_shape=out_shape,
        scratch_shapes=scratch,
        compiler_params=_params("arbitrary"),
        name="proj_b",
    )(u, w_kv, kn)


def _pool_kernel(z_ref, halo_ref, pm_ref, ps_ref, y_ref, ext_ref, lvl_ref, *, tp, pos0, zero_first_halo):
    i = pl.program_id(1)
    halo = halo_ref[...]
    if zero_first_halo:
        halo = jnp.where(i == 0, 0.0, halo)
    top, n = POOL_PAD + POOL_HALO, POOL_HALO + tp
    pad = jnp.zeros((POOL_PAD, ext_ref.shape[1]), F32)
    ext_ref[0:POOL_PAD, :] = pad
    ext_ref[POOL_PAD:top, :] = halo
    ext_ref[top:, :] = z_ref[...]
    lvl_ref[0, 0:POOL_PAD, :] = pad
    lvl_ref[1, 0:POOL_PAD, :] = pad
    pos = pos0 + i * tp + lax.broadcasted_iota(jnp.int32, (tp, 1), 0)
    for gi, w in enumerate(POOL_WINDOWS):
        cols = slice(gi * POOL_GROUP_WIDTH, (gi + 1) * POOL_GROUP_WIDTH)
        zc = ext_ref[pl.ds(top, tp), cols]
        src, slot, span = ext_ref, 0, 1
        while span < w:
            both = src[pl.ds(POOL_PAD, n), cols] + src[pl.ds(POOL_PAD - span, n), cols]
            span *= 2
            if span < w:
                lvl_ref[slot, pl.ds(POOL_PAD, n), cols] = both
                src, slot = lvl_ref.at[slot], 1 - slot
        wsum = both[POOL_HALO:]
        cnt = jnp.minimum(w, pos + 1).astype(F32)
        pooled = (wsum / cnt - zc).astype(BF16)
        y = jnp.dot(pooled, pm_ref[gi], preferred_element_type=F32) * ps_ref[:, cols]
        y_ref[:, cols] = y.astype(BF16)


def _pool(z, halo, pool_map, pool_scale, *, tp, pos0, zero_first_halo):
    b, t, p = z.shape
    tp = _row_tile(t, tp)
    hb = tp // POOL_HALO
    if halo is z:
        halo_map = lambda bi, i: (bi, jnp.maximum(i * hb - 1, 0), 0)
    else:
        assert t == tp and halo.shape == (b, POOL_HALO, p)
        halo_map = lambda bi, i: (bi, 0, 0)
    return pl.pallas_call(
        functools.partial(_pool_kernel, tp=tp, pos0=pos0, zero_first_halo=zero_first_halo),
        grid=(b, t // tp),
        in_specs=[
            pl.BlockSpec((None, tp, p), lambda bi, i: (bi, i, 0)),
            pl.BlockSpec((None, POOL_HALO, p), halo_map),
            _resident(pool_map.shape, lambda bi, i: (0, 0, 0)),
            _resident((1, p), lambda bi, i: (0, 0)),
        ],
        out_specs=pl.BlockSpec((None, tp, p), lambda bi, i: (bi, i, 0)),
        out_shape=jax.ShapeDtypeStruct((b, t, p), BF16),
        scratch_shapes=[pltpu.VMEM((POOL_PAD + POOL_HALO + tp, p), F32),
                        pltpu.VMEM((2, POOL_PAD + POOL_HALO + tp, p), F32)],
        compiler_params=_params("parallel", "arbitrary"),
        name="pool",
    )(z, halo, pool_map, pool_scale)


def _band_attn_kernel(q_ref, k_ref, v_ref, o_ref, *, nblk, kw, dil):
    rb = q_ref.shape[0]
    i = pl.program_id(2)

    def body(it, carry):
        rl, nl = it // nblk, it % nblk
        n = i * nblk + nl
        if kw == BAND:
            kstart, off = 0, 0
        else:
            kstart = pl.multiple_of(jnp.maximum(n - 1, 0) * BAND, BAND)
            off = n * BAND - kstart
        qstart = pl.multiple_of(nl * BAND, BAND)
        steps = (off + lax.broadcasted_iota(jnp.int32, (BAND, kw), 0)
                 - lax.broadcasted_iota(jnp.int32, (BAND, kw), 1))
        valid = jnp.abs(steps - BAND // 2) <= BAND // 2
        neg_dist = jnp.where(valid, (steps * dil).astype(F32) * -LOG2_E, NEG_INF)
        lane = lax.broadcasted_iota(jnp.int32, (BAND, LSE_LANES), 1)
        lse_tile = jnp.zeros((BAND, LSE_LANES), F32)
        scores = []
        for hh in range(HEADS):
            qh = q_ref[rl, pl.ds(qstart, BAND), _head_cols(hh)]
            kh = k_ref[rl, pl.ds(kstart, kw), _head_cols(hh)]
            scores.append(lax.dot_general(qh, kh, (((1,), (1,)), ((), ())), preferred_element_type=F32))
        probs, dens = [], []
        for hh in range(HEADS):
            logits = scores[hh] + (2.0 ** -(hh + 1)) * neg_dist
            mx = jnp.max(logits, axis=-1, keepdims=True)
            p = jnp.exp2(logits - mx)
            den = jnp.sum(p, axis=-1, keepdims=True)
            probs.append(p.astype(BF16))
            dens.append(den)
            lse_tile = jnp.where(lane == hh, (mx + jnp.log2(den)) * LN_2, lse_tile)
        for hh in range(HEADS):
            vh = v_ref[rl, pl.ds(kstart, kw), _head_cols(hh)]
            o = jnp.dot(probs[hh], vh, preferred_element_type=F32) / dens[hh]
            o_ref[rl, pl.ds(qstart, BAND), _head_cols(hh)] = o
        o_ref[rl, pl.ds(qstart, BAND), GROUP_WIDTH:] = lse_tile
        return carry

    lax.fori_loop(0, rb * nblk, body, 0, unroll=True)


def _band_attn(q, k, v, g, *, blocks_per_step=4):
    batch, dil, mr, _ = q.shape
    assert dil == DILS[g] and mr % BAND == 0
    tq = min(mr, blocks_per_step * BAND)
    nblk = tq // BAND
    rb = min(dil, blocks_per_step // nblk)
    kw = min(mr, 2 * BAND)
    return pl.pallas_call(
        functools.partial(_band_attn_kernel, nblk=nblk, kw=kw, dil=dil),
        grid=(batch, dil // rb, mr // tq),
        in_specs=[
            pl.BlockSpec((None, rb, tq, GROUP_WIDTH), lambda b, r, i: (b, r, i, 0)),
            pl.BlockSpec((None, rb, mr, GROUP_WIDTH), lambda b, r, i: (b, r, 0, 0)),
            pl.BlockSpec((None, rb, mr, GROUP_WIDTH), lambda b, r, i: (b, r, 0, 0)),
        ],
        out_specs=pl.BlockSpec((None, rb, tq, ATTN_OUT_COLS), lambda b, r, i: (b, r, i, 0)),
        out_shape=jax.ShapeDtypeStruct((batch, dil, mr, ATTN_OUT_COLS), F32),
        compiler_params=_params("parallel", "parallel", "arbitrary"),
        name="band_attn_g%d" % g,
    )(q, k, v)


def _roll_copies(b, cache_ref, new_ref, out_hbm, sems, which):
    length, t_new = cache_ref.shape[1], new_ref.shape[2]
    kept = length - t_new
    return (pltpu.make_async_copy(cache_ref.at[0, pl.ds(t_new, kept)],
                                  out_hbm.at[b, pl.ds(0, kept)], sems.at[which, 0]),
            pltpu.make_async_copy(new_ref.at[0, 0], out_hbm.at[b, pl.ds(kept, t_new)], sems.at[which, 1]))


def _gather_attn_kernel(q_ref, kc_ref, vc_ref, kn_ref, vn_ref, o_ref, ko_hbm, vo_hbm, oh_ref, lh_ref, sems,
                        *, dil, t_new):
    b = pl.program_id(0)
    copies = (*_roll_copies(b, kc_ref, kn_ref, ko_hbm, sems, 0),
              *_roll_copies(b, vc_ref, vn_ref, vo_hbm, sems, 1))
    for copy in copies:
        copy.start()
    length = kc_ref.shape[1]
    head = lax.broadcasted_iota(jnp.int32, (HEADS, 1), 0)
    slope = lax.bitcast_convert_type((126 - head) << 23, F32) * LOG2_E
    for i in range(t_new):
        qi = q_ref[0, 0, i]
        n_new = i // dil + 1
        n_old = BAND + 1 - n_new
        first = length + i - BAND * dil
        assert first >= 0 and first + (n_old - 1) * dil < length
        kc = kc_ref[0, pl.ds(first, n_old, stride=dil)]
        vc = vc_ref[0, pl.ds(first, n_old, stride=dil)]
        steps = BAND - lax.broadcasted_iota(jnp.int32, (n_old, 1, 1), 0)
        sc = jnp.sum(kc * qi[None], axis=-1, keepdims=True) - slope[None] * (steps * dil).astype(F32)
        sn = [jnp.sum(kn_ref[0, 0, i - jj * dil] * qi, axis=-1, keepdims=True) - slope * float(jj * dil)
              for jj in range(n_new)]
        mx = jnp.max(sc, axis=0)
        for s in sn:
            mx = jnp.maximum(mx, s)
        pc = jnp.exp2(sc - mx[None])
        den = jnp.sum(pc, axis=0)
        acc = jnp.sum(pc * vc, axis=0)
        for jj, s in enumerate(sn):
            pn = jnp.exp2(s - mx)
            den = den + pn
            acc = acc + pn * vn_ref[0, 0, i - jj * dil]
        oh_ref[i] = acc / den
        lh_ref[i] = jnp.broadcast_to((mx + jnp.log2(den)) * LN_2, (HEADS, LSE_LANES))
    lane = lax.broadcasted_iota(jnp.int32, (t_new, LSE_LANES), 1)
    lse_tile = jnp.zeros((t_new, LSE_LANES), F32)
    for hh in range(HEADS):
        o_ref[:, _head_cols(hh)] = oh_ref[:, hh, :]
        lse_tile = jnp.where(lane == hh, lh_ref[:, hh, :], lse_tile)
    o_ref[:, GROUP_WIDTH:] = lse_tile
    for copy in copies:
        copy.wait()


def _gather_attn(q, kc, vc, kn, vn, g):
    win, dil = ATTN_GROUPS[g]
    b, length = kc.shape[:2]
    t_new = q.shape[2]
    assert length == win and t_new <= dil * BAND
    new = pl.BlockSpec((1, 1, t_new, HEADS, HEAD_DIM), lambda bi: (g, bi, 0, 0, 0))
    cache = pl.BlockSpec((1, length, HEADS, HEAD_DIM), lambda bi: (bi, 0, 0, 0))
    whole_hbm = pl.BlockSpec(memory_space=pl.ANY)
    return pl.pallas_call(
        functools.partial(_gather_attn_kernel, dil=dil, t_new=t_new),
        grid=(b,),
        in_specs=[new, cache, cache, new, new],
        out_specs=[pl.BlockSpec((None, t_new, ATTN_OUT_COLS), lambda bi: (bi, 0, 0)), whole_hbm, whole_hbm],
        out_shape=[jax.ShapeDtypeStruct((b, t_new, ATTN_OUT_COLS), F32),
                   jax.ShapeDtypeStruct(kc.shape, F32), jax.ShapeDtypeStruct(vc.shape, F32)],
        scratch_shapes=[pltpu.VMEM((t_new, HEADS, HEAD_DIM), F32),
                        pltpu.VMEM((t_new, HEADS, LSE_LANES), F32),
                        pltpu.SemaphoreType.DMA((2, 2))],
        compiler_params=_params("arbitrary"),
        name="gather_attn_g%d" % g,
    )(q, kc, vc, kn, vn)


def _merge_kernel(h_ref, y_ref, o0_ref, o1_ref, o2_ref, gate_ref, wp_ref, wa_ref, wo_ref, out_ref,
                  oa_ref, mix_ref, tok_ref, *, dils):
    o_refs = (o0_ref, o1_ref, o2_ref)
    tm, d = out_ref.shape
    s = pl.program_id(0)

    @pl.when(s == 0)
    def _():
        oa_ref[1] = jnp.zeros(oa_ref.shape[1:], BF16)
        mix_ref[1] = jnp.zeros(mix_ref.shape[1:], BF16)

    def tile(gi, c):
        if dils[gi] == 1:
            return o_refs[gi][0, :, c * 128:(c + 1) * 128]
        return tok_ref[gi, c]

    def step(fill, drain):
        for gi, dil in enumerate(dils):
            if dil == 1:
                continue
            n = tm // dil
            for r in range(dil):
                for c in range(ATTN_OUT_TILES):
                    tok_ref[gi, c, pl.ds(r, n, stride=dil), :] = o_refs[gi][r, :, c * 128:(c + 1) * 128]
        lses = [tile(gi, HEADS) for gi in range(N_GROUPS)]
        mx = jnp.maximum(jnp.maximum(lses[0], lses[1]), lses[2])
        es = [jnp.exp(l - mx) for l in lses]
        inv = 1.0 / (es[0] + es[1] + es[2])
        wts = [e * inv for e in es]
        for hh in range(HEADS):
            acc = wts[0][:, hh:hh + 1] * tile(0, hh)
            for gi in range(1, N_GROUPS):
                acc = acc + wts[gi][:, hh:hh + 1] * tile(gi, hh)
            oa_ref[fill, :, _head_cols(hh)] = acc.astype(BF16)

        a_attn = jnp.dot(oa_ref[drain], wa_ref[...], preferred_element_type=F32)
        a_pool = jnp.dot(y_ref[...], wp_ref[...], preferred_element_type=F32)
        mix = gate_ref[:, :d].astype(F32) * a_pool + gate_ref[:, d:].astype(F32) * a_attn
        mix_ref[fill] = mix.astype(BF16)

        out_ref[...] = h_ref[...] + jnp.dot(mix_ref[drain], wo_ref[...], preferred_element_type=F32)

    pl.when(s % 2 == 0)(functools.partial(step, 0, 1))
    pl.when(s % 2 == 1)(functools.partial(step, 1, 0))


def _merge(h, y_pool, o_groups, gates, w_pool_out, w_attn_out, w_out, *, tm=256):
    m, d = h.shape
    dils = tuple(o.shape[1] for o in o_groups)
    seq = o_groups[0].shape[1] * o_groups[0].shape[2]
    tm = _row_tile(seq, tm)
    per_seq, n_tiles = seq // tm, m // tm
    last = n_tiles - 1
    row1 = lambda s: (jnp.clip(s - 1, 0, last), 0)
    row2 = lambda s: (jnp.maximum(s - 2, 0), 0)
    fixed = lambda s: (0, 0)

    def attn_tile(s):
        i = jnp.minimum(s, last)
        return (i // per_seq, 0, i % per_seq, 0)

    o_specs = [pl.BlockSpec((None, dil, tm // dil, ATTN_OUT_COLS), attn_tile) for dil in dils]
    return pl.pallas_call(
        functools.partial(_merge_kernel, dils=dils),
        grid=(n_tiles + 2,),
        in_specs=[
            pl.BlockSpec((tm, d), row2),
            pl.BlockSpec((tm, POOL_WIDTH), row1),
            *o_specs,
            pl.BlockSpec((tm, 2 * d), row1),
            _resident(w_pool_out.shape, fixed),
            _resident(w_attn_out.shape, fixed),
            _resident(w_out.shape, fixed),
        ],
        out_specs=pl.BlockSpec((tm, d), row2),
        out_shape=jax.ShapeDtypeStruct((m, d), F32),
        scratch_shapes=[pltpu.VMEM((2, tm, GROUP_WIDTH), BF16),
                        pltpu.VMEM((2, tm, d), BF16),
                        pltpu.VMEM((N_GROUPS, ATTN_OUT_TILES, tm, 128), F32)],
        compiler_params=_params("arbitrary"),
        name="merge",
    )(h, y_pool, *o_groups, gates, w_pool_out, w_attn_out, w_out)


def _ple_kernel(h_ref, p_ref, g_ref, wg_ref, wp_ref, out_ref):
    h = h_ref[...]
    hn = _rms(h, g_ref[...]).astype(BF16)
    gate = jax.nn.sigmoid(jnp.dot(hn, wg_ref[...], preferred_element_type=F32))
    emb = jnp.dot(p_ref[...].astype(BF16), wp_ref[...], preferred_element_type=F32)
    out_ref[...] = h + gate * emb


def _ple(h, p, g, w_gate, w_proj, *, tm=1024):
    m, d = h.shape
    tm = _row_tile(m, tm)
    row = lambda i: (i, 0)
    fixed = lambda i: (0, 0)
    return pl.pallas_call(
        _ple_kernel,
        grid=(m // tm,),
        in_specs=[
            pl.BlockSpec((tm, d), row),
            pl.BlockSpec((tm, p.shape[1]), row),
            _resident((1, d), fixed),
            _resident(w_gate.shape, fixed),
            _resident(w_proj.shape, fixed),
        ],
        out_specs=pl.BlockSpec((tm, d), row),
        out_shape=jax.ShapeDtypeStruct((m, d), F32),
        compiler_params=_params("parallel"),
        name="ple",
    )(h, p, g, w_gate, w_proj)


def _layer(x, p, w, *, prompt, pool_prev=None, caches=None):
    b, t, d = x.shape
    m = b * t
    h1, u = _ffn(x.reshape(m, d), w["ffn1_norm"], w["ffn1_w_in"], w["ffn1_w_out"], w["mix_norm"])
    outs_a = _proj_a(u, w["w_zq"], w["w_gate"], w["q_norm"], batch=b, seq=t, prompt=prompt)
    outs_b = _proj_b(u, w["w_kv"], w["k_norm"], batch=b, seq=t, prompt=prompt)
    z, gates = outs_a[0], outs_a[-1]
    z3 = z.reshape(b, t, POOL_WIDTH)
    if prompt:
        qs = outs_a[1:1 + N_GROUPS]
        kbs, vbs = outs_b[0:N_GROUPS], outs_b[N_GROUPS:2 * N_GROUPS]
        kfs, vfs = outs_b[2 * N_GROUPS:3 * N_GROUPS], outs_b[3 * N_GROUPS:]
        y_pool = _pool(z3, z3, w["pool_map"], w["pool_scale"], tp=512, pos0=0, zero_first_halo=True)
        new_pool = z3[:, t - POOL_STATE:]
        o_groups = [_band_attn(qs[g], kbs[g], vbs[g], g) for g in range(N_GROUPS)]
        new_kv = []
        for g in range(N_GROUPS):
            new_kv += [kfs[g].reshape(b, -1, HEADS, HEAD_DIM), vfs[g].reshape(b, -1, HEADS, HEAD_DIM)]
    else:
        qf = outs_a[1].reshape(N_GROUPS, b, t, HEADS, HEAD_DIM)
        kn = outs_b[0].reshape(N_GROUPS, b, t, HEADS, HEAD_DIM)
        vn = outs_b[1].reshape(N_GROUPS, b, t, HEADS, HEAD_DIM)
        halo = jnp.pad(pool_prev, ((0, 0), (POOL_HALO - POOL_STATE, 0), (0, 0)))
        y_pool = _pool(z3, halo, w["pool_map"], w["pool_scale"], tp=t, pos0=PAST_LEN, zero_first_halo=False)
        new_pool = jnp.concatenate([pool_prev, z3], axis=1)[:, t:]
        o_groups, new_kv = [], []
        for g in range(N_GROUPS):
            kc, vc = caches[2 * g], caches[2 * g + 1]
            o, k_new, v_new = _gather_attn(qf, kc, vc, kn, vn, g)
            o_groups.append(o.reshape(1, 1, m, ATTN_OUT_COLS))
            new_kv += [k_new, v_new]
    h2 = _merge(h1, y_pool.reshape(m, POOL_WIDTH), o_groups, gates,
                w["w_pool_out"], w["w_attn_out"], w["w_out"])
    h3 = _ffn(h2, w["ffn2_norm"], w["ffn2_w_in"], w["ffn2_w_out"])
    h4 = _ple(h3, p.reshape(m, -1), w["ple_norm"], w["ple_gate"], w["ple_proj"])
    return h4.reshape(b, t, d), new_pool, new_kv


def kernel(x_prompt, x_sample, p_prompt, p_sample, state_pool, cache_k0, cache_v0, cache_k1, cache_v1, cache_k2, cache_v2, ffn1_norm, ffn1_w_in, ffn1_w_out, mix_norm, w_in, q_norm, k_norm, pool_map, pool_scale, w_pool_out, w_attn_out, w_out, ffn2_norm, ffn2_w_in, ffn2_w_out, ple_norm, ple_gate, ple_proj):
    caches = (cache_k0, cache_v0, cache_k1, cache_v1, cache_k2, cache_v2)
    depth = ffn1_norm.shape[0]
    kv0 = POOL_WIDTH + N_GROUPS * GROUP_WIDTH
    gate0 = kv0 + 2 * N_GROUPS * GROUP_WIDTH
    hp, hs = x_prompt, x_sample
    pool_p, pool_s, kv_p, kv_s = [], [], [], []
    for i in range(depth):
        row = lambda a: a[i][None, :]
        w = dict(
            ffn1_norm=row(ffn1_norm), ffn1_w_in=ffn1_w_in[i].astype(BF16), ffn1_w_out=ffn1_w_out[i].astype(BF16),
            mix_norm=row(mix_norm), q_norm=row(q_norm), k_norm=row(k_norm),
            w_zq=w_in[i, :, :kv0].astype(BF16), w_kv=w_in[i, :, kv0:gate0].astype(BF16),
            w_gate=w_in[i, :, gate0:].astype(BF16),
            pool_map=pool_map[i].astype(BF16), pool_scale=row(pool_scale),
            w_pool_out=w_pool_out[i].astype(BF16), w_attn_out=w_attn_out[i].astype(BF16),
            w_out=w_out[i].astype(BF16),
            ffn2_norm=row(ffn2_norm), ffn2_w_in=ffn2_w_in[i].astype(BF16), ffn2_w_out=ffn2_w_out[i].astype(BF16),
            ple_norm=row(ple_norm), ple_gate=ple_gate[i].astype(BF16), ple_proj=ple_proj[i].astype(BF16),
        )
        hp, pp, kvp = _layer(hp, p_prompt[i], w, prompt=True)
        hs, ps, kvs = _layer(hs, p_sample[i], w, prompt=False, pool_prev=state_pool[i],
                             caches=tuple(c[i] for c in caches))
        pool_p.append(pp)
        pool_s.append(ps)
        kv_p.append(kvp)
        kv_s.append(kvs)
    stack = lambda rows: jnp.stack(rows)
    kp = [stack([r[j] for r in kv_p]) for j in range(2 * N_GROUPS)]
    ks = [stack([r[j] for r in kv_s]) for j in range(2 * N_GROUPS)]
    return (hp, hs, stack(pool_p), *kp, stack(pool_s), *ks)
```

```python
import functools
import math

import jax
import jax.numpy as jnp
from jax import lax
from jax.experimental import pallas as pl
from jax.experimental.pallas import tpu as pltpu

F32 = jnp.float32
BF16 = jnp.bfloat16

POOL_WINDOWS = (2, 4, 8, 16)
POOL_GROUP_WIDTH = 256
POOL_WIDTH = POOL_GROUP_WIDTH * len(POOL_WINDOWS)
POOL_STATE = max(POOL_WINDOWS) - 1
POOL_HALO = 16
POOL_PAD = 8
HEAD_DIM = 128
HEADS = 8
GROUP_WIDTH = HEADS * HEAD_DIM
ATTN_GROUPS = ((128, 1), (512, 4), (2048, 16))
DILS = tuple(d for _, d in ATTN_GROUPS)
N_GROUPS = len(ATTN_GROUPS)
REORDER_STRIDE = 4
REORDER_PASSES = tuple(round(math.log(d, REORDER_STRIDE)) for d in DILS)
SLAB_OF = tuple(sum(REORDER_PASSES[:g]) for g in range(N_GROUPS))
N_SLABS = sum(REORDER_PASSES)
BAND = 128
LSE_LANES = 128
ATTN_OUT_COLS = GROUP_WIDTH + LSE_LANES
ATTN_OUT_TILES = ATTN_OUT_COLS // 128
RMS_EPS = 1e-6
NEG_INF = -1e30
LOG2_E = 1.4426950408889634
LN_2 = 0.6931471805599453
Q_SCALE = HEAD_DIM ** -0.5 * LOG2_E
PAST_LEN = 16384

VMEM_LIMIT = 60 * 1024 * 1024


def _params(*semantics):
    return pltpu.CompilerParams(dimension_semantics=semantics, vmem_limit_bytes=VMEM_LIMIT)


def _resident(shape, index_map):
    return pl.BlockSpec(shape, index_map, pipeline_mode=pl.Buffered(1))


def _rms(x, g):
    return x * lax.rsqrt(jnp.mean(x * x, axis=-1, keepdims=True) + RMS_EPS) * g


def _row_tile(m, want):
    t = min(m, want)
    assert m % t == 0, (m, t)
    return t


def _head_cols(hh):
    return slice(hh * HEAD_DIM, (hh + 1) * HEAD_DIM)


def _ffn_kernel(*refs, emit_norm):
    if emit_norm:
        x_ref, g_ref, wa_ref, wb_ref, wo_ref, gn_ref, o_ref, u_ref = refs
        xn_ref = u_ref
    else:
        x_ref, g_ref, wa_ref, wb_ref, wo_ref, o_ref, xn_ref = refs
    k = pl.program_id(1)

    @pl.when(k == 0)
    def _():
        x = x_ref[...]
        xn_ref[...] = _rms(x, g_ref[...]).astype(BF16)
        o_ref[...] = x

    xn = xn_ref[...]
    a = jnp.dot(xn, wa_ref[...], preferred_element_type=F32)
    b = jnp.dot(xn, wb_ref[...], preferred_element_type=F32)
    hid = (0.5 * (a * jax.nn.sigmoid(a)) * b).astype(BF16)
    o_ref[...] += jnp.dot(hid, wo_ref[...], preferred_element_type=F32)

    if emit_norm:
        @pl.when(k == pl.num_programs(1) - 1)
        def _():
            u_ref[...] = _rms(o_ref[...], gn_ref[...]).astype(BF16)


def _ffn(x, g, w_in, w_out, g_next=None, *, tm=1024, tf=512):
    m, d = x.shape
    d_ff = w_out.shape[0]
    tm = _row_tile(m, tm)
    tf = _row_tile(d_ff, tf)
    nk = d_ff // tf
    emit_norm = g_next is not None
    row = pl.BlockSpec((tm, d), lambda i, k: (i, 0))
    vec = _resident((1, d), lambda i, k: (0, 0))
    in_specs = [row, vec,
                pl.BlockSpec((d, tf), lambda i, k: (0, k)),
                pl.BlockSpec((d, tf), lambda i, k: (0, k + nk)),
                pl.BlockSpec((tf, d), lambda i, k: (k, 0))]
    args = [x, g, w_in, w_in, w_out]
    out_specs, out_shape = row, jax.ShapeDtypeStruct((m, d), F32)
    if emit_norm:
        in_specs.append(vec)
        args.append(g_next)
        out_specs = [row, row]
        out_shape = [out_shape, jax.ShapeDtypeStruct((m, d), BF16)]
    return pl.pallas_call(
        functools.partial(_ffn_kernel, emit_norm=emit_norm),
        grid=(m // tm, nk),
        in_specs=in_specs,
        out_specs=out_specs,
        out_shape=out_shape,
        scratch_shapes=[] if emit_norm else [pltpu.VMEM((tm, d), BF16)],
        compiler_params=_params("parallel", "arbitrary"),
        name="ffn",
    )(*args)


MXU_COLS = 512


def _tile_heads(u_ref, w_ref):
    heads = []
    for c in range(0, GROUP_WIDTH, MXU_COLS):
        part = jnp.dot(u_ref[...], w_ref[:, c:c + MXU_COLS], preferred_element_type=F32)
        heads += [part[:, i:i + HEAD_DIM] for i in range(0, MXU_COLS, HEAD_DIM)]
    return heads


def _store_cols(ref, heads, dtype):
    for hh, x in enumerate(heads):
        ref[:, _head_cols(hh)] = x.astype(dtype)


def _store_head_rows(ref, heads):
    for hh, x in enumerate(heads):
        ref[pl.ds(hh, x.shape[0], stride=HEADS), :] = x


def _store_residue_major(ref, slab_ref, heads, dil):
    if dil == 1:
        _store_cols(ref.at[0], heads, BF16)
        return
    passes = slab_ref.shape[0]
    assert REORDER_STRIDE ** passes == dil
    for hh, x in enumerate(heads):
        slab_ref[0, hh] = x
    chunk, residues = heads[0].shape[0], [0]
    for p in range(passes):
        sub, weight = chunk // REORDER_STRIDE, REORDER_STRIDE ** p
        for c, res in enumerate(residues):
            for r in range(REORDER_STRIDE):
                for hh in range(HEADS):
                    rows = slab_ref[p, hh, pl.ds(c * chunk + r, sub, stride=REORDER_STRIDE), :]
                    if p + 1 < passes:
                        slab_ref[p + 1, hh, pl.ds(c * chunk + r * sub, sub), :] = rows
                    else:
                        ref[res + r * weight, :, _head_cols(hh)] = rows.astype(BF16)
        residues = [res + r * weight for res in residues for r in range(REORDER_STRIDE)]
        chunk = sub


_HEAVY_FIRST = tuple(sorted(range(N_GROUPS), key=lambda g: -DILS[g]))


def _proj_a_kernel(*refs, prompt, n_gate):
    u_ref, wz_ref, *refs = refs
    wq_refs, wg_refs, refs = refs[:N_GROUPS], refs[N_GROUPS:N_GROUPS + n_gate], refs[N_GROUPS + n_gate:]
    if prompt:
        qn_ref, z_ref, q0_ref, q1_ref, q2_ref, gate_ref, slab_ref = refs
        q_refs = (q0_ref, q1_ref, q2_ref)
    else:
        qn_ref, z_ref, q_ref, gate_ref = refs
    for g in _HEAVY_FIRST:
        q = [_rms(x, qn_ref[...]) * Q_SCALE for x in _tile_heads(u_ref, wq_refs[g])]
        if prompt:
            _store_residue_major(q_refs[g], slab_ref.at[pl.ds(SLAB_OF[g], REORDER_PASSES[g])], q, DILS[g])
        else:
            _store_head_rows(q_ref.at[g], q)
    for t, wg_ref in enumerate(wg_refs):
        gate = [jax.nn.sigmoid(x) for x in _tile_heads(u_ref, wg_ref)]
        _store_cols(gate_ref.at[:, t * GROUP_WIDTH:(t + 1) * GROUP_WIDTH], gate, BF16)
    _store_cols(z_ref, _tile_heads(u_ref, wz_ref), F32)


def _proj_b_kernel(*refs, prompt):
    u_ref, *refs = refs
    wk_refs, wv_refs, refs = refs[:N_GROUPS], refs[N_GROUPS:2 * N_GROUPS], refs[2 * N_GROUPS:]
    if prompt:
        (kn_ref, kb0_ref, kb1_ref, kb2_ref, vb0_ref, vb1_ref, vb2_ref,
         kf0_ref, kf1_ref, kf2_ref, vf0_ref, vf1_ref, vf2_ref, slab_ref) = refs
        kb_refs, vb_refs = (kb0_ref, kb1_ref, kb2_ref), (vb0_ref, vb1_ref, vb2_ref)
        kf_refs, vf_refs = (kf0_ref, kf1_ref, kf2_ref), (vf0_ref, vf1_ref, vf2_ref)
    else:
        kn_ref, kf_ref, vf_ref = refs
        kf_refs = [kf_ref.at[g] for g in range(N_GROUPS)]
        vf_refs = [vf_ref.at[g] for g in range(N_GROUPS)]
    tm = u_ref.shape[0]
    for g in _HEAVY_FIRST:
        k = [_rms(x, kn_ref[...]) for x in _tile_heads(u_ref, wk_refs[g])]
        v = _tile_heads(u_ref, wv_refs[g])
        kept = kf_refs[g].shape[0] // HEADS
        _store_head_rows(kf_refs[g], [x[tm - kept:] for x in k])
        _store_head_rows(vf_refs[g], [x[tm - kept:] for x in v])
        if prompt:
            slabs = pl.ds(SLAB_OF[g], REORDER_PASSES[g])
            _store_residue_major(kb_refs[g], slab_ref.at[0, slabs], k, DILS[g])
            _store_residue_major(vb_refs[g], slab_ref.at[1, slabs], v, DILS[g])


def _residue_specs(batch, seq, tm, dtype):
    per_seq = seq // tm
    specs, shapes = [], []
    for dil in DILS:
        specs.append(pl.BlockSpec((None, dil, tm // dil, GROUP_WIDTH),
                                  lambda i: (i // per_seq, 0, i % per_seq, 0)))
        shapes.append(jax.ShapeDtypeStruct((batch, dil, seq // dil, GROUP_WIDTH), dtype))
    return specs, shapes


def _head_rows_spec(n_arrays, m, tm):
    return (pl.BlockSpec((n_arrays, tm * HEADS, HEAD_DIM), lambda i: (0, i, 0)),
            jax.ShapeDtypeStruct((n_arrays, m * HEADS, HEAD_DIM), F32))


def _suffix_head_rows_spec(batch, seq, tm, keep):
    rows = min(keep, tm)
    assert keep % rows == 0
    blocks, per_seq = keep // rows, seq // tm
    spec = pl.BlockSpec((None, None, rows * HEADS, HEAD_DIM),
                        lambda i: (i // per_seq, jnp.maximum(i % per_seq - (per_seq - blocks), 0), 0, 0))
    return spec, jax.ShapeDtypeStruct((batch, blocks, rows * HEADS, HEAD_DIM), F32)


def _weight_tiles(w, tiles):
    d = w.shape[0]
    return [_resident((d, GROUP_WIDTH), functools.partial(lambda t, i: (0, t), t)) for t in tiles]


def _proj_a(u, w, qn, *, batch, seq, prompt, tm=256):
    m, d = u.shape
    gate0 = 1 + 3 * N_GROUPS
    tiles = [0, *range(1, 1 + N_GROUPS), *range(gate0, w.shape[1] // GROUP_WIDTH)]
    n_gate = len(tiles) - 1 - N_GROUPS
    tm = _row_tile(seq if prompt else m, tm)
    fixed = lambda i: (0, 0)
    rows = lambda width: pl.BlockSpec((tm, width), lambda i: (i, 0))
    scratch = []
    if prompt:
        q_specs, q_shapes = _residue_specs(batch, seq, tm, BF16)
        scratch = [pltpu.VMEM((N_SLABS, HEADS, tm, HEAD_DIM), F32)]
    else:
        spec, shape = _head_rows_spec(N_GROUPS, m, tm)
        q_specs, q_shapes = [spec], [shape]
    return pl.pallas_call(
        functools.partial(_proj_a_kernel, prompt=prompt, n_gate=n_gate),
        grid=(m // tm,),
        in_specs=[rows(d), *_weight_tiles(w, tiles), _resident((1, HEAD_DIM), fixed)],
        out_specs=[rows(GROUP_WIDTH), *q_specs, rows(n_gate * GROUP_WIDTH)],
        out_shape=[jax.ShapeDtypeStruct((m, GROUP_WIDTH), F32), *q_shapes,
                   jax.ShapeDtypeStruct((m, n_gate * GROUP_WIDTH), BF16)],
        scratch_shapes=scratch,
        compiler_params=_params("parallel"),
        name="proj_a",
    )(u, *[w] * len(tiles), qn)


def _proj_b(u, w, kn, *, batch, seq, prompt, tm=256):
    m, d = u.shape
    tiles = range(1 + N_GROUPS, 1 + 3 * N_GROUPS)
    tm = _row_tile(seq if prompt else m, tm)
    fixed = lambda i: (0, 0)
    scratch = []
    if prompt:
        kb_specs, kb_shapes = _residue_specs(batch, seq, tm, BF16)
        kf = [_suffix_head_rows_spec(batch, seq, tm, min(window, seq)) for window, _ in ATTN_GROUPS]
        kf_specs, kf_shapes = [s for s, _ in kf], [s for _, s in kf]
        out_specs = [*kb_specs, *kb_specs, *kf_specs, *kf_specs]
        out_shape = [*kb_shapes, *kb_shapes, *kf_shapes, *kf_shapes]
        scratch = [pltpu.VMEM((2, N_SLABS, HEADS, tm, HEAD_DIM), F32)]
    else:
        spec, shape = _head_rows_spec(N_GROUPS, m, tm)
        out_specs, out_shape = [spec, spec], [shape, shape]
    return pl.pallas_call(
        functools.partial(_proj_b_kernel, prompt=prompt),
        grid=(m // tm,),
        in_specs=[pl.BlockSpec((tm, d), lambda i: (i, 0)), *_weight_tiles(w, tiles),
                  _resident((1, HEAD_DIM), fixed)],
        out_specs=out_specs,
        out_shape=out_shape,
        scratch_shapes=scratch,
        compiler_params=_params("arbitrary"),
        name="proj_b",
    )(u, *[w] * len(tiles), kn)


def _pool_kernel(z_ref, halo_ref, pm_ref, ps_ref, y_ref, ext_ref, lvl_ref, *, tp, pos0, zero_first_halo):
    i = pl.program_id(1)
    halo = halo_ref[...]
    if zero_first_halo:
        halo = jnp.where(i == 0, 0.0, halo)
    top, n = POOL_PAD + POOL_HALO, POOL_HALO + tp
    pad = jnp.zeros((POOL_PAD, ext_ref.shape[1]), F32)
    ext_ref[0:POOL_PAD, :] = pad
    ext_ref[POOL_PAD:top, :] = halo
    ext_ref[top:, :] = z_ref[...]
    lvl_ref[0, 0:POOL_PAD, :] = pad
    lvl_ref[1, 0:POOL_PAD, :] = pad
    pos = pos0 + i * tp + lax.broadcasted_iota(jnp.int32, (tp, 1), 0)
    for gi, w in enumerate(POOL_WINDOWS):
        cols = slice(gi * POOL_GROUP_WIDTH, (gi + 1) * POOL_GROUP_WIDTH)
        zc = ext_ref[pl.ds(top, tp), cols]
        src, slot, span = ext_ref, 0, 1
        while span < w:
            both = src[pl.ds(POOL_PAD, n), cols] + src[pl.ds(POOL_PAD - span, n), cols]
            span *= 2
            if span < w:
                lvl_ref[slot, pl.ds(POOL_PAD, n), cols] = both
                src, slot = lvl_ref.at[slot], 1 - slot
        wsum = both[POOL_HALO:]
        cnt = jnp.minimum(w, pos + 1).astype(F32)
        pooled = (wsum / cnt - zc).astype(BF16)
        y = jnp.dot(pooled, pm_ref[gi], preferred_element_type=F32) * ps_ref[:, cols]
        y_ref[:, cols] = y.astype(BF16)


def _pool(z, halo, pool_map, pool_scale, *, tp, pos0, zero_first_halo):
    b, t, p = z.shape
    tp = _row_tile(t, tp)
    hb = tp // POOL_HALO
    if halo is z:
        halo_map = lambda bi, i: (bi, jnp.maximum(i * hb - 1, 0), 0)
    else:
        assert t == tp and halo.shape == (b, POOL_HALO, p)
        halo_map = lambda bi, i: (bi, 0, 0)
    return pl.pallas_call(
        functools.partial(_pool_kernel, tp=tp, pos0=pos0, zero_first_halo=zero_first_halo),
        grid=(b, t // tp),
        in_specs=[
            pl.BlockSpec((None, tp, p), lambda bi, i: (bi, i, 0)),
            pl.BlockSpec((None, POOL_HALO, p), halo_map),
            _resident(pool_map.shape, lambda bi, i: (0, 0, 0)),
            _resident((1, p), lambda bi, i: (0, 0)),
        ],
        out_specs=pl.BlockSpec((None, tp, p), lambda bi, i: (bi, i, 0)),
        out_shape=jax.ShapeDtypeStruct((b, t, p), BF16),
        scratch_shapes=[pltpu.VMEM((POOL_PAD + POOL_HALO + tp, p), F32),
                        pltpu.VMEM((2, POOL_PAD + POOL_HALO + tp, p), F32)],
        compiler_params=_params("parallel", "arbitrary"),
        name="pool",
    )(z, halo, pool_map, pool_scale)


def _band_attn_kernel(q_ref, k_ref, v_ref, o_ref, *, nblk, kw, dil):
    rb = q_ref.shape[0]
    i = pl.program_id(2)

    def body(it, carry):
        rl, nl = it // nblk, it % nblk
        n = i * nblk + nl
        if kw == BAND:
            kstart, off = 0, 0
        else:
            kstart = pl.multiple_of(jnp.maximum(n - 1, 0) * BAND, BAND)
            off = n * BAND - kstart
        qstart = pl.multiple_of(nl * BAND, BAND)
        steps = (off + lax.broadcasted_iota(jnp.int32, (BAND, kw), 0)
                 - lax.broadcasted_iota(jnp.int32, (BAND, kw), 1))
        valid = jnp.abs(steps - BAND // 2) <= BAND // 2
        neg_dist = jnp.where(valid, (steps * dil).astype(F32) * -LOG2_E, NEG_INF)
        lane = lax.broadcasted_iota(jnp.int32, (BAND, LSE_LANES), 1)
        lse_tile = jnp.zeros((BAND, LSE_LANES), F32)
        scores = []
        for hh in range(HEADS):
            qh = q_ref[rl, pl.ds(qstart, BAND), _head_cols(hh)]
            kh = k_ref[rl, pl.ds(kstart, kw), _head_cols(hh)]
            scores.append(lax.dot_general(qh, kh, (((1,), (1,)), ((), ())), preferred_element_type=F32))
        probs, dens = [], []
        for hh in range(HEADS):
            logits = scores[hh] + (2.0 ** -(hh + 1)) * neg_dist
            mx = jnp.max(logits, axis=-1, keepdims=True)
            p = jnp.exp2(logits - mx)
            den = jnp.sum(p, axis=-1, keepdims=True)
            probs.append(p.astype(BF16))
            dens.append(den)
            lse_tile = jnp.where(lane == hh, (mx + jnp.log2(den)) * LN_2, lse_tile)
        for hh in range(HEADS):
            vh = v_ref[rl, pl.ds(kstart, kw), _head_cols(hh)]
            o = jnp.dot(probs[hh], vh, preferred_element_type=F32) / dens[hh]
            o_ref[rl, pl.ds(qstart, BAND), _head_cols(hh)] = o
        o_ref[rl, pl.ds(qstart, BAND), GROUP_WIDTH:] = lse_tile
        return carry

    lax.fori_loop(0, rb * nblk, body, 0, unroll=True)


def _band_attn(q, k, v, g, *, blocks_per_step=4):
    batch, dil, mr, _ = q.shape
    assert dil == DILS[g] and mr % BAND == 0
    tq = min(mr, blocks_per_step * BAND)
    nblk = tq // BAND
    rb = min(dil, blocks_per_step // nblk)
    kw = min(mr, 2 * BAND)
    return pl.pallas_call(
        functools.partial(_band_attn_kernel, nblk=nblk, kw=kw, dil=dil),
        grid=(batch, dil // rb, mr // tq),
        in_specs=[
            pl.BlockSpec((None, rb, tq, GROUP_WIDTH), lambda b, r, i: (b, r, i, 0)),
            pl.BlockSpec((None, rb, mr, GROUP_WIDTH), lambda b, r, i: (b, r, 0, 0)),
            pl.BlockSpec((None, rb, mr, GROUP_WIDTH), lambda b, r, i: (b, r, 0, 0)),
        ],
        out_specs=pl.BlockSpec((None, rb, tq, ATTN_OUT_COLS), lambda b, r, i: (b, r, i, 0)),
        out_shape=jax.ShapeDtypeStruct((batch, dil, mr, ATTN_OUT_COLS), F32),
        compiler_params=_params("parallel", "parallel", "arbitrary"),
        name="band_attn_g%d" % g,
    )(q, k, v)


def _roll_copies(b, cache_ref, new_ref, out_hbm, sems, which):
    length, t_new = cache_ref.shape[1], new_ref.shape[2]
    kept = length - t_new
    return (pltpu.make_async_copy(cache_ref.at[0, pl.ds(t_new, kept)],
                                  out_hbm.at[b, pl.ds(0, kept)], sems.at[which, 0]),
            pltpu.make_async_copy(new_ref.at[0, 0], out_hbm.at[b, pl.ds(kept, t_new)], sems.at[which, 1]))


def _gather_attn_kernel(q_ref, kc_ref, vc_ref, kn_ref, vn_ref, o_ref, ko_hbm, vo_hbm, oh_ref, lh_ref, sems,
                        *, dil, t_new):
    b = pl.program_id(0)
    copies = (*_roll_copies(b, kc_ref, kn_ref, ko_hbm, sems, 0),
              *_roll_copies(b, vc_ref, vn_ref, vo_hbm, sems, 1))
    for copy in copies:
        copy.start()
    length = kc_ref.shape[1]
    head = lax.broadcasted_iota(jnp.int32, (HEADS, 1), 0)
    slope = lax.bitcast_convert_type((126 - head) << 23, F32) * LOG2_E
    for i in range(t_new):
        qi = q_ref[0, 0, i]
        n_new = i // dil + 1
        n_old = BAND + 1 - n_new
        first = length + i - BAND * dil
        assert first >= 0 and first + (n_old - 1) * dil < length
        kc = kc_ref[0, pl.ds(first, n_old, stride=dil)]
        vc = vc_ref[0, pl.ds(first, n_old, stride=dil)]
        steps = BAND - lax.broadcasted_iota(jnp.int32, (n_old, 1, 1), 0)
        sc = jnp.sum(kc * qi[None], axis=-1, keepdims=True) - slope[None] * (steps * dil).astype(F32)
        sn = [jnp.sum(kn_ref[0, 0, i - jj * dil] * qi, axis=-1, keepdims=True) - slope * float(jj * dil)
              for jj in range(n_new)]
        mx = jnp.max(sc, axis=0)
        for s in sn:
            mx = jnp.maximum(mx, s)
        pc = jnp.exp2(sc - mx[None])
        den = jnp.sum(pc, axis=0)
        acc = jnp.sum(pc * vc, axis=0)
        for jj, s in enumerate(sn):
            pn = jnp.exp2(s - mx)
            den = den + pn
            acc = acc + pn * vn_ref[0, 0, i - jj * dil]
        oh_ref[i] = acc / den
        lh_ref[i] = jnp.broadcast_to((mx + jnp.log2(den)) * LN_2, (HEADS, LSE_LANES))
    lane = lax.broadcasted_iota(jnp.int32, (t_new, LSE_LANES), 1)
    lse_tile = jnp.zeros((t_new, LSE_LANES), F32)
    for hh in range(HEADS):
        o_ref[:, _head_cols(hh)] = oh_ref[:, hh, :]
        lse_tile = jnp.where(lane == hh, lh_ref[:, hh, :], lse_tile)
    o_ref[:, GROUP_WIDTH:] = lse_tile
    for copy in copies:
        copy.wait()


def _gather_attn(q, kc, vc, kn, vn, g):
    win, dil = ATTN_GROUPS[g]
    b, length = kc.shape[:2]
    t_new = q.shape[2]
    assert length == win and t_new <= dil * BAND
    new = pl.BlockSpec((1, 1, t_new, HEADS, HEAD_DIM), lambda bi: (g, bi, 0, 0, 0))
    cache = pl.BlockSpec((1, length, HEADS, HEAD_DIM), lambda bi: (bi, 0, 0, 0))
    whole_hbm = pl.BlockSpec(memory_space=pl.ANY)
    return pl.pallas_call(
        functools.partial(_gather_attn_kernel, dil=dil, t_new=t_new),
        grid=(b,),
        in_specs=[new, cache, cache, new, new],
        out_specs=[pl.BlockSpec((None, t_new, ATTN_OUT_COLS), lambda bi: (bi, 0, 0)), whole_hbm, whole_hbm],
        out_shape=[jax.ShapeDtypeStruct((b, t_new, ATTN_OUT_COLS), F32),
                   jax.ShapeDtypeStruct(kc.shape, F32), jax.ShapeDtypeStruct(vc.shape, F32)],
        scratch_shapes=[pltpu.VMEM((t_new, HEADS, HEAD_DIM), F32),
                        pltpu.VMEM((t_new, HEADS, LSE_LANES), F32),
                        pltpu.SemaphoreType.DMA((2, 2))],
        compiler_params=_params("arbitrary"),
        name="gather_attn_g%d" % g,
    )(q, kc, vc, kn, vn)


def _merge_kernel(h_ref, y_ref, o0_ref, o1_ref, o2_ref, gate_ref, wp_ref, wa_ref, wo_ref, out_ref,
                  oa_ref, mix_ref, tok_ref, *, dils):
    o_refs = (o0_ref, o1_ref, o2_ref)
    tm, d = out_ref.shape
    s = pl.program_id(0)

    @pl.when(s == 0)
    def _():
        oa_ref[1] = jnp.zeros(oa_ref.shape[1:], BF16)
        mix_ref[1] = jnp.zeros(mix_ref.shape[1:], BF16)

    def tile(gi, c):
        if dils[gi] == 1:
            return o_refs[gi][0, :, c * 128:(c + 1) * 128]
        return tok_ref[gi, c]

    def step(fill, drain):
        for gi, dil in enumerate(dils):
            if dil == 1:
                continue
            n = tm // dil
            for r in range(dil):
                for c in range(ATTN_OUT_TILES):
                    tok_ref[gi, c, pl.ds(r, n, stride=dil), :] = o_refs[gi][r, :, c * 128:(c + 1) * 128]
        lses = [tile(gi, HEADS) for gi in range(N_GROUPS)]
        mx = jnp.maximum(jnp.maximum(lses[0], lses[1]), lses[2])
        es = [jnp.exp(l - mx) for l in lses]
        inv = 1.0 / (es[0] + es[1] + es[2])
        wts = [e * inv for e in es]
        for hh in range(HEADS):
            acc = wts[0][:, hh:hh + 1] * tile(0, hh)
            for gi in range(1, N_GROUPS):
                acc = acc + wts[gi][:, hh:hh + 1] * tile(gi, hh)
            oa_ref[fill, :, _head_cols(hh)] = acc.astype(BF16)

        a_attn = jnp.dot(oa_ref[drain], wa_ref[...], preferred_element_type=F32)
        a_pool = jnp.dot(y_ref[...], wp_ref[...], preferred_element_type=F32)
        mix = gate_ref[:, :d].astype(F32) * a_pool + gate_ref[:, d:].astype(F32) * a_attn
        mix_ref[fill] = mix.astype(BF16)

        out_ref[...] = h_ref[...] + jnp.dot(mix_ref[drain], wo_ref[...], preferred_element_type=F32)

    pl.when(s % 2 == 0)(functools.partial(step, 0, 1))
    pl.when(s % 2 == 1)(functools.partial(step, 1, 0))


def _merge(h, y_pool, o_groups, gates, w_pool_out, w_attn_out, w_out, *, tm=256):
    m, d = h.shape
    dils = tuple(o.shape[1] for o in o_groups)
    seq = o_groups[0].shape[1] * o_groups[0].shape[2]
    tm = _row_tile(seq, tm)
    per_seq, n_tiles = seq // tm, m // tm
    last = n_tiles - 1
    row1 = lambda s: (jnp.clip(s - 1, 0, last), 0)
    row2 = lambda s: (jnp.maximum(s - 2, 0), 0)
    fixed = lambda s: (0, 0)

    def attn_tile(s):
        i = jnp.minimum(s, last)
        return (i // per_seq, 0, i % per_seq, 0)

    o_specs = [pl.BlockSpec((None, dil, tm // dil, ATTN_OUT_COLS), attn_tile) for dil in dils]
    return pl.pallas_call(
        functools.partial(_merge_kernel, dils=dils),
        grid=(n_tiles + 2,),
        in_specs=[
            pl.BlockSpec((tm, d), row2),
            pl.BlockSpec((tm, POOL_WIDTH), row1),
            *o_specs,
            pl.BlockSpec((tm, 2 * d), row1),
            _resident(w_pool_out.shape, fixed),
            _resident(w_attn_out.shape, fixed),
            _resident(w_out.shape, fixed),
        ],
        out_specs=pl.BlockSpec((tm, d), row2),
        out_shape=jax.ShapeDtypeStruct((m, d), F32),
        scratch_shapes=[pltpu.VMEM((2, tm, GROUP_WIDTH), BF16),
                        pltpu.VMEM((2, tm, d), BF16),
                        pltpu.VMEM((N_GROUPS, ATTN_OUT_TILES, tm, 128), F32)],
        compiler_params=_params("arbitrary"),
        name="merge",
    )(h, y_pool, *o_groups, gates, w_pool_out, w_attn_out, w_out)


def _ple_kernel(h_ref, p_ref, g_ref, wg_ref, wp_ref, out_ref):
    h = h_ref[...]
    hn = _rms(h, g_ref[...]).astype(BF16)
    gate = jax.nn.sigmoid(jnp.dot(hn, wg_ref[...], preferred_element_type=F32))
    emb = jnp.dot(p_ref[...].astype(BF16), wp_ref[...], preferred_element_type=F32)
    out_ref[...] = h + gate * emb


def _ple(h, p, g, w_gate, w_proj, *, tm=1024):
    m, d = h.shape
    tm = _row_tile(m, tm)
    row = lambda i: (i, 0)
    fixed = lambda i: (0, 0)
    return pl.pallas_call(
        _ple_kernel,
        grid=(m // tm,),
        in_specs=[
            pl.BlockSpec((tm, d), row),
            pl.BlockSpec((tm, p.shape[1]), row),
            _resident((1, d), fixed),
            _resident(w_gate.shape, fixed),
            _resident(w_proj.shape, fixed),
        ],
        out_specs=pl.BlockSpec((tm, d), row),
        out_shape=jax.ShapeDtypeStruct((m, d), F32),
        compiler_params=_params("parallel"),
        name="ple",
    )(h, p, g, w_gate, w_proj)


def _layer(x, p, w, *, prompt, pool_prev=None, caches=None):
    b, t, d = x.shape
    m = b * t
    h1, u = _ffn(x.reshape(m, d), w["ffn1_norm"], w["ffn1_w_in"], w["ffn1_w_out"], w["mix_norm"])
    outs_a = _proj_a(u, w["w_in"], w["q_norm"], batch=b, seq=t, prompt=prompt)
    outs_b = _proj_b(u, w["w_in"], w["k_norm"], batch=b, seq=t, prompt=prompt)
    z, gates = outs_a[0], outs_a[-1]
    z3 = z.reshape(b, t, POOL_WIDTH)
    if prompt:
        qs = outs_a[1:1 + N_GROUPS]
        kbs, vbs = outs_b[0:N_GROUPS], outs_b[N_GROUPS:2 * N_GROUPS]
        kfs, vfs = outs_b[2 * N_GROUPS:3 * N_GROUPS], outs_b[3 * N_GROUPS:]
        y_pool = _pool(z3, z3, w["pool_map"], w["pool_scale"], tp=512, pos0=0, zero_first_halo=True)
        new_pool = z3[:, t - POOL_STATE:]
        o_groups = [_band_attn(qs[g], kbs[g], vbs[g], g) for g in range(N_GROUPS)]
        new_kv = []
        for g in range(N_GROUPS):
            new_kv += [kfs[g].reshape(b, -1, HEADS, HEAD_DIM), vfs[g].reshape(b, -1, HEADS, HEAD_DIM)]
    else:
        qf = outs_a[1].reshape(N_GROUPS, b, t, HEADS, HEAD_DIM)
        kn = outs_b[0].reshape(N_GROUPS, b, t, HEADS, HEAD_DIM)
        vn = outs_b[1].reshape(N_GROUPS, b, t, HEADS, HEAD_DIM)
        halo = jnp.pad(pool_prev, ((0, 0), (POOL_HALO - POOL_STATE, 0), (0, 0)))
        y_pool = _pool(z3, halo, w["pool_map"], w["pool_scale"], tp=t, pos0=PAST_LEN, zero_first_halo=False)
        new_pool = jnp.concatenate([pool_prev, z3], axis=1)[:, t:]
        o_groups, new_kv = [], []
        for g in range(N_GROUPS):
            kc, vc = caches[2 * g], caches[2 * g + 1]
            o, k_new, v_new = _gather_attn(qf, kc, vc, kn, vn, g)
            o_groups.append(o.reshape(1, 1, m, ATTN_OUT_COLS))
            new_kv += [k_new, v_new]
    h2 = _merge(h1, y_pool.reshape(m, POOL_WIDTH), o_groups, gates,
                w["w_pool_out"], w["w_attn_out"], w["w_out"])
    h3 = _ffn(h2, w["ffn2_norm"], w["ffn2_w_in"], w["ffn2_w_out"])
    h4 = _ple(h3, p.reshape(m, -1), w["ple_norm"], w["ple_gate"], w["ple_proj"])
    return h4.reshape(b, t, d), new_pool, new_kv


def kernel(x_prompt, x_sample, p_prompt, p_sample, state_pool, cache_k0, cache_v0, cache_k1, cache_v1, cache_k2, cache_v2, ffn1_norm, ffn1_w_in, ffn1_w_out, mix_norm, w_in, q_norm, k_norm, pool_map, pool_scale, w_pool_out, w_attn_out, w_out, ffn2_norm, ffn2_w_in, ffn2_w_out, ple_norm, ple_gate, ple_proj):
    caches = (cache_k0, cache_v0, cache_k1, cache_v1, cache_k2, cache_v2)
    depth = ffn1_norm.shape[0]
    hp, hs = x_prompt, x_sample
    pool_p, pool_s, kv_p, kv_s = [], [], [], []
    for i in range(depth):
        row = lambda a: a[i][None, :]
        w = dict(
            ffn1_norm=row(ffn1_norm), ffn1_w_in=ffn1_w_in[i].astype(BF16), ffn1_w_out=ffn1_w_out[i].astype(BF16),
            mix_norm=row(mix_norm), q_norm=row(q_norm), k_norm=row(k_norm),
            w_in=w_in[i].astype(BF16),
            pool_map=pool_map[i].astype(BF16), pool_scale=row(pool_scale),
            w_pool_out=w_pool_out[i].astype(BF16), w_attn_out=w_attn_out[i].astype(BF16),
            w_out=w_out[i].astype(BF16),
            ffn2_norm=row(ffn2_norm), ffn2_w_in=ffn2_w_in[i].astype(BF16), ffn2_w_out=ffn2_w_out[i].astype(BF16),
            ple_norm=row(ple_norm), ple_gate=ple_gate[i].astype(BF16), ple_proj=ple_proj[i].astype(BF16),
        )
        hp, pp, kvp = _layer(hp, p_prompt[i], w, prompt=True)
        hs, ps, kvs = _layer(hs, p_sample[i], w, prompt=False, pool_prev=state_pool[i],
                             caches=tuple(c[i] for c in caches))
        pool_p.append(pp)
        pool_s.append(ps)
        kv_p.append(kvp)
        kv_s.append(kvs)
    stack = lambda rows: jnp.stack(rows)
    kp = [stack([r[j] for r in kv_p]) for j in range(2 * N_GROUPS)]
    ks = [stack([r[j] for r in kv_s]) for j in range(2 * N_GROUPS)]
    return (hp, hs, stack(pool_p), *kp, stack(pool_s), *ks)
```

```python
import functools
import math

import jax
import jax.numpy as jnp
from jax import lax
from jax.experimental import pallas as pl
from jax.experimental.pallas import tpu as pltpu

F32 = jnp.float32
BF16 = jnp.bfloat16

POOL_WINDOWS = (2, 4, 8, 16)
POOL_GROUP_WIDTH = 256
POOL_WIDTH = POOL_GROUP_WIDTH * len(POOL_WINDOWS)
POOL_STATE = max(POOL_WINDOWS) - 1
POOL_HALO = 16
POOL_PAD = 8
HEAD_DIM = 128
HEADS = 8
GROUP_WIDTH = HEADS * HEAD_DIM
ATTN_GROUPS = ((128, 1), (512, 4), (2048, 16))
DILS = tuple(d for _, d in ATTN_GROUPS)
N_GROUPS = len(ATTN_GROUPS)
REORDER_STRIDE = 4
REORDER_PASSES = tuple(round(math.log(d, REORDER_STRIDE)) for d in DILS)
SLAB_OF = tuple(sum(REORDER_PASSES[:g]) for g in range(N_GROUPS))
N_SLABS = sum(REORDER_PASSES)
BAND = 128
LSE_LANES = 128
ATTN_OUT_COLS = GROUP_WIDTH + LSE_LANES
ATTN_OUT_TILES = ATTN_OUT_COLS // 128
RMS_EPS = 1e-6
NEG_INF = -1e30
LOG2_E = 1.4426950408889634
LN_2 = 0.6931471805599453
Q_SCALE = HEAD_DIM ** -0.5 * LOG2_E
PAST_LEN = 16384

VMEM_LIMIT = 60 * 1024 * 1024


def _params(*semantics):
    return pltpu.CompilerParams(dimension_semantics=semantics, vmem_limit_bytes=VMEM_LIMIT)


def _resident(shape, index_map):
    return pl.BlockSpec(shape, index_map, pipeline_mode=pl.Buffered(1))


def _rms(x, g):
    return x * lax.rsqrt(jnp.mean(x * x, axis=-1, keepdims=True) + RMS_EPS) * g


def _row_tile(m, want):
    t = min(m, want)
    assert m % t == 0, (m, t)
    return t


def _head_cols(hh):
    return slice(hh * HEAD_DIM, (hh + 1) * HEAD_DIM)


def _ffn_kernel(*refs, emit_norm):
    if emit_norm:
        x_ref, g_ref, wa_ref, wb_ref, wo_ref, gn_ref, o_ref, u_ref = refs
        xn_ref = u_ref
    else:
        x_ref, g_ref, wa_ref, wb_ref, wo_ref, o_ref, xn_ref = refs
    k = pl.program_id(1)

    @pl.when(k == 0)
    def _():
        x = x_ref[...]
        xn_ref[...] = _rms(x, g_ref[...]).astype(BF16)
        o_ref[...] = x

    xn = xn_ref[...]
    a = jnp.dot(xn, wa_ref[...], preferred_element_type=F32)
    b = jnp.dot(xn, wb_ref[...], preferred_element_type=F32)
    hid = (0.5 * (a * jax.nn.sigmoid(a)) * b).astype(BF16)
    o_ref[...] += jnp.dot(hid, wo_ref[...], preferred_element_type=F32)

    if emit_norm:
        @pl.when(k == pl.num_programs(1) - 1)
        def _():
            u_ref[...] = _rms(o_ref[...], gn_ref[...]).astype(BF16)


def _ffn(x, g, w_in, w_out, g_next=None, *, tm=1024, tf=512):
    m, d = x.shape
    d_ff = w_out.shape[0]
    tm = _row_tile(m, tm)
    tf = _row_tile(d_ff, tf)
    nk = d_ff // tf
    emit_norm = g_next is not None
    row = pl.BlockSpec((tm, d), lambda i, k: (i, 0))
    vec = _resident((1, d), lambda i, k: (0, 0))
    in_specs = [row, vec,
                pl.BlockSpec((d, tf), lambda i, k: (0, k)),
                pl.BlockSpec((d, tf), lambda i, k: (0, k + nk)),
                pl.BlockSpec((tf, d), lambda i, k: (k, 0))]
    args = [x, g, w_in, w_in, w_out]
    out_specs, out_shape = row, jax.ShapeDtypeStruct((m, d), F32)
    if emit_norm:
        in_specs.append(vec)
        args.append(g_next)
        out_specs = [row, row]
        out_shape = [out_shape, jax.ShapeDtypeStruct((m, d), BF16)]
    return pl.pallas_call(
        functools.partial(_ffn_kernel, emit_norm=emit_norm),
        grid=(m // tm, nk),
        in_specs=in_specs,
        out_specs=out_specs,
        out_shape=out_shape,
        scratch_shapes=[] if emit_norm else [pltpu.VMEM((tm, d), BF16)],
        compiler_params=_params("parallel", "arbitrary"),
        name="ffn",
    )(*args)


MXU_COLS = 512


def _tile_heads(u_ref, w_ref):
    heads = []
    for c in range(0, GROUP_WIDTH, MXU_COLS):
        part = jnp.dot(u_ref[...], w_ref[:, c:c + MXU_COLS], preferred_element_type=F32)
        heads += [part[:, i:i + HEAD_DIM] for i in range(0, MXU_COLS, HEAD_DIM)]
    return heads


def _store_cols(ref, heads, dtype):
    for hh, x in enumerate(heads):
        ref[:, _head_cols(hh)] = x.astype(dtype)


def _store_head_rows(ref, heads):
    for hh, x in enumerate(heads):
        ref[pl.ds(hh, x.shape[0], stride=HEADS), :] = x


def _store_residue_major(ref, slab_ref, heads, dil):
    if dil == 1:
        _store_cols(ref.at[0], heads, BF16)
        return
    passes = slab_ref.shape[0]
    assert REORDER_STRIDE ** passes == dil
    for hh, x in enumerate(heads):
        slab_ref[0, hh] = x
    chunk, residues = heads[0].shape[0], [0]
    for p in range(passes):
        sub, weight = chunk // REORDER_STRIDE, REORDER_STRIDE ** p
        for c, res in enumerate(residues):
            for r in range(REORDER_STRIDE):
                for hh in range(HEADS):
                    rows = slab_ref[p, hh, pl.ds(c * chunk + r, sub, stride=REORDER_STRIDE), :]
                    if p + 1 < passes:
                        slab_ref[p + 1, hh, pl.ds(c * chunk + r * sub, sub), :] = rows
                    else:
                        ref[res + r * weight, :, _head_cols(hh)] = rows.astype(BF16)
        residues = [res + r * weight for res in residues for r in range(REORDER_STRIDE)]
        chunk = sub


_HEAVY_FIRST = tuple(sorted(range(N_GROUPS), key=lambda g: -DILS[g]))


def _proj_a_kernel(*refs, prompt, n_gate):
    u_ref, wz_ref, *refs = refs
    wq_refs, wg_refs, refs = refs[:N_GROUPS], refs[N_GROUPS:N_GROUPS + n_gate], refs[N_GROUPS + n_gate:]
    if prompt:
        qn_ref, z_ref, q0_ref, q1_ref, q2_ref, gate_ref, slab_ref = refs
        q_refs = (q0_ref, q1_ref, q2_ref)
    else:
        qn_ref, z_ref, q_ref, gate_ref = refs
    for g in _HEAVY_FIRST:
        q = [_rms(x, qn_ref[...]) * Q_SCALE for x in _tile_heads(u_ref, wq_refs[g])]
        if prompt:
            _store_residue_major(q_refs[g], slab_ref.at[pl.ds(SLAB_OF[g], REORDER_PASSES[g])], q, DILS[g])
        else:
            _store_head_rows(q_ref.at[g], q)
    for t, wg_ref in enumerate(wg_refs):
        gate = [jax.nn.sigmoid(x) for x in _tile_heads(u_ref, wg_ref)]
        _store_cols(gate_ref.at[:, t * GROUP_WIDTH:(t + 1) * GROUP_WIDTH], gate, BF16)
    _store_cols(z_ref, _tile_heads(u_ref, wz_ref), F32)


def _proj_b_kernel(*refs, prompt):
    u_ref, *refs = refs
    wk_refs, wv_refs, refs = refs[:N_GROUPS], refs[N_GROUPS:2 * N_GROUPS], refs[2 * N_GROUPS:]
    if prompt:
        (kn_ref, kb0_ref, kb1_ref, kb2_ref, vb0_ref, vb1_ref, vb2_ref,
         kf0_ref, kf1_ref, kf2_ref, vf0_ref, vf1_ref, vf2_ref, slab_ref) = refs
        kb_refs, vb_refs = (kb0_ref, kb1_ref, kb2_ref), (vb0_ref, vb1_ref, vb2_ref)
        kf_refs, vf_refs = (kf0_ref, kf1_ref, kf2_ref), (vf0_ref, vf1_ref, vf2_ref)
    else:
        kn_ref, kf_ref, vf_ref = refs
        kf_refs = [kf_ref.at[g] for g in range(N_GROUPS)]
        vf_refs = [vf_ref.at[g] for g in range(N_GROUPS)]
    tm = u_ref.shape[0]
    for g in _HEAVY_FIRST:
        k = [_rms(x, kn_ref[...]) for x in _tile_heads(u_ref, wk_refs[g])]
        v = _tile_heads(u_ref, wv_refs[g])
        kept = kf_refs[g].shape[0] // HEADS
        _store_head_rows(kf_refs[g], [x[tm - kept:] for x in k])
        _store_head_rows(vf_refs[g], [x[tm - kept:] for x in v])
        if prompt:
            slabs = pl.ds(SLAB_OF[g], REORDER_PASSES[g])
            _store_residue_major(kb_refs[g], slab_ref.at[0, slabs], k, DILS[g])
            _store_residue_major(vb_refs[g], slab_ref.at[1, slabs], v, DILS[g])


def _residue_specs(batch, seq, tm, dtype):
    per_seq = seq // tm
    specs, shapes = [], []
    for dil in DILS:
        specs.append(pl.BlockSpec((None, dil, tm // dil, GROUP_WIDTH),
                                  lambda i: (i // per_seq, 0, i % per_seq, 0)))
        shapes.append(jax.ShapeDtypeStruct((batch, dil, seq // dil, GROUP_WIDTH), dtype))
    return specs, shapes


def _head_rows_spec(n_arrays, m, tm):
    return (pl.BlockSpec((n_arrays, tm * HEADS, HEAD_DIM), lambda i: (0, i, 0)),
            jax.ShapeDtypeStruct((n_arrays, m * HEADS, HEAD_DIM), F32))


def _suffix_head_rows_spec(batch, seq, tm, keep):
    rows = min(keep, tm)
    assert keep % rows == 0
    blocks, per_seq = keep // rows, seq // tm
    spec = pl.BlockSpec((None, None, rows * HEADS, HEAD_DIM),
                        lambda i: (i // per_seq, jnp.maximum(i % per_seq - (per_seq - blocks), 0), 0, 0))
    return spec, jax.ShapeDtypeStruct((batch, blocks, rows * HEADS, HEAD_DIM), F32)


def _weight_tiles(w, tiles):
    d = w.shape[0]
    return [_resident((d, GROUP_WIDTH), functools.partial(lambda t, i: (0, t), t)) for t in tiles]


def _proj_a(u, w, qn, *, batch, seq, prompt, tm=256):
    m, d = u.shape
    gate0 = 1 + 3 * N_GROUPS
    tiles = [0, *range(1, 1 + N_GROUPS), *range(gate0, w.shape[1] // GROUP_WIDTH)]
    n_gate = len(tiles) - 1 - N_GROUPS
    tm = _row_tile(seq if prompt else m, tm)
    fixed = lambda i: (0, 0)
    rows = lambda width: pl.BlockSpec((tm, width), lambda i: (i, 0))
    scratch = []
    if prompt:
        q_specs, q_shapes = _residue_specs(batch, seq, tm, BF16)
        scratch = [pltpu.VMEM((N_SLABS, HEADS, tm, HEAD_DIM), F32)]
    else:
        spec, shape = _head_rows_spec(N_GROUPS, m, tm)
        q_specs, q_shapes = [spec], [shape]
    return pl.pallas_call(
        functools.partial(_proj_a_kernel, prompt=prompt, n_gate=n_gate),
        grid=(m // tm,),
        in_specs=[rows(d), *_weight_tiles(w, tiles), _resident((1, HEAD_DIM), fixed)],
        out_specs=[rows(GROUP_WIDTH), *q_specs, rows(n_gate * GROUP_WIDTH)],
        out_shape=[jax.ShapeDtypeStruct((m, GROUP_WIDTH), F32), *q_shapes,
                   jax.ShapeDtypeStruct((m, n_gate * GROUP_WIDTH), BF16)],
        scratch_shapes=scratch,
        compiler_params=_params("parallel"),
        name="proj_a",
    )(u, *[w] * len(tiles), qn)


def _proj_b(u, w, kn, *, batch, seq, prompt, tm=256):
    m, d = u.shape
    tiles = range(1 + N_GROUPS, 1 + 3 * N_GROUPS)
    tm = _row_tile(seq if prompt else m, tm)
    fixed = lambda i: (0, 0)
    scratch = []
    if prompt:
        kb_specs, kb_shapes = _residue_specs(batch, seq, tm, BF16)
        kf = [_suffix_head_rows_spec(batch, seq, tm, min(window, seq)) for window, _ in ATTN_GROUPS]
        kf_specs, kf_shapes = [s for s, _ in kf], [s for _, s in kf]
        out_specs = [*kb_specs, *kb_specs, *kf_specs, *kf_specs]
        out_shape = [*kb_shapes, *kb_shapes, *kf_shapes, *kf_shapes]
        scratch = [pltpu.VMEM((2, N_SLABS, HEADS, tm, HEAD_DIM), F32)]
    else:
        spec, shape = _head_rows_spec(N_GROUPS, m, tm)
        out_specs, out_shape = [spec, spec], [shape, shape]
    return pl.pallas_call(
        functools.partial(_proj_b_kernel, prompt=prompt),
        grid=(m // tm,),
        in_specs=[pl.BlockSpec((tm, d), lambda i: (i, 0)), *_weight_tiles(w, tiles),
                  _resident((1, HEAD_DIM), fixed)],
        out_specs=out_specs,
        out_shape=out_shape,
        scratch_shapes=scratch,
        compiler_params=_params("arbitrary"),
        name="proj_b",
    )(u, *[w] * len(tiles), kn)


def _pool_kernel(z_ref, halo_ref, pm_ref, ps_ref, y_ref, ext_ref, lvl_ref, *, tp, pos0, zero_first_halo):
    i = pl.program_id(1)
    halo = halo_ref[...]
    if zero_first_halo:
        halo = jnp.where(i == 0, 0.0, halo)
    top, n = POOL_PAD + POOL_HALO, POOL_HALO + tp
    pad = jnp.zeros((POOL_PAD, ext_ref.shape[1]), F32)
    ext_ref[0:POOL_PAD, :] = pad
    ext_ref[POOL_PAD:top, :] = halo
    ext_ref[top:, :] = z_ref[...]
    lvl_ref[0, 0:POOL_PAD, :] = pad
    lvl_ref[1, 0:POOL_PAD, :] = pad
    pos = pos0 + i * tp + lax.broadcasted_iota(jnp.int32, (tp, 1), 0)
    for gi, w in enumerate(POOL_WINDOWS):
        cols = slice(gi * POOL_GROUP_WIDTH, (gi + 1) * POOL_GROUP_WIDTH)
        zc = ext_ref[pl.ds(top, tp), cols]
        src, slot, span = ext_ref, 0, 1
        while span < w:
            both = src[pl.ds(POOL_PAD, n), cols] + src[pl.ds(POOL_PAD - span, n), cols]
            span *= 2
            if span < w:
                lvl_ref[slot, pl.ds(POOL_PAD, n), cols] = both
                src, slot = lvl_ref.at[slot], 1 - slot
        wsum = both[POOL_HALO:]
        cnt = jnp.minimum(w, pos + 1).astype(F32)
        pooled = (wsum / cnt - zc).astype(BF16)
        y = jnp.dot(pooled, pm_ref[gi], preferred_element_type=F32) * ps_ref[:, cols]
        y_ref[:, cols] = y.astype(BF16)


def _pool(z, halo, pool_map, pool_scale, *, tp, pos0, zero_first_halo):
    b, t, p = z.shape
    tp = _row_tile(t, tp)
    hb = tp // POOL_HALO
    if halo is z:
        halo_map = lambda bi, i: (bi, jnp.maximum(i * hb - 1, 0), 0)
    else:
        assert t == tp and halo.shape == (b, POOL_HALO, p)
        halo_map = lambda bi, i: (bi, 0, 0)
    return pl.pallas_call(
        functools.partial(_pool_kernel, tp=tp, pos0=pos0, zero_first_halo=zero_first_halo),
        grid=(b, t // tp),
        in_specs=[
            pl.BlockSpec((None, tp, p), lambda bi, i: (bi, i, 0)),
            pl.BlockSpec((None, POOL_HALO, p), halo_map),
            _resident(pool_map.shape, lambda bi, i: (0, 0, 0)),
            _resident((1, p), lambda bi, i: (0, 0)),
        ],
        out_specs=pl.BlockSpec((None, tp, p), lambda bi, i: (bi, i, 0)),
        out_shape=jax.ShapeDtypeStruct((b, t, p), BF16),
        scratch_shapes=[pltpu.VMEM((POOL_PAD + POOL_HALO + tp, p), F32),
                        pltpu.VMEM((2, POOL_PAD + POOL_HALO + tp, p), F32)],
        compiler_params=_params("parallel", "arbitrary"),
        name="pool",
    )(z, halo, pool_map, pool_scale)


def _band_attn_kernel(q_ref, k_ref, v_ref, o_ref, *, nblk, kw, dil):
    rb = q_ref.shape[0]
    i = pl.program_id(2)

    def body(it, carry):
        rl, nl = it // nblk, it % nblk
        n = i * nblk + nl
        if kw == BAND:
            kstart, off = 0, 0
        else:
            kstart = pl.multiple_of(jnp.maximum(n - 1, 0) * BAND, BAND)
            off = n * BAND - kstart
        qstart = pl.multiple_of(nl * BAND, BAND)
        steps = (off + lax.broadcasted_iota(jnp.int32, (BAND, kw), 0)
                 - lax.broadcasted_iota(jnp.int32, (BAND, kw), 1))
        valid = jnp.abs(steps - BAND // 2) <= BAND // 2
        neg_dist = jnp.where(valid, (steps * dil).astype(F32) * -LOG2_E, NEG_INF)
        lane = lax.broadcasted_iota(jnp.int32, (BAND, LSE_LANES), 1)
        lse_tile = jnp.zeros((BAND, LSE_LANES), F32)
        scores = []
        for hh in range(HEADS):
            qh = q_ref[rl, pl.ds(qstart, BAND), _head_cols(hh)]
            kh = k_ref[rl, pl.ds(kstart, kw), _head_cols(hh)]
            scores.append(lax.dot_general(qh, kh, (((1,), (1,)), ((), ())), preferred_element_type=F32))
        probs, dens = [], []
        for hh in range(HEADS):
            logits = scores[hh] + (2.0 ** -(hh + 1)) * neg_dist
            mx = jnp.max(logits, axis=-1, keepdims=True)
            p = jnp.exp2(logits - mx)
            den = jnp.sum(p, axis=-1, keepdims=True)
            probs.append(p.astype(BF16))
            dens.append(den)
            lse_tile = jnp.where(lane == hh, (mx + jnp.log2(den)) * LN_2, lse_tile)
        for hh in range(HEADS):
            vh = v_ref[rl, pl.ds(kstart, kw), _head_cols(hh)]
            o = jnp.dot(probs[hh], vh, preferred_element_type=F32) / dens[hh]
            o_ref[rl, pl.ds(qstart, BAND), _head_cols(hh)] = o
        o_ref[rl, pl.ds(qstart, BAND), GROUP_WIDTH:] = lse_tile
        return carry

    lax.fori_loop(0, rb * nblk, body, 0, unroll=True)


def _band_attn(q, k, v, g, *, blocks_per_step=8):
    batch, dil, mr, _ = q.shape
    assert dil == DILS[g] and mr % BAND == 0
    tq = min(mr, blocks_per_step * BAND)
    nblk = tq // BAND
    rb = min(dil, blocks_per_step // nblk)
    kw = min(mr, 2 * BAND)
    return pl.pallas_call(
        functools.partial(_band_attn_kernel, nblk=nblk, kw=kw, dil=dil),
        grid=(batch, dil // rb, mr // tq),
        in_specs=[
            pl.BlockSpec((None, rb, tq, GROUP_WIDTH), lambda b, r, i: (b, r, i, 0)),
            pl.BlockSpec((None, rb, mr, GROUP_WIDTH), lambda b, r, i: (b, r, 0, 0)),
            pl.BlockSpec((None, rb, mr, GROUP_WIDTH), lambda b, r, i: (b, r, 0, 0)),
        ],
        out_specs=pl.BlockSpec((None, rb, tq, ATTN_OUT_COLS), lambda b, r, i: (b, r, i, 0)),
        out_shape=jax.ShapeDtypeStruct((batch, dil, mr, ATTN_OUT_COLS), F32),
        compiler_params=_params("parallel", "parallel", "arbitrary"),
        name="band_attn_g%d" % g,
    )(q, k, v)


def _roll_copies(b, cache_ref, new_ref, out_hbm, sems, which):
    length, t_new = cache_ref.shape[1], new_ref.shape[2]
    kept = length - t_new
    return (pltpu.make_async_copy(cache_ref.at[0, pl.ds(t_new, kept)],
                                  out_hbm.at[b, pl.ds(0, kept)], sems.at[which, 0]),
            pltpu.make_async_copy(new_ref.at[0, 0], out_hbm.at[b, pl.ds(kept, t_new)], sems.at[which, 1]))


def _gather_attn_kernel(q_ref, kc_ref, vc_ref, kn_ref, vn_ref, o_ref, ko_hbm, vo_hbm, oh_ref, lh_ref, sems,
                        *, dil, t_new):
    b = pl.program_id(0)
    copies = (*_roll_copies(b, kc_ref, kn_ref, ko_hbm, sems, 0),
              *_roll_copies(b, vc_ref, vn_ref, vo_hbm, sems, 1))
    for copy in copies:
        copy.start()
    length = kc_ref.shape[1]
    head = lax.broadcasted_iota(jnp.int32, (HEADS, 1), 0)
    slope = lax.bitcast_convert_type((126 - head) << 23, F32) * LOG2_E
    for i in range(t_new):
        qi = q_ref[0, 0, i]
        n_new = i // dil + 1
        n_old = BAND + 1 - n_new
        first = length + i - BAND * dil
        assert first >= 0 and first + (n_old - 1) * dil < length
        kc = kc_ref[0, pl.ds(first, n_old, stride=dil)]
        vc = vc_ref[0, pl.ds(first, n_old, stride=dil)]
        steps = BAND - lax.broadcasted_iota(jnp.int32, (n_old, 1, 1), 0)
        sc = jnp.sum(kc * qi[None], axis=-1, keepdims=True) - slope[None] * (steps * dil).astype(F32)
        sn = [jnp.sum(kn_ref[0, 0, i - jj * dil] * qi, axis=-1, keepdims=True) - slope * float(jj * dil)
              for jj in range(n_new)]
        mx = jnp.max(sc, axis=0)
        for s in sn:
            mx = jnp.maximum(mx, s)
        pc = jnp.exp2(sc - mx[None])
        den = jnp.sum(pc, axis=0)
        acc = jnp.sum(pc * vc, axis=0)
        for jj, s in enumerate(sn):
            pn = jnp.exp2(s - mx)
            den = den + pn
            acc = acc + pn * vn_ref[0, 0, i - jj * dil]
        oh_ref[i] = acc / den
        lh_ref[i] = jnp.broadcast_to((mx + jnp.log2(den)) * LN_2, (HEADS, LSE_LANES))
    lane = lax.broadcasted_iota(jnp.int32, (t_new, LSE_LANES), 1)
    lse_tile = jnp.zeros((t_new, LSE_LANES), F32)
    for hh in range(HEADS):
        o_ref[:, _head_cols(hh)] = oh_ref[:, hh, :]
        lse_tile = jnp.where(lane == hh, lh_ref[:, hh, :], lse_tile)
    o_ref[:, GROUP_WIDTH:] = lse_tile
    for copy in copies:
        copy.wait()


def _gather_attn(q, kc, vc, kn, vn, g):
    win, dil = ATTN_GROUPS[g]
    b, length = kc.shape[:2]
    t_new = q.shape[2]
    assert length == win and t_new <= dil * BAND
    new = pl.BlockSpec((1, 1, t_new, HEADS, HEAD_DIM), lambda bi: (g, bi, 0, 0, 0))
    cache = pl.BlockSpec((1, length, HEADS, HEAD_DIM), lambda bi: (bi, 0, 0, 0))
    whole_hbm = pl.BlockSpec(memory_space=pl.ANY)
    return pl.pallas_call(
        functools.partial(_gather_attn_kernel, dil=dil, t_new=t_new),
        grid=(b,),
        in_specs=[new, cache, cache, new, new],
        out_specs=[pl.BlockSpec((None, t_new, ATTN_OUT_COLS), lambda bi: (bi, 0, 0)), whole_hbm, whole_hbm],
        out_shape=[jax.ShapeDtypeStruct((b, t_new, ATTN_OUT_COLS), F32),
                   jax.ShapeDtypeStruct(kc.shape, F32), jax.ShapeDtypeStruct(vc.shape, F32)],
        scratch_shapes=[pltpu.VMEM((t_new, HEADS, HEAD_DIM), F32),
                        pltpu.VMEM((t_new, HEADS, LSE_LANES), F32),
                        pltpu.SemaphoreType.DMA((2, 2))],
        compiler_params=_params("arbitrary"),
        name="gather_attn_g%d" % g,
    )(q, kc, vc, kn, vn)


def _merge_kernel(h_ref, y_ref, o0_ref, o1_ref, o2_ref, gate_ref, wp_ref, wa_ref, wo_ref, out_ref,
                  oa_ref, mix_ref, tok_ref, *, dils):
    o_refs = (o0_ref, o1_ref, o2_ref)
    tm, d = out_ref.shape
    s = pl.program_id(0)

    @pl.when(s == 0)
    def _():
        oa_ref[1] = jnp.zeros(oa_ref.shape[1:], BF16)
        mix_ref[1] = jnp.zeros(mix_ref.shape[1:], BF16)

    def tile(gi, c):
        if dils[gi] == 1:
            return o_refs[gi][0, :, c * 128:(c + 1) * 128]
        return tok_ref[gi, c]

    def step(fill, drain):
        for gi, dil in enumerate(dils):
            if dil == 1:
                continue
            n = tm // dil
            for r in range(dil):
                for c in range(ATTN_OUT_TILES):
                    tok_ref[gi, c, pl.ds(r, n, stride=dil), :] = o_refs[gi][r, :, c * 128:(c + 1) * 128]
        lses = [tile(gi, HEADS) for gi in range(N_GROUPS)]
        mx = jnp.maximum(jnp.maximum(lses[0], lses[1]), lses[2])
        es = [jnp.exp(l - mx) for l in lses]
        inv = 1.0 / (es[0] + es[1] + es[2])
        wts = [e * inv for e in es]
        for hh in range(HEADS):
            acc = wts[0][:, hh:hh + 1] * tile(0, hh)
            for gi in range(1, N_GROUPS):
                acc = acc + wts[gi][:, hh:hh + 1] * tile(gi, hh)
            oa_ref[fill, :, _head_cols(hh)] = acc.astype(BF16)

        a_attn = jnp.dot(oa_ref[drain], wa_ref[...], preferred_element_type=F32)
        a_pool = jnp.dot(y_ref[...], wp_ref[...], preferred_element_type=F32)
        mix = gate_ref[:, :d].astype(F32) * a_pool + gate_ref[:, d:].astype(F32) * a_attn
        mix_ref[fill] = mix.astype(BF16)

        out_ref[...] = h_ref[...] + jnp.dot(mix_ref[drain], wo_ref[...], preferred_element_type=F32)

    pl.when(s % 2 == 0)(functools.partial(step, 0, 1))
    pl.when(s % 2 == 1)(functools.partial(step, 1, 0))


def _merge(h, y_pool, o_groups, gates, w_pool_out, w_attn_out, w_out, *, tm=256):
    m, d = h.shape
    dils = tuple(o.shape[1] for o in o_groups)
    seq = o_groups[0].shape[1] * o_groups[0].shape[2]
    tm = _row_tile(seq, tm)
    per_seq, n_tiles = seq // tm, m // tm
    last = n_tiles - 1
    row1 = lambda s: (jnp.clip(s - 1, 0, last), 0)
    row2 = lambda s: (jnp.maximum(s - 2, 0), 0)
    fixed = lambda s: (0, 0)

    def attn_tile(s):
        i = jnp.minimum(s, last)
        return (i // per_seq, 0, i % per_seq, 0)

    o_specs = [pl.BlockSpec((None, dil, tm // dil, ATTN_OUT_COLS), attn_tile) for dil in dils]
    return pl.pallas_call(
        functools.partial(_merge_kernel, dils=dils),
        grid=(n_tiles + 2,),
        in_specs=[
            pl.BlockSpec((tm, d), row2),
            pl.BlockSpec((tm, POOL_WIDTH), row1),
            *o_specs,
            pl.BlockSpec((tm, 2 * d), row1),
            _resident(w_pool_out.shape, fixed),
            _resident(w_attn_out.shape, fixed),
            _resident(w_out.shape, fixed),
        ],
        out_specs=pl.BlockSpec((tm, d), row2),
        out_shape=jax.ShapeDtypeStruct((m, d), F32),
        scratch_shapes=[pltpu.VMEM((2, tm, GROUP_WIDTH), BF16),
                        pltpu.VMEM((2, tm, d), BF16),
                        pltpu.VMEM((N_GROUPS, ATTN_OUT_TILES, tm, 128), F32)],
        compiler_params=_params("arbitrary"),
        name="merge",
    )(h, y_pool, *o_groups, gates, w_pool_out, w_attn_out, w_out)


def _ple_kernel(h_ref, p_ref, g_ref, wg_ref, wp_ref, out_ref):
    h = h_ref[...]
    hn = _rms(h, g_ref[...]).astype(BF16)
    gate = jax.nn.sigmoid(jnp.dot(hn, wg_ref[...], preferred_element_type=F32))
    emb = jnp.dot(p_ref[...].astype(BF16), wp_ref[...], preferred_element_type=F32)
    out_ref[...] = h + gate * emb


def _ple(h, p, g, w_gate, w_proj, *, tm=1024):
    m, d = h.shape
    tm = _row_tile(m, tm)
    row = lambda i: (i, 0)
    fixed = lambda i: (0, 0)
    return pl.pallas_call(
        _ple_kernel,
        grid=(m // tm,),
        in_specs=[
            pl.BlockSpec((tm, d), row),
            pl.BlockSpec((tm, p.shape[1]), row),
            _resident((1, d), fixed),
            _resident(w_gate.shape, fixed),
            _resident(w_proj.shape, fixed),
        ],
        out_specs=pl.BlockSpec((tm, d), row),
        out_shape=jax.ShapeDtypeStruct((m, d), F32),
        compiler_params=_params("parallel"),
        name="ple",
    )(h, p, g, w_gate, w_proj)


def _layer(x, p, w, *, prompt, pool_prev=None, caches=None):
    b, t, d = x.shape
    m = b * t
    h1, u = _ffn(x.reshape(m, d), w["ffn1_norm"], w["ffn1_w_in"], w["ffn1_w_out"], w["mix_norm"])
    outs_a = _proj_a(u, w["w_in"], w["q_norm"], batch=b, seq=t, prompt=prompt)
    outs_b = _proj_b(u, w["w_in"], w["k_norm"], batch=b, seq=t, prompt=prompt)
    z, gates = outs_a[0], outs_a[-1]
    z3 = z.reshape(b, t, POOL_WIDTH)
    if prompt:
        qs = outs_a[1:1 + N_GROUPS]
        kbs, vbs = outs_b[0:N_GROUPS], outs_b[N_GROUPS:2 * N_GROUPS]
        kfs, vfs = outs_b[2 * N_GROUPS:3 * N_GROUPS], outs_b[3 * N_GROUPS:]
        y_pool = _pool(z3, z3, w["pool_map"], w["pool_scale"], tp=512, pos0=0, zero_first_halo=True)
        new_pool = z3[:, t - POOL_STATE:]
        o_groups = [_band_attn(qs[g], kbs[g], vbs[g], g) for g in range(N_GROUPS)]
        new_kv = []
        for g in range(N_GROUPS):
            new_kv += [kfs[g].reshape(b, -1, HEADS, HEAD_DIM), vfs[g].reshape(b, -1, HEADS, HEAD_DIM)]
    else:
        qf = outs_a[1].reshape(N_GROUPS, b, t, HEADS, HEAD_DIM)
        kn = outs_b[0].reshape(N_GROUPS, b, t, HEADS, HEAD_DIM)
        vn = outs_b[1].reshape(N_GROUPS, b, t, HEADS, HEAD_DIM)
        halo = jnp.pad(pool_prev, ((0, 0), (POOL_HALO - POOL_STATE, 0), (0, 0)))
        y_pool = _pool(z3, halo, w["pool_map"], w["pool_scale"], tp=t, pos0=PAST_LEN, zero_first_halo=False)
        new_pool = jnp.concatenate([pool_prev, z3], axis=1)[:, t:]
        o_groups, new_kv = [], []
        for g in range(N_GROUPS):
            kc, vc = caches[2 * g], caches[2 * g + 1]
            o, k_new, v_new = _gather_attn(qf, kc, vc, kn, vn, g)
            o_groups.append(o.reshape(1, 1, m, ATTN_OUT_COLS))
            new_kv += [k_new, v_new]
    h2 = _merge(h1, y_pool.reshape(m, POOL_WIDTH), o_groups, gates,
                w["w_pool_out"], w["w_attn_out"], w["w_out"])
    h3 = _ffn(h2, w["ffn2_norm"], w["ffn2_w_in"], w["ffn2_w_out"])
    h4 = _ple(h3, p.reshape(m, -1), w["ple_norm"], w["ple_gate"], w["ple_proj"])
    return h4.reshape(b, t, d), new_pool, new_kv


def kernel(x_prompt, x_sample, p_prompt, p_sample, state_pool, cache_k0, cache_v0, cache_k1, cache_v1, cache_k2, cache_v2, ffn1_norm, ffn1_w_in, ffn1_w_out, mix_norm, w_in, q_norm, k_norm, pool_map, pool_scale, w_pool_out, w_attn_out, w_out, ffn2_norm, ffn2_w_in, ffn2_w_out, ple_norm, ple_gate, ple_proj):
    caches = (cache_k0, cache_v0, cache_k1, cache_v1, cache_k2, cache_v2)
    depth = ffn1_norm.shape[0]
    hp, hs = x_prompt, x_sample
    pool_p, pool_s, kv_p, kv_s = [], [], [], []
    for i in range(depth):
        row = lambda a: a[i][None, :]
        w = dict(
            ffn1_norm=row(ffn1_norm), ffn1_w_in=ffn1_w_in[i].astype(BF16), ffn1_w_out=ffn1_w_out[i].astype(BF16),
            mix_norm=row(mix_norm), q_norm=row(q_norm), k_norm=row(k_norm),
            w_in=w_in[i].astype(BF16),
            pool_map=pool_map[i].astype(BF16), pool_scale=row(pool_scale),
            w_pool_out=w_pool_out[i].astype(BF16), w_attn_out=w_attn_out[i].astype(BF16),
            w_out=w_out[i].astype(BF16),
            ffn2_norm=row(ffn2_norm), ffn2_w_in=ffn2_w_in[i].astype(BF16), ffn2_w_out=ffn2_w_out[i].astype(BF16),
            ple_norm=row(ple_norm), ple_gate=ple_gate[i].astype(BF16), ple_proj=ple_proj[i].astype(BF16),
        )
        hp, pp, kvp = _layer(hp, p_prompt[i], w, prompt=True)
        hs, ps, kvs = _layer(hs, p_sample[i], w, prompt=False, pool_prev=state_pool[i],
                             caches=tuple(c[i] for c in caches))
        pool_p.append(pp)
        pool_s.append(ps)
        kv_p.append(kvp)
        kv_s.append(kvs)
    stack = lambda rows: jnp.stack(rows)
    kp = [stack([r[j] for r in kv_p]) for j in range(2 * N_GROUPS)]
    ks = [stack([r[j] for r in kv_s]) for j in range(2 * N_GROUPS)]
    return (hp, hs, stack(pool_p), *kp, stack(pool_s), *ks)
```

```python
import functools
import math

import jax
import jax.numpy as jnp
from jax import lax
from jax.experimental import pallas as pl
from jax.experimental.pallas import tpu as pltpu

F32 = jnp.float32
BF16 = jnp.bfloat16

POOL_WINDOWS = (2, 4, 8, 16)
POOL_GROUP_WIDTH = 256
POOL_WIDTH = POOL_GROUP_WIDTH * len(POOL_WINDOWS)
POOL_STATE = max(POOL_WINDOWS) - 1
POOL_HALO = 16
POOL_PAD = 8
HEAD_DIM = 128
HEADS = 8
GROUP_WIDTH = HEADS * HEAD_DIM
ATTN_GROUPS = ((128, 1), (512, 4), (2048, 16))
DILS = tuple(d for _, d in ATTN_GROUPS)
N_GROUPS = len(ATTN_GROUPS)
REORDER_STRIDE = 4
REORDER_PASSES = tuple(round(math.log(d, REORDER_STRIDE)) for d in DILS)
SLAB_OF = tuple(sum(REORDER_PASSES[:g]) for g in range(N_GROUPS))
N_SLABS = sum(REORDER_PASSES)
BAND = 128
LSE_LANES = 128
ATTN_OUT_COLS = GROUP_WIDTH + LSE_LANES
ATTN_OUT_TILES = ATTN_OUT_COLS // 128
RMS_EPS = 1e-6
NEG_INF = -1e30
LOG2_E = 1.4426950408889634
LN_2 = 0.6931471805599453
Q_SCALE = HEAD_DIM ** -0.5 * LOG2_E
PAST_LEN = 16384

VMEM_LIMIT = 60 * 1024 * 1024


def _params(*semantics):
    return pltpu.CompilerParams(dimension_semantics=semantics, vmem_limit_bytes=VMEM_LIMIT)


def _resident(shape, index_map):
    return pl.BlockSpec(shape, index_map, pipeline_mode=pl.Buffered(1))


def _rms(x, g):
    return x * lax.rsqrt(jnp.mean(x * x, axis=-1, keepdims=True) + RMS_EPS) * g


def _row_tile(m, want):
    t = min(m, want)
    assert m % t == 0, (m, t)
    return t


def _head_cols(hh):
    return slice(hh * HEAD_DIM, (hh + 1) * HEAD_DIM)


def _ffn_kernel(*refs, emit_norm):
    if emit_norm:
        x_ref, g_ref, wa_ref, wb_ref, wo_ref, gn_ref, o_ref, u_ref = refs
        xn_ref = u_ref
    else:
        x_ref, g_ref, wa_ref, wb_ref, wo_ref, o_ref, xn_ref = refs
    k = pl.program_id(1)

    @pl.when(k == 0)
    def _():
        x = x_ref[...]
        xn_ref[...] = _rms(x, g_ref[...]).astype(BF16)
        o_ref[...] = x

    xn = xn_ref[...]
    a = jnp.dot(xn, wa_ref[...], preferred_element_type=F32)
    b = jnp.dot(xn, wb_ref[...], preferred_element_type=F32)
    hid = (0.5 * (a * jax.nn.sigmoid(a)) * b).astype(BF16)
    o_ref[...] += jnp.dot(hid, wo_ref[...], preferred_element_type=F32)

    if emit_norm:
        @pl.when(k == pl.num_programs(1) - 1)
        def _():
            u_ref[...] = _rms(o_ref[...], gn_ref[...]).astype(BF16)


def _ffn(x, g, w_in, w_out, g_next=None, *, tm=1024, tf=512):
    m, d = x.shape
    d_ff = w_out.shape[0]
    tm = _row_tile(m, tm)
    tf = _row_tile(d_ff, tf)
    nk = d_ff // tf
    emit_norm = g_next is not None
    row = pl.BlockSpec((tm, d), lambda i, k: (i, 0))
    vec = _resident((1, d), lambda i, k: (0, 0))
    in_specs = [row, vec,
                pl.BlockSpec((d, tf), lambda i, k: (0, k)),
                pl.BlockSpec((d, tf), lambda i, k: (0, k + nk)),
                pl.BlockSpec((tf, d), lambda i, k: (k, 0))]
    args = [x, g, w_in, w_in, w_out]
    out_specs, out_shape = row, jax.ShapeDtypeStruct((m, d), F32)
    if emit_norm:
        in_specs.append(vec)
        args.append(g_next)
        out_specs = [row, row]
        out_shape = [out_shape, jax.ShapeDtypeStruct((m, d), BF16)]
    return pl.pallas_call(
        functools.partial(_ffn_kernel, emit_norm=emit_norm),
        grid=(m // tm, nk),
        in_specs=in_specs,
        out_specs=out_specs,
        out_shape=out_shape,
        scratch_shapes=[] if emit_norm else [pltpu.VMEM((tm, d), BF16)],
        compiler_params=_params("parallel", "arbitrary"),
        name="ffn",
    )(*args)


MXU_COLS = 512


def _tile_heads(u_ref, w_ref):
    heads = []
    for c in range(0, GROUP_WIDTH, MXU_COLS):
        part = jnp.dot(u_ref[...], w_ref[:, c:c + MXU_COLS], preferred_element_type=F32)
        heads += [part[:, i:i + HEAD_DIM] for i in range(0, MXU_COLS, HEAD_DIM)]
    return heads


def _store_cols(ref, heads, dtype):
    for hh, x in enumerate(heads):
        ref[:, _head_cols(hh)] = x.astype(dtype)


def _store_head_rows(ref, heads):
    for hh, x in enumerate(heads):
        ref[pl.ds(hh, x.shape[0], stride=HEADS), :] = x


def _store_residue_major(ref, slab_ref, heads, dil):
    if dil == 1:
        _store_cols(ref.at[0], heads, BF16)
        return
    passes = slab_ref.shape[0]
    assert REORDER_STRIDE ** passes == dil
    for hh, x in enumerate(heads):
        slab_ref[0, hh] = x
    chunk, residues = heads[0].shape[0], [0]
    for p in range(passes):
        sub, weight = chunk // REORDER_STRIDE, REORDER_STRIDE ** p
        for c, res in enumerate(residues):
            for r in range(REORDER_STRIDE):
                for hh in range(HEADS):
                    rows = slab_ref[p, hh, pl.ds(c * chunk + r, sub, stride=REORDER_STRIDE), :]
                    if p + 1 < passes:
                        slab_ref[p + 1, hh, pl.ds(c * chunk + r * sub, sub), :] = rows
                    else:
                        ref[res + r * weight, :, _head_cols(hh)] = rows.astype(BF16)
        residues = [res + r * weight for res in residues for r in range(REORDER_STRIDE)]
        chunk = sub


_HEAVY_FIRST = tuple(sorted(range(N_GROUPS), key=lambda g: -DILS[g]))


def _proj_a_kernel(*refs, prompt, n_gate):
    u_ref, wz_ref, *refs = refs
    wq_refs, wg_refs, refs = refs[:N_GROUPS], refs[N_GROUPS:N_GROUPS + n_gate], refs[N_GROUPS + n_gate:]
    if prompt:
        qn_ref, z_ref, q0_ref, q1_ref, q2_ref, gate_ref, slab_ref = refs
        q_refs = (q0_ref, q1_ref, q2_ref)
    else:
        qn_ref, z_ref, q_ref, gate_ref = refs
    for g in _HEAVY_FIRST:
        q = [_rms(x, qn_ref[...]) * Q_SCALE for x in _tile_heads(u_ref, wq_refs[g])]
        if prompt:
            _store_residue_major(q_refs[g], slab_ref.at[pl.ds(SLAB_OF[g], REORDER_PASSES[g])], q, DILS[g])
        else:
            _store_head_rows(q_ref.at[g], q)
    for t, wg_ref in enumerate(wg_refs):
        gate = [jax.nn.sigmoid(x) for x in _tile_heads(u_ref, wg_ref)]
        _store_cols(gate_ref.at[:, t * GROUP_WIDTH:(t + 1) * GROUP_WIDTH], gate, BF16)
    _store_cols(z_ref, _tile_heads(u_ref, wz_ref), F32)


def _proj_b_kernel(*refs, prompt):
    u_ref, *refs = refs
    wk_refs, wv_refs, refs = refs[:N_GROUPS], refs[N_GROUPS:2 * N_GROUPS], refs[2 * N_GROUPS:]
    if prompt:
        (kn_ref, kb0_ref, kb1_ref, kb2_ref, vb0_ref, vb1_ref, vb2_ref,
         kf0_ref, kf1_ref, kf2_ref, vf0_ref, vf1_ref, vf2_ref, slab_ref) = refs
        kb_refs, vb_refs = (kb0_ref, kb1_ref, kb2_ref), (vb0_ref, vb1_ref, vb2_ref)
        kf_refs, vf_refs = (kf0_ref, kf1_ref, kf2_ref), (vf0_ref, vf1_ref, vf2_ref)
    else:
        kn_ref, kf_ref, vf_ref = refs
        kf_refs = [kf_ref.at[g] for g in range(N_GROUPS)]
        vf_refs = [vf_ref.at[g] for g in range(N_GROUPS)]
    tm = u_ref.shape[0]
    for g in _HEAVY_FIRST:
        k = [_rms(x, kn_ref[...]) for x in _tile_heads(u_ref, wk_refs[g])]
        v = _tile_heads(u_ref, wv_refs[g])
        kept = kf_refs[g].shape[0] // HEADS
        _store_head_rows(kf_refs[g], [x[tm - kept:] for x in k])
        _store_head_rows(vf_refs[g], [x[tm - kept:] for x in v])
        if prompt:
            slabs = pl.ds(SLAB_OF[g], REORDER_PASSES[g])
            _store_residue_major(kb_refs[g], slab_ref.at[0, slabs], k, DILS[g])
            _store_residue_major(vb_refs[g], slab_ref.at[1, slabs], v, DILS[g])


def _residue_specs(batch, seq, tm, dtype):
    per_seq = seq // tm
    specs, shapes = [], []
    for dil in DILS:
        specs.append(pl.BlockSpec((None, dil, tm // dil, GROUP_WIDTH),
                                  lambda i: (i // per_seq, 0, i % per_seq, 0)))
        shapes.append(jax.ShapeDtypeStruct((batch, dil, seq // dil, GROUP_WIDTH), dtype))
    return specs, shapes


def _head_rows_spec(n_arrays, m, tm):
    return (pl.BlockSpec((n_arrays, tm * HEADS, HEAD_DIM), lambda i: (0, i, 0)),
            jax.ShapeDtypeStruct((n_arrays, m * HEADS, HEAD_DIM), F32))


def _suffix_head_rows_spec(batch, seq, tm, keep):
    rows = min(keep, tm)
    assert keep % rows == 0
    blocks, per_seq = keep // rows, seq // tm
    spec = pl.BlockSpec((None, None, rows * HEADS, HEAD_DIM),
                        lambda i: (i // per_seq, jnp.maximum(i % per_seq - (per_seq - blocks), 0), 0, 0))
    return spec, jax.ShapeDtypeStruct((batch, blocks, rows * HEADS, HEAD_DIM), F32)


def _weight_tiles(w, tiles):
    d = w.shape[0]
    return [_resident((d, GROUP_WIDTH), functools.partial(lambda t, i: (0, t), t)) for t in tiles]


def _proj_a(u, w, qn, *, batch, seq, prompt, tm=256):
    m, d = u.shape
    gate0 = 1 + 3 * N_GROUPS
    tiles = [0, *range(1, 1 + N_GROUPS), *range(gate0, w.shape[1] // GROUP_WIDTH)]
    n_gate = len(tiles) - 1 - N_GROUPS
    tm = _row_tile(seq if prompt else m, tm)
    fixed = lambda i: (0, 0)
    rows = lambda width: pl.BlockSpec((tm, width), lambda i: (i, 0))
    scratch = []
    if prompt:
        q_specs, q_shapes = _residue_specs(batch, seq, tm, BF16)
        scratch = [pltpu.VMEM((N_SLABS, HEADS, tm, HEAD_DIM), F32)]
    else:
        spec, shape = _head_rows_spec(N_GROUPS, m, tm)
        q_specs, q_shapes = [spec], [shape]
    return pl.pallas_call(
        functools.partial(_proj_a_kernel, prompt=prompt, n_gate=n_gate),
        grid=(m // tm,),
        in_specs=[rows(d), *_weight_tiles(w, tiles), _resident((1, HEAD_DIM), fixed)],
        out_specs=[rows(GROUP_WIDTH), *q_specs, rows(n_gate * GROUP_WIDTH)],
        out_shape=[jax.ShapeDtypeStruct((m, GROUP_WIDTH), F32), *q_shapes,
                   jax.ShapeDtypeStruct((m, n_gate * GROUP_WIDTH), BF16)],
        scratch_shapes=scratch,
        compiler_params=_params("parallel"),
        name="proj_a",
    )(u, *[w] * len(tiles), qn)


def _proj_b(u, w, kn, *, batch, seq, prompt, tm=256):
    m, d = u.shape
    tiles = range(1 + N_GROUPS, 1 + 3 * N_GROUPS)
    tm = _row_tile(seq if prompt else m, tm)
    fixed = lambda i: (0, 0)
    scratch = []
    if prompt:
        kb_specs, kb_shapes = _residue_specs(batch, seq, tm, BF16)
        kf = [_suffix_head_rows_spec(batch, seq, tm, min(window, seq)) for window, _ in ATTN_GROUPS]
        kf_specs, kf_shapes = [s for s, _ in kf], [s for _, s in kf]
        out_specs = [*kb_specs, *kb_specs, *kf_specs, *kf_specs]
        out_shape = [*kb_shapes, *kb_shapes, *kf_shapes, *kf_shapes]
        scratch = [pltpu.VMEM((2, N_SLABS, HEADS, tm, HEAD_DIM), F32)]
    else:
        spec, shape = _head_rows_spec(N_GROUPS, m, tm)
        out_specs, out_shape = [spec, spec], [shape, shape]
    return pl.pallas_call(
        functools.partial(_proj_b_kernel, prompt=prompt),
        grid=(m // tm,),
        in_specs=[pl.BlockSpec((tm, d), lambda i: (i, 0)), *_weight_tiles(w, tiles),
                  _resident((1, HEAD_DIM), fixed)],
        out_specs=out_specs,
        out_shape=out_shape,
        scratch_shapes=scratch,
        compiler_params=_params("arbitrary"),
        name="proj_b",
    )(u, *[w] * len(tiles), kn)


def _pool_kernel(z_ref, halo_ref, pm_ref, ps_ref, y_ref, ext_ref, lvl_ref, *, tp, pos0, zero_first_halo):
    i = pl.program_id(1)
    halo = halo_ref[...]
    if zero_first_halo:
        halo = jnp.where(i == 0, 0.0, halo)
    top, n = POOL_PAD + POOL_HALO, POOL_HALO + tp
    pad = jnp.zeros((POOL_PAD, ext_ref.shape[1]), F32)
    ext_ref[0:POOL_PAD, :] = pad
    ext_ref[POOL_PAD:top, :] = halo
    ext_ref[top:, :] = z_ref[...]
    lvl_ref[0, 0:POOL_PAD, :] = pad
    lvl_ref[1, 0:POOL_PAD, :] = pad
    pos = pos0 + i * tp + lax.broadcasted_iota(jnp.int32, (tp, 1), 0)
    for gi, w in enumerate(POOL_WINDOWS):
        cols = slice(gi * POOL_GROUP_WIDTH, (gi + 1) * POOL_GROUP_WIDTH)
        zc = ext_ref[pl.ds(top, tp), cols]
        src, slot, span = ext_ref, 0, 1
        while span < w:
            both = src[pl.ds(POOL_PAD, n), cols] + src[pl.ds(POOL_PAD - span, n), cols]
            span *= 2
            if span < w:
                lvl_ref[slot, pl.ds(POOL_PAD, n), cols] = both
                src, slot = lvl_ref.at[slot], 1 - slot
        wsum = both[POOL_HALO:]
        cnt = jnp.minimum(w, pos + 1).astype(F32)
        pooled = (wsum / cnt - zc).astype(BF16)
        y = jnp.dot(pooled, pm_ref[gi], preferred_element_type=F32) * ps_ref[:, cols]
        y_ref[:, cols] = y.astype(BF16)


def _pool(z, halo, pool_map, pool_scale, *, tp, pos0, zero_first_halo):
    b, t, p = z.shape
    tp = _row_tile(t, tp)
    hb = tp // POOL_HALO
    if halo is z:
        halo_map = lambda bi, i: (bi, jnp.maximum(i * hb - 1, 0), 0)
    else:
        assert t == tp and halo.shape == (b, POOL_HALO, p)
        halo_map = lambda bi, i: (bi, 0, 0)
    return pl.pallas_call(
        functools.partial(_pool_kernel, tp=tp, pos0=pos0, zero_first_halo=zero_first_halo),
        grid=(b, t // tp),
        in_specs=[
            pl.BlockSpec((None, tp, p), lambda bi, i: (bi, i, 0)),
            pl.BlockSpec((None, POOL_HALO, p), halo_map),
            _resident(pool_map.shape, lambda bi, i: (0, 0, 0)),
            _resident((1, p), lambda bi, i: (0, 0)),
        ],
        out_specs=pl.BlockSpec((None, tp, p), lambda bi, i: (bi, i, 0)),
        out_shape=jax.ShapeDtypeStruct((b, t, p), BF16),
        scratch_shapes=[pltpu.VMEM((POOL_PAD + POOL_HALO + tp, p), F32),
                        pltpu.VMEM((2, POOL_PAD + POOL_HALO + tp, p), F32)],
        compiler_params=_params("parallel", "arbitrary"),
        name="pool",
    )(z, halo, pool_map, pool_scale)


def _band_attn_kernel(q_ref, k_ref, v_ref, o_ref, *, nblk, kw, dil):
    rb = q_ref.shape[0]
    i = pl.program_id(2)

    def body(it, carry):
        rl, nl = it // nblk, it % nblk
        n = i * nblk + nl
        if kw == BAND:
            kstart, off = 0, 0
        else:
            kstart = pl.multiple_of(jnp.maximum(n - 1, 0) * BAND, BAND)
            off = n * BAND - kstart
        qstart = pl.multiple_of(nl * BAND, BAND)
        steps = (off + lax.broadcasted_iota(jnp.int32, (BAND, kw), 0)
                 - lax.broadcasted_iota(jnp.int32, (BAND, kw), 1))
        valid = jnp.abs(steps - BAND // 2) <= BAND // 2
        neg_dist = jnp.where(valid, (steps * dil).astype(F32) * -LOG2_E, NEG_INF)
        lane = lax.broadcasted_iota(jnp.int32, (BAND, LSE_LANES), 1)
        lse_tile = jnp.zeros((BAND, LSE_LANES), F32)
        scores = []
        for hh in range(HEADS):
            qh = q_ref[rl, pl.ds(qstart, BAND), _head_cols(hh)]
            kh = k_ref[rl, pl.ds(kstart, kw), _head_cols(hh)]
            scores.append(lax.dot_general(qh, kh, (((1,), (1,)), ((), ())), preferred_element_type=F32))
        probs, dens = [], []
        for hh in range(HEADS):
            logits = scores[hh] + (2.0 ** -(hh + 1)) * neg_dist
            mx = jnp.max(logits, axis=-1, keepdims=True)
            p = jnp.exp2(logits - mx)
            den = jnp.sum(p, axis=-1, keepdims=True)
            probs.append(p.astype(BF16))
            dens.append(den)
            lse_tile = jnp.where(lane == hh, (mx + jnp.log2(den)) * LN_2, lse_tile)
        for hh in range(HEADS):
            vh = v_ref[rl, pl.ds(kstart, kw), _head_cols(hh)]
            o = jnp.dot(probs[hh], vh, preferred_element_type=F32) / dens[hh]
            o_ref[rl, pl.ds(qstart, BAND), _head_cols(hh)] = o
        o_ref[rl, pl.ds(qstart, BAND), GROUP_WIDTH:] = lse_tile
        return carry

    lax.fori_loop(0, rb * nblk, body, 0, unroll=True)


def _band_attn(q, k, v, g, *, blocks_per_step=16):
    batch, dil, mr, _ = q.shape
    assert dil == DILS[g] and mr % BAND == 0
    tq = min(mr, blocks_per_step * BAND)
    nblk = tq // BAND
    rb = min(dil, blocks_per_step // nblk)
    kw = min(mr, 2 * BAND)
    return pl.pallas_call(
        functools.partial(_band_attn_kernel, nblk=nblk, kw=kw, dil=dil),
        grid=(batch, dil // rb, mr // tq),
        in_specs=[
            pl.BlockSpec((None, rb, tq, GROUP_WIDTH), lambda b, r, i: (b, r, i, 0)),
            pl.BlockSpec((None, rb, mr, GROUP_WIDTH), lambda b, r, i: (b, r, 0, 0)),
            pl.BlockSpec((None, rb, mr, GROUP_WIDTH), lambda b, r, i: (b, r, 0, 0)),
        ],
        out_specs=pl.BlockSpec((None, rb, tq, ATTN_OUT_COLS), lambda b, r, i: (b, r, i, 0)),
        out_shape=jax.ShapeDtypeStruct((batch, dil, mr, ATTN_OUT_COLS), F32),
        compiler_params=_params("parallel", "parallel", "arbitrary"),
        name="band_attn_g%d" % g,
    )(q, k, v)


def _roll_copies(b, cache_ref, new_ref, out_hbm, sems, which):
    length, t_new = cache_ref.shape[1], new_ref.shape[2]
    kept = length - t_new
    return (pltpu.make_async_copy(cache_ref.at[0, pl.ds(t_new, kept)],
                                  out_hbm.at[b, pl.ds(0, kept)], sems.at[which, 0]),
            pltpu.make_async_copy(new_ref.at[0, 0], out_hbm.at[b, pl.ds(kept, t_new)], sems.at[which, 1]))


def _gather_attn_kernel(q_ref, kc_ref, vc_ref, kn_ref, vn_ref, o_ref, ko_hbm, vo_hbm, oh_ref, lh_ref, sems,
                        *, dil, t_new):
    b = pl.program_id(0)
    copies = (*_roll_copies(b, kc_ref, kn_ref, ko_hbm, sems, 0),
              *_roll_copies(b, vc_ref, vn_ref, vo_hbm, sems, 1))
    for copy in copies:
        copy.start()
    length = kc_ref.shape[1]
    head = lax.broadcasted_iota(jnp.int32, (HEADS, 1), 0)
    slope = lax.bitcast_convert_type((126 - head) << 23, F32) * LOG2_E
    for i in range(t_new):
        qi = q_ref[0, 0, i]
        n_new = i // dil + 1
        n_old = BAND + 1 - n_new
        first = length + i - BAND * dil
        assert first >= 0 and first + (n_old - 1) * dil < length
        kc = kc_ref[0, pl.ds(first, n_old, stride=dil)]
        vc = vc_ref[0, pl.ds(first, n_old, stride=dil)]
        steps = BAND - lax.broadcasted_iota(jnp.int32, (n_old, 1, 1), 0)
        sc = jnp.sum(kc * qi[None], axis=-1, keepdims=True) - slope[None] * (steps * dil).astype(F32)
        sn = [jnp.sum(kn_ref[0, 0, i - jj * dil] * qi, axis=-1, keepdims=True) - slope * float(jj * dil)
              for jj in range(n_new)]
        mx = jnp.max(sc, axis=0)
        for s in sn:
            mx = jnp.maximum(mx, s)
        pc = jnp.exp2(sc - mx[None])
        den = jnp.sum(pc, axis=0)
        acc = jnp.sum(pc * vc, axis=0)
        for jj, s in enumerate(sn):
            pn = jnp.exp2(s - mx)
            den = den + pn
            acc = acc + pn * vn_ref[0, 0, i - jj * dil]
        oh_ref[i] = acc / den
        lh_ref[i] = jnp.broadcast_to((mx + jnp.log2(den)) * LN_2, (HEADS, LSE_LANES))
    lane = lax.broadcasted_iota(jnp.int32, (t_new, LSE_LANES), 1)
    lse_tile = jnp.zeros((t_new, LSE_LANES), F32)
    for hh in range(HEADS):
        o_ref[:, _head_cols(hh)] = oh_ref[:, hh, :]
        lse_tile = jnp.where(lane == hh, lh_ref[:, hh, :], lse_tile)
    o_ref[:, GROUP_WIDTH:] = lse_tile
    for copy in copies:
        copy.wait()


def _gather_attn(q, kc, vc, kn, vn, g):
    win, dil = ATTN_GROUPS[g]
    b, length = kc.shape[:2]
    t_new = q.shape[2]
    assert length == win and t_new <= dil * BAND
    new = pl.BlockSpec((1, 1, t_new, HEADS, HEAD_DIM), lambda bi: (g, bi, 0, 0, 0))
    cache = pl.BlockSpec((1, length, HEADS, HEAD_DIM), lambda bi: (bi, 0, 0, 0))
    whole_hbm = pl.BlockSpec(memory_space=pl.ANY)
    return pl.pallas_call(
        functools.partial(_gather_attn_kernel, dil=dil, t_new=t_new),
        grid=(b,),
        in_specs=[new, cache, cache, new, new],
        out_specs=[pl.BlockSpec((None, t_new, ATTN_OUT_COLS), lambda bi: (bi, 0, 0)), whole_hbm, whole_hbm],
        out_shape=[jax.ShapeDtypeStruct((b, t_new, ATTN_OUT_COLS), F32),
                   jax.ShapeDtypeStruct(kc.shape, F32), jax.ShapeDtypeStruct(vc.shape, F32)],
        scratch_shapes=[pltpu.VMEM((t_new, HEADS, HEAD_DIM), F32),
                        pltpu.VMEM((t_new, HEADS, LSE_LANES), F32),
                        pltpu.SemaphoreType.DMA((2, 2))],
        compiler_params=_params("arbitrary"),
        name="gather_attn_g%d" % g,
    )(q, kc, vc, kn, vn)


def _merge_kernel(h_ref, y_ref, o0_ref, o1_ref, o2_ref, gate_ref, wp_ref, wa_ref, wo_ref, out_ref,
                  oa_ref, mix_ref, tok_ref, *, dils):
    o_refs = (o0_ref, o1_ref, o2_ref)
    tm, d = out_ref.shape
    s = pl.program_id(0)

    @pl.when(s == 0)
    def _():
        oa_ref[1] = jnp.zeros(oa_ref.shape[1:], BF16)
        mix_ref[1] = jnp.zeros(mix_ref.shape[1:], BF16)

    def tile(gi, c):
        if dils[gi] == 1:
            return o_refs[gi][0, :, c * 128:(c + 1) * 128]
        return tok_ref[gi, c]

    def step(fill, drain):
        for gi, dil in enumerate(dils):
            if dil == 1:
                continue
            n = tm // dil
            for r in range(dil):
                for c in range(ATTN_OUT_TILES):
                    tok_ref[gi, c, pl.ds(r, n, stride=dil), :] = o_refs[gi][r, :, c * 128:(c + 1) * 128]
        lses = [tile(gi, HEADS) for gi in range(N_GROUPS)]
        mx = jnp.maximum(jnp.maximum(lses[0], lses[1]), lses[2])
        es = [jnp.exp(l - mx) for l in lses]
        inv = 1.0 / (es[0] + es[1] + es[2])
        wts = [e * inv for e in es]
        for hh in range(HEADS):
            acc = wts[0][:, hh:hh + 1] * tile(0, hh)
            for gi in range(1, N_GROUPS):
                acc = acc + wts[gi][:, hh:hh + 1] * tile(gi, hh)
            oa_ref[fill, :, _head_cols(hh)] = acc.astype(BF16)

        a_attn = jnp.dot(oa_ref[drain], wa_ref[...], preferred_element_type=F32)
        a_pool = jnp.dot(y_ref[...], wp_ref[...], preferred_element_type=F32)
        mix = gate_ref[:, :d].astype(F32) * a_pool + gate_ref[:, d:].astype(F32) * a_attn
        mix_ref[fill] = mix.astype(BF16)

        out_ref[...] = h_ref[...] + jnp.dot(mix_ref[drain], wo_ref[...], preferred_element_type=F32)

    pl.when(s % 2 == 0)(functools.partial(step, 0, 1))
    pl.when(s % 2 == 1)(functools.partial(step, 1, 0))


def _merge(h, y_pool, o_groups, gates, w_pool_out, w_attn_out, w_out, *, tm=256):
    m, d = h.shape
    dils = tuple(o.shape[1] for o in o_groups)
    seq = o_groups[0].shape[1] * o_groups[0].shape[2]
    tm = _row_tile(seq, tm)
    per_seq, n_tiles = seq // tm, m // tm
    last = n_tiles - 1
    row1 = lambda s: (jnp.clip(s - 1, 0, last), 0)
    row2 = lambda s: (jnp.maximum(s - 2, 0), 0)
    fixed = lambda s: (0, 0)

    def attn_tile(s):
        i = jnp.minimum(s, last)
        return (i // per_seq, 0, i % per_seq, 0)

    o_specs = [pl.BlockSpec((None, dil, tm // dil, ATTN_OUT_COLS), attn_tile) for dil in dils]
    return pl.pallas_call(
        functools.partial(_merge_kernel, dils=dils),
        grid=(n_tiles + 2,),
        in_specs=[
            pl.BlockSpec((tm, d), row2),
            pl.BlockSpec((tm, POOL_WIDTH), row1),
            *o_specs,
            pl.BlockSpec((tm, 2 * d), row1),
            _resident(w_pool_out.shape, fixed),
            _resident(w_attn_out.shape, fixed),
            _resident(w_out.shape, fixed),
        ],
        out_specs=pl.BlockSpec((tm, d), row2),
        out_shape=jax.ShapeDtypeStruct((m, d), F32),
        scratch_shapes=[pltpu.VMEM((2, tm, GROUP_WIDTH), BF16),
                        pltpu.VMEM((2, tm, d), BF16),
                        pltpu.VMEM((N_GROUPS, ATTN_OUT_TILES, tm, 128), F32)],
        compiler_params=_params("arbitrary"),
        name="merge",
    )(h, y_pool, *o_groups, gates, w_pool_out, w_attn_out, w_out)


def _ple_kernel(h_ref, p_ref, g_ref, wg_ref, wp_ref, out_ref):
    h = h_ref[...]
    hn = _rms(h, g_ref[...]).astype(BF16)
    gate = jax.nn.sigmoid(jnp.dot(hn, wg_ref[...], preferred_element_type=F32))
    emb = jnp.dot(p_ref[...].astype(BF16), wp_ref[...], preferred_element_type=F32)
    out_ref[...] = h + gate * emb


def _ple(h, p, g, w_gate, w_proj, *, tm=1024):
    m, d = h.shape
    tm = _row_tile(m, tm)
    row = lambda i: (i, 0)
    fixed = lambda i: (0, 0)
    return pl.pallas_call(
        _ple_kernel,
        grid=(m // tm,),
        in_specs=[
            pl.BlockSpec((tm, d), row),
            pl.BlockSpec((tm, p.shape[1]), row),
            _resident((1, d), fixed),
            _resident(w_gate.shape, fixed),
            _resident(w_proj.shape, fixed),
        ],
        out_specs=pl.BlockSpec((tm, d), row),
        out_shape=jax.ShapeDtypeStruct((m, d), F32),
        compiler_params=_params("parallel"),
        name="ple",
    )(h, p, g, w_gate, w_proj)


def _layer(x, p, w, *, prompt, pool_prev=None, caches=None):
    b, t, d = x.shape
    m = b * t
    h1, u = _ffn(x.reshape(m, d), w["ffn1_norm"], w["ffn1_w_in"], w["ffn1_w_out"], w["mix_norm"])
    outs_a = _proj_a(u, w["w_in"], w["q_norm"], batch=b, seq=t, prompt=prompt)
    outs_b = _proj_b(u, w["w_in"], w["k_norm"], batch=b, seq=t, prompt=prompt)
    z, gates = outs_a[0], outs_a[-1]
    z3 = z.reshape(b, t, POOL_WIDTH)
    if prompt:
        qs = outs_a[1:1 + N_GROUPS]
        kbs, vbs = outs_b[0:N_GROUPS], outs_b[N_GROUPS:2 * N_GROUPS]
        kfs, vfs = outs_b[2 * N_GROUPS:3 * N_GROUPS], outs_b[3 * N_GROUPS:]
        y_pool = _pool(z3, z3, w["pool_map"], w["pool_scale"], tp=512, pos0=0, zero_first_halo=True)
        new_pool = z3[:, t - POOL_STATE:]
        o_groups = [_band_attn(qs[g], kbs[g], vbs[g], g) for g in range(N_GROUPS)]
        new_kv = []
        for g in range(N_GROUPS):
            new_kv += [kfs[g].reshape(b, -1, HEADS, HEAD_DIM), vfs[g].reshape(b, -1, HEADS, HEAD_DIM)]
    else:
        qf = outs_a[1].reshape(N_GROUPS, b, t, HEADS, HEAD_DIM)
        kn = outs_b[0].reshape(N_GROUPS, b, t, HEADS, HEAD_DIM)
        vn = outs_b[1].reshape(N_GROUPS, b, t, HEADS, HEAD_DIM)
        halo = jnp.pad(pool_prev, ((0, 0), (POOL_HALO - POOL_STATE, 0), (0, 0)))
        y_pool = _pool(z3, halo, w["pool_map"], w["pool_scale"], tp=t, pos0=PAST_LEN, zero_first_halo=False)
        new_pool = jnp.concatenate([pool_prev, z3], axis=1)[:, t:]
        o_groups, new_kv = [], []
        for g in range(N_GROUPS):
            kc, vc = caches[2 * g], caches[2 * g + 1]
            o, k_new, v_new = _gather_attn(qf, kc, vc, kn, vn, g)
            o_groups.append(o.reshape(1, 1, m, ATTN_OUT_COLS))
            new_kv += [k_new, v_new]
    h2 = _merge(h1, y_pool.reshape(m, POOL_WIDTH), o_groups, gates,
                w["w_pool_out"], w["w_attn_out"], w["w_out"])
    h3 = _ffn(h2, w["ffn2_norm"], w["ffn2_w_in"], w["ffn2_w_out"])
    h4 = _ple(h3, p.reshape(m, -1), w["ple_norm"], w["ple_gate"], w["ple_proj"])
    return h4.reshape(b, t, d), new_pool, new_kv


def kernel(x_prompt, x_sample, p_prompt, p_sample, state_pool, cache_k0, cache_v0, cache_k1, cache_v1, cache_k2, cache_v2, ffn1_norm, ffn1_w_in, ffn1_w_out, mix_norm, w_in, q_norm, k_norm, pool_map, pool_scale, w_pool_out, w_attn_out, w_out, ffn2_norm, ffn2_w_in, ffn2_w_out, ple_norm, ple_gate, ple_proj):
    caches = (cache_k0, cache_v0, cache_k1, cache_v1, cache_k2, cache_v2)
    depth = ffn1_norm.shape[0]
    hp, hs = x_prompt, x_sample
    pool_p, pool_s, kv_p, kv_s = [], [], [], []
    for i in range(depth):
        row = lambda a: a[i][None, :]
        w = dict(
            ffn1_norm=row(ffn1_norm), ffn1_w_in=ffn1_w_in[i].astype(BF16), ffn1_w_out=ffn1_w_out[i].astype(BF16),
            mix_norm=row(mix_norm), q_norm=row(q_norm), k_norm=row(k_norm),
            w_in=w_in[i].astype(BF16),
            pool_map=pool_map[i].astype(BF16), pool_scale=row(pool_scale),
            w_pool_out=w_pool_out[i].astype(BF16), w_attn_out=w_attn_out[i].astype(BF16),
            w_out=w_out[i].astype(BF16),
            ffn2_norm=row(ffn2_norm), ffn2_w_in=ffn2_w_in[i].astype(BF16), ffn2_w_out=ffn2_w_out[i].astype(BF16),
            ple_norm=row(ple_norm), ple_gate=ple_gate[i].astype(BF16), ple_proj=ple_proj[i].astype(BF16),
        )
        hp, pp, kvp = _layer(hp, p_prompt[i], w, prompt=True)
        hs, ps, kvs = _layer(hs, p_sample[i], w, prompt=False, pool_prev=state_pool[i],
                             caches=tuple(c[i] for c in caches))
        pool_p.append(pp)
        pool_s.append(ps)
        kv_p.append(kvp)
        kv_s.append(kvs)
    stack = lambda rows: jnp.stack(rows)
    kp = [stack([r[j] for r in kv_p]) for j in range(2 * N_GROUPS)]
    ks = [stack([r[j] for r in kv_s]) for j in range(2 * N_GROUPS)]
    return (hp, hs, stack(pool_p), *kp, stack(pool_s), *ks)
```
